```python
import math
import jax, jax.numpy as jnp
from jax import lax
import numpy as np


D_MODEL = 2048
BATCH = 16
SEQ = 2048
DEPTH = 4

GRID_W = 64
CTX_LEN = 256
N_MIXERS = 3
Q_BLOCK = 128
ROPE_THETA = 10000.0
NORM_EPS = 1e-6

D_FF = 4 * D_MODEL

MLA_NOPE = 128
MLA_ROPE = 64
MLA_V = 128
MLA_HEADS = D_MODEL // 128
MLA_Q_RANK = 768
MLA_KV_RANK = 512

HY_SHORT = 3
HY_EMB_DIM = 33
HY_FILT_ORDER = 64
HY_TARGET = 1e-2
HY_FAST_PCT = 0.3
HY_SLOW_PCT = 1.5

DF_HEAD_DIM = 128
DF_HEADS = D_MODEL // (2 * DF_HEAD_DIM)
DF_SUBLN_EPS = 1e-5

N_LAYERS_A = (DEPTH + N_MIXERS - 1) // N_MIXERS
N_LAYERS_B = (DEPTH + N_MIXERS - 2) // N_MIXERS
N_LAYERS_C = (DEPTH + N_MIXERS - 3) // N_MIXERS

kernel_name = 'hybrid_mla_hyena_diffattn_dit'


def rmsnorm(x, g, eps=NORM_EPS):
    xf = x.astype(jnp.float32)
    y = xf * lax.rsqrt(jnp.mean(xf * xf, axis=-1, keepdims=True) + eps)
    return (y * g.astype(jnp.float32)).astype(x.dtype)


def modulate(x, shift, scale):
    return x * (1 + scale) + shift


def axial_rope_tables(L, rot_dim):
    rows = L // GRID_W
    row = jnp.repeat(jnp.arange(rows, dtype=jnp.float32), GRID_W)
    col = jnp.tile(jnp.arange(GRID_W, dtype=jnp.float32), rows)
    pos = jnp.stack([row, col], axis=-1)
    n_freq = rot_dim // 4
    inv_freq = ROPE_THETA ** (-jnp.arange(n_freq, dtype=jnp.float32) / n_freq)
    ang = pos[:, :, None, None] * inv_freq
    ang = jnp.broadcast_to(ang, (L, 2, 2, n_freq)).reshape(L, rot_dim)
    return jnp.cos(ang), jnp.sin(ang)


def apply_axial_rope(x, cos, sin):
    R = x.shape[-1]
    bshape = (cos.shape[0],) + (1,) * (x.ndim - 3) + (R,)
    cos = cos.reshape(bshape).astype(x.dtype)
    sin = sin.reshape(bshape).astype(x.dtype)
    xs = x.reshape(*x.shape[:-1], 2, 2, R // 4)
    rot = jnp.stack([-xs[..., 1, :], xs[..., 0, :]], axis=-2).reshape(x.shape)
    return x * cos + rot * sin


def attend(q, k, v, scale):
    s = jnp.einsum('bqhd,bshd->bhqs', q, k).astype(jnp.float32) * scale
    p = jax.nn.softmax(s, axis=-1).astype(v.dtype)
    return jnp.einsum('bhqs,bshd->bqhd', p, v)


def diff_attend(q, k, v, lam, scale):
    s = jnp.einsum('bqhcd,bshcd->bhcqs', q, k).astype(jnp.float32) * scale
    p = jax.nn.softmax(s, axis=-1)
    a = p[:, :, 0] - lam * p[:, :, 1]
    return jnp.einsum('bhqs,bshd->bqhd', a.astype(v.dtype), v)


def sweep_query_blocks(block_fn, *qs):
    B, L = qs[0].shape[:2]
    nb = L // Q_BLOCK
    blocks = tuple(jnp.moveaxis(q.reshape(B, nb, Q_BLOCK, *q.shape[2:]), 1, 0) for q in qs)
    out = lax.map(lambda qb: block_fn(*qb), blocks)
    return jnp.moveaxis(out, 0, 1).reshape(B, L, *out.shape[3:])


def mla_mixer(uc, ul, w_dq, q_norm_g, w_uq, w_dkv, kv_norm_g, w_ukv, w_o, cos, sin, need_ctx):
    B, C, _ = uc.shape
    L = ul.shape[1]
    S = C + L
    u = jnp.concatenate([uc, ul], axis=1)
    ckv_full = u @ w_dkv
    ckv = rmsnorm(ckv_full[..., :MLA_KV_RANK], kv_norm_g)
    kv = (ckv @ w_ukv).reshape(B, S, MLA_HEADS, MLA_NOPE + MLA_V)
    k_nope, v = kv[..., :MLA_NOPE], kv[..., MLA_NOPE:]
    k_rope = ckv_full[..., None, MLA_KV_RANK:]
    k_rope = jnp.concatenate([k_rope[:, :C], apply_axial_rope(k_rope[:, C:], cos, sin)], axis=1)
    k = jnp.concatenate([k_nope, jnp.broadcast_to(k_rope, (B, S, MLA_HEADS, MLA_ROPE))], axis=-1)
    scale = (MLA_NOPE + MLA_ROPE) ** -0.5

    def queries(us):
        q = rmsnorm(us @ w_dq, q_norm_g) @ w_uq
        return q.reshape(B, us.shape[1], MLA_HEADS, MLA_NOPE + MLA_ROPE)

    ql = queries(ul)
    ql = jnp.concatenate([ql[..., :MLA_NOPE], apply_axial_rope(ql[..., MLA_NOPE:], cos, sin)], axis=-1)
    ol = sweep_query_blocks(lambda qb: attend(qb, k, v, scale), ql)
    yl = ol.reshape(B, L, MLA_HEADS * MLA_V) @ w_o
    if not need_ctx:
        return None, yl
    oc = attend(queries(uc), k[:, :C], v[:, :C], scale)
    return oc.reshape(B, C, MLA_HEADS * MLA_V) @ w_o, yl


def short_conv(u, w, b):
    L = u.shape[1]
    pad = HY_SHORT // 2
    up = jnp.pad(u, ((0, 0), (pad, pad), (0, 0)))
    return sum(up[:, j:j + L] * w[j] for j in range(HY_SHORT)) + b


def hyena_filters(L, w1, b1, w2, b2, w3, b3, freq, w_out):
    t = jnp.linspace(0.0, 1.0, L, dtype=jnp.float32)[:, None]
    bands = (HY_EMB_DIM - 1) // 2
    w = 2.0 * math.pi * jnp.arange(L, dtype=jnp.float32)[:, None] / L
    f = jnp.linspace(1e-4, bands - 1, bands, dtype=jnp.float32)
    feats = jnp.concatenate([t, jnp.cos(f * w), -jnp.sin(f * w)], axis=-1)
    h = jnp.sin(freq[0] * (feats @ w1 + b1))
    h = jnp.sin(freq[1] * (h @ w2 + b2))
    h = jnp.sin(freq[2] * (h @ w3 + b3))
    h = (h @ w_out).astype(jnp.float32)
    deltas = jnp.abs(jnp.linspace(math.log(HY_TARGET) / HY_SLOW_PCT, math.log(HY_TARGET) / HY_FAST_PCT,
                                  D_MODEL, dtype=jnp.float32))
    decay = jnp.exp(-t * deltas)
    return h[:, :D_MODEL] * decay, h[:, D_MODEL:] * decay


def bidir_long_conv(u, h_fwd, h_bwd):
    L = u.shape[1]
    n = 2 * L
    k = jnp.concatenate([h_fwd, jnp.zeros_like(h_fwd[:1]), h_bwd[:0:-1]], axis=0)
    kf = jnp.fft.rfft(k, n=n, axis=0)
    uf = jnp.fft.rfft(u.astype(jnp.float32), n=n, axis=1)
    y = jnp.fft.irfft(uf * kf, n=n, axis=1)[:, :L]
    return y.astype(u.dtype)


def hyena_mixer(uc, ul, w_in, b_in, conv_w, conv_b, f_w1, f_b1, f_w2, f_b2, f_w3, f_b3, f_freq, f_wout,
                f_bias, w_out, b_out, need_ctx):
    def operator(u):
        L = u.shape[1]
        z = short_conv(u @ w_in + b_in, conv_w, conv_b)
        x0, x1, v = jnp.split(z, 3, axis=-1)
        h_f, h_b = hyena_filters(L, f_w1, f_b1, f_w2, f_b2, f_w3, f_b3, f_freq, f_wout)
        g = v * x1
        y = (bidir_long_conv(g, h_f, h_b) + g * f_bias) * x0
        return y @ w_out + b_out

    yl = operator(ul)
    if not need_ctx:
        return None, yl
    return operator(uc), yl


def diff_mixer(uc, ul, w_qkv, lambdas, subln_g, w_o, lambda_init, cos, sin, need_ctx):
    B, C, _ = uc.shape
    L = ul.shape[1]
    S = C + L
    qkv = jnp.concatenate([uc, ul], axis=1) @ w_qkv
    q, k, v = jnp.split(qkv, 3, axis=-1)
    q = q.reshape(B, S, DF_HEADS, 2, DF_HEAD_DIM)
    k = k.reshape(B, S, DF_HEADS, 2, DF_HEAD_DIM)
    v = v.reshape(B, S, DF_HEADS, 2 * DF_HEAD_DIM)
    q = jnp.concatenate([q[:, :C], apply_axial_rope(q[:, C:], cos, sin)], axis=1)
    k = jnp.concatenate([k[:, :C], apply_axial_rope(k[:, C:], cos, sin)], axis=1)
    lf = lambdas.astype(jnp.float32)
    lam = jnp.exp(jnp.sum(lf[0] * lf[1])) - jnp.exp(jnp.sum(lf[2] * lf[3])) + lambda_init
    scale = DF_HEAD_DIM ** -0.5

    def head_out(o):
        o = rmsnorm(o, subln_g, eps=DF_SUBLN_EPS) * (1.0 - lambda_init)
        return o.reshape(o.shape[0], o.shape[1], DF_HEADS * 2 * DF_HEAD_DIM) @ w_o

    ol = sweep_query_blocks(lambda qb: diff_attend(qb, k, v, lam, scale), q[:, C:])
    yl = head_out(ol)
    if not need_ctx:
        return None, yl
    oc = diff_attend(q[:, :C], k[:, :C], v[:, :C], lam, scale)
    return head_out(oc), yl


def sq_relu_mlp(x, w1, w2):
    return jnp.square(jax.nn.relu(x @ w1)) @ w2


def setup_inputs(seed: int = 0) -> dict:
    key = jax.random.key(seed)
    keys = jax.random.split(key, 40)
    cnt = [0]

    def nk():
        k = keys[cnt[0]]
        cnt[0] += 1
        return k

    def normal(shape):
        return jax.random.normal(nk(), shape, jnp.float32)

    def dense(shape, fan_in, gain=1.0):
        return normal(shape) * (gain * fan_in ** -0.5)

    def gains(shape):
        return 1.0 + 0.05 * normal(shape)

    def small(shape, s=0.01):
        return s * normal(shape)

    D = D_MODEL
    return {
        'x': normal((BATCH, SEQ, D)),
        'c': normal((BATCH, D)),
        'ctx': normal((BATCH, CTX_LEN, D)),
        'c_ctx': normal((D,)),
        'ada_w': dense((DEPTH, D, 6 * D), D, 0.5),
        'ada_b': small((DEPTH, 6 * D)),
        'norm_g': gains((DEPTH, 2, D)),
        'mlp_w1': dense((DEPTH, D, D_FF), D),
        'mlp_w2': dense((DEPTH, D_FF, D), D_FF),
        'final_norm_g': gains((D,)),
        'mla_w_dq': dense((N_LAYERS_A, D, MLA_Q_RANK), D),
        'mla_q_norm_g': gains((N_LAYERS_A, MLA_Q_RANK)),
        'mla_w_uq': dense((N_LAYERS_A, MLA_Q_RANK, MLA_HEADS * (MLA_NOPE + MLA_ROPE)), MLA_Q_RANK),
        'mla_w_dkv': dense((N_LAYERS_A, D, MLA_KV_RANK + MLA_ROPE), D),
        'mla_kv_norm_g': gains((N_LAYERS_A, MLA_KV_RANK)),
        'mla_w_ukv': dense((N_LAYERS_A, MLA_KV_RANK, MLA_HEADS * (MLA_NOPE + MLA_V)), MLA_KV_RANK),
        'mla_w_o': dense((N_LAYERS_A, MLA_HEADS * MLA_V, D), MLA_HEADS * MLA_V),
        'hy_w_in': dense((N_LAYERS_B, D, 3 * D), D),
        'hy_b_in': small((N_LAYERS_B, 3 * D)),
        'hy_conv_w': dense((N_LAYERS_B, HY_SHORT, 3 * D), HY_SHORT),
        'hy_conv_b': small((N_LAYERS_B, 3 * D)),
        'hy_filt_w1': dense((N_LAYERS_B, HY_EMB_DIM, HY_FILT_ORDER), HY_EMB_DIM),
        'hy_filt_b1': small((N_LAYERS_B, HY_FILT_ORDER), 0.1),
        'hy_filt_w2': dense((N_LAYERS_B, HY_FILT_ORDER, HY_FILT_ORDER), HY_FILT_ORDER),
        'hy_filt_b2': small((N_LAYERS_B, HY_FILT_ORDER), 0.1),
        'hy_filt_w3': dense((N_LAYERS_B, HY_FILT_ORDER, HY_FILT_ORDER), HY_FILT_ORDER),
        'hy_filt_b3': small((N_LAYERS_B, HY_FILT_ORDER), 0.1),
        'hy_filt_freq': gains((N_LAYERS_B, 3, HY_FILT_ORDER)),
        'hy_filt_wout': dense((N_LAYERS_B, HY_FILT_ORDER, 2 * D), HY_FILT_ORDER, 0.1),
        'hy_filt_bias': small((N_LAYERS_B, D), 0.5),
        'hy_w_out': dense((N_LAYERS_B, D, D), D),
        'hy_b_out': small((N_LAYERS_B, D)),
        'df_w_qkv': dense((N_LAYERS_C, D, 3 * D), D),
        'df_lambda': small((N_LAYERS_C, 4, DF_HEAD_DIM), 0.1),
        'df_subln_g': gains((N_LAYERS_C, 2 * DF_HEAD_DIM)),
        'df_w_o': dense((N_LAYERS_C, D, D), D),
    }


def reference(x, c, ctx, c_ctx, ada_w, ada_b, norm_g, mlp_w1, mlp_w2, final_norm_g,
              mla_w_dq, mla_q_norm_g, mla_w_uq, mla_w_dkv, mla_kv_norm_g, mla_w_ukv, mla_w_o,
              hy_w_in, hy_b_in, hy_conv_w, hy_conv_b, hy_filt_w1, hy_filt_b1, hy_filt_w2, hy_filt_b2,
              hy_filt_w3, hy_filt_b3, hy_filt_freq, hy_filt_wout, hy_filt_bias, hy_w_out, hy_b_out,
              df_w_qkv, df_lambda, df_subln_g, df_w_o):
    L = x.shape[1]
    cos_a, sin_a = axial_rope_tables(L, MLA_ROPE)
    cos_d, sin_d = axial_rope_tables(L, DF_HEAD_DIM)
    s_lat = jax.nn.silu(c)
    s_ctx = jax.nn.silu(c_ctx)
    hl, hc = x, ctx
    for i in range(DEPTH):
        need_ctx = i < DEPTH - 1
        sh1, sc1, g1, sh2, sc2, g2 = [m[:, None, :] for m in jnp.split(s_lat @ ada_w[i] + ada_b[i], 6, axis=-1)]
        csh1, csc1, cg1, csh2, csc2, cg2 = jnp.split(s_ctx @ ada_w[i] + ada_b[i], 6, axis=-1)
        ul = modulate(rmsnorm(hl, norm_g[i, 0]), sh1, sc1)
        uc = modulate(rmsnorm(hc, norm_g[i, 0]), csh1, csc1)
        kind, j = i % N_MIXERS, i // N_MIXERS
        if kind == 0:
            yc, yl = mla_mixer(uc, ul, mla_w_dq[j], mla_q_norm_g[j], mla_w_uq[j], mla_w_dkv[j],
                               mla_kv_norm_g[j], mla_w_ukv[j], mla_w_o[j], cos_a, sin_a, need_ctx)
        elif kind == 1:
            yc, yl = hyena_mixer(uc, ul, hy_w_in[j], hy_b_in[j], hy_conv_w[j], hy_conv_b[j],
                                 hy_filt_w1[j], hy_filt_b1[j], hy_filt_w2[j], hy_filt_b2[j],
                                 hy_filt_w3[j], hy_filt_b3[j], hy_filt_freq[j], hy_filt_wout[j],
                                 hy_filt_bias[j], hy_w_out[j], hy_b_out[j], need_ctx)
        else:
            lambda_init = 0.8 - 0.6 * math.exp(-0.3 * i)
            yc, yl = diff_mixer(uc, ul, df_w_qkv[j], df_lambda[j], df_subln_g[j], df_w_o[j],
                                lambda_init, cos_d, sin_d, need_ctx)
        hl = hl + g1 * yl
        hl = hl + g2 * sq_relu_mlp(modulate(rmsnorm(hl, norm_g[i, 1]), sh2, sc2), mlp_w1[i], mlp_w2[i])
        if need_ctx:
            hc = hc + cg1 * yc
            hc = hc + cg2 * sq_relu_mlp(modulate(rmsnorm(hc, norm_g[i, 1]), csh2, csc2), mlp_w1[i], mlp_w2[i])
    return rmsnorm(hl, final_norm_g)
```

```python
import functools
import math

import jax
import jax.numpy as jnp
from jax import lax
from jax.experimental import pallas as pl
from jax.experimental.pallas import tpu as pltpu

F32 = jnp.float32
BF16 = jnp.bfloat16
HIGHEST = lax.Precision.HIGHEST

GRID_W = 64
ROPE_THETA = 10000.0
NORM_EPS = 1e-6
N_MIXERS = 3

MLA_NOPE = 128
MLA_ROPE = 64
MLA_V = 128
MLA_KV_RANK = 512
MLA_HEAD_PAD = 256

HY_SHORT = 3
HY_EMB_DIM = 33
HY_EMB_PAD = 64
HY_TARGET = 1e-2
HY_FAST_PCT = 0.3
HY_SLOW_PCT = 1.5

DF_HEAD_DIM = 128
DF_SUBLN_EPS = 1e-5

LANES = 128
MXU_COLS = 256
VMEM_LIMIT_MB = 56


def _params(semantics, vmem_mb=VMEM_LIMIT_MB):
    return pltpu.CompilerParams(dimension_semantics=semantics, vmem_limit_bytes=vmem_mb << 20)


class Rows:
    def __init__(self, B, C, L, bm, lat_only=False):
        assert L % bm == 0 and (B * C) % bm == 0
        self.bm = bm
        self.tiles_per_batch = L // bm
        self.n_lat = B * L // bm
        self.n = self.n_lat + (0 if lat_only else B * C // bm)

    def mod_row(self, i):
        return jnp.where(i < self.n_lat, 1 + i // self.tiles_per_batch, 0)

    def pos_block(self, i):
        return jnp.where(i < self.n_lat, i % self.tiles_per_batch, self.tiles_per_batch)


def _mod_spec(layer, chunk, rows, bn, col_of):
    return pl.BlockSpec((None, None, None, 1, bn),
                        lambda *g: (layer, chunk, rows.mod_row(g[0]), 0, col_of(*g)))


def fused_matmul(a, w, extras, outs, epilogue, *, rows, bn, bk=None, name):
    K = a.shape[1]
    N = w.shape[1]
    bm = rows.bm
    bk = bk or K
    nk = K // bk
    assert K % bk == 0 and N % bn == 0
    in_specs = [pl.BlockSpec((bm, bk), lambda i, j, k: (i, k)),
                pl.BlockSpec((bk, bn), lambda i, j, k: (k, j))]
    in_specs += [pl.BlockSpec(bs, im) for _, bs, im in extras]
    n_ex = len(extras)
    n_out = len(outs)

    def kern(*refs):
        a_ref, w_ref = refs[:2]
        ex = refs[2:2 + n_ex]
        out = refs[2 + n_ex:2 + n_ex + n_out]
        part = jnp.dot(a_ref[...], w_ref[...], preferred_element_type=F32)
        if nk == 1:
            epilogue(part, ex, out)
        else:
            acc_ref = refs[-1]
            k = pl.program_id(2)

            @pl.when(k == 0)
            def _():
                acc_ref[...] = part

            @pl.when(k > 0)
            def _():
                acc_ref[...] += part

            @pl.when(k == nk - 1)
            def _():
                epilogue(acc_ref[...], ex, out)

    res = pl.pallas_call(
        kern,
        grid=(rows.n, N // bn, nk),
        in_specs=in_specs,
        out_specs=[pl.BlockSpec(bs, im) for _, _, bs, im in outs],
        out_shape=[jax.ShapeDtypeStruct(s, d) for s, d, _, _ in outs],
        scratch_shapes=[pltpu.VMEM((bm, bn), F32)] if nk > 1 else [],
        compiler_params=_params(("parallel", "arbitrary", "arbitrary")),
        name=name,
    )(a, w, *[e[0] for e in extras])
    return res


def _rot_half(x, quarter):
    n = x.shape[-1]
    lane = lax.broadcasted_iota(jnp.int32, x.shape, x.ndim - 1)
    first = (lane % (2 * quarter)) < quarter
    return jnp.where(first, -pltpu.roll(x, n - quarter, x.ndim - 1), pltpu.roll(x, quarter, x.ndim - 1))


def _rms(x, g, eps):
    return x * lax.rsqrt(jnp.mean(x * x, axis=-1, keepdims=True) + eps) * g


def gated_matmul(a, w, resid, mod, layer, chunk, rows, *, name, bn=1024, bk=None, bias=None):
    T = rows.n * rows.bm
    N = w.shape[1]
    extras = [(resid, (rows.bm, bn), lambda i, j, k: (i, j)),
              (mod,) + _spec_parts(_mod_spec(layer, chunk, rows, bn, lambda i, j, k: j))]
    if bias is not None:
        extras.append((bias.reshape(1, N), (1, bn), lambda i, j, k: (0, j)))

    def epi(acc, ex, out):
        y = acc if bias is None else acc + ex[2][...]
        out[0][...] = ex[0][...] + ex[1][...] * y

    return fused_matmul(a, w, extras, [((T, N), F32, (rows.bm, bn), lambda i, j, k: (i, j))], epi,
                        rows=rows, bn=bn, bk=bk, name=name)[0]


def _spec_parts(spec):
    return spec.block_shape, spec.index_map


def adaln_table(cvec, ada_w, ada_b):
    depth, D, _ = ada_w.shape
    R = cvec.shape[0]

    def kern(c_ref, w_ref, b_ref, o_ref):
        c = c_ref[...]
        s = c / (1.0 + jnp.exp(-c))
        o_ref[...] = jnp.dot(s, w_ref[...], precision=HIGHEST, preferred_element_type=F32) + b_ref[...]

    out = pl.pallas_call(
        kern,
        grid=(depth, 6),
        in_specs=[pl.BlockSpec((R, D), lambda l, j: (0, 0)),
                  pl.BlockSpec((None, D, D), lambda l, j: (l, 0, j)),
                  pl.BlockSpec((None, 1, D), lambda l, j: (l, 0, j))],
        out_specs=pl.BlockSpec((None, None, R, D), lambda l, j: (l, j, 0, 0)),
        out_shape=jax.ShapeDtypeStruct((depth, 6, R, D), F32),
        compiler_params=_params(("parallel", "arbitrary")),
        name="adaln_table",
    )(cvec, ada_w, ada_b.reshape(depth, 1, 6 * D))
    return out.reshape(depth, 6, R, 1, D)


def norm_modulate(h, norm_g, mod, layer, which, rows):
    D = h.shape[1]
    bm = rows.bm
    g4 = norm_g.reshape(norm_g.shape[0], 2, 1, D)

    def kern(x_ref, g_ref, sh_ref, sc_ref, o_ref):
        y = _rms(x_ref[...], g_ref[...], NORM_EPS)
        o_ref[...] = (y * (1.0 + sc_ref[...]) + sh_ref[...]).astype(BF16)

    return pl.pallas_call(
        kern,
        grid=(rows.n,),
        in_specs=[pl.BlockSpec((bm, D), lambda i: (i, 0)),
                  pl.BlockSpec((None, None, 1, D), lambda i: (layer, which, 0, 0)),
                  _mod_spec(layer, 3 * which, rows, D, lambda i: 0),
                  _mod_spec(layer, 3 * which + 1, rows, D, lambda i: 0)],
        out_specs=pl.BlockSpec((bm, D), lambda i: (i, 0)),
        out_shape=jax.ShapeDtypeStruct((rows.n * bm, D), BF16),
        compiler_params=_params(("parallel",)),
        name="norm_modulate",
    )(h, g4, mod, mod)


def final_norm(h, g, rows):
    D = h.shape[1]
    bm = rows.bm

    def kern(x_ref, g_ref, o_ref):
        o_ref[...] = _rms(x_ref[...], g_ref[...], NORM_EPS)

    return pl.pallas_call(
        kern,
        grid=(rows.n,),
        in_specs=[pl.BlockSpec((bm, D), lambda i: (i, 0)),
                  pl.BlockSpec((1, D), lambda i: (0, 0))],
        out_specs=pl.BlockSpec((bm, D), lambda i: (i, 0)),
        out_shape=jax.ShapeDtypeStruct((rows.n * bm, D), F32),
        compiler_params=_params(("parallel",)),
        name="final_norm",
    )(h, g.reshape(1, D))


def mlp(h, norm_g, mod, layer, w1, w2, rows):
    u = norm_modulate(h, norm_g, mod, layer, 1, rows)
    T = rows.n * rows.bm
    F = w1.shape[1]

    def relu2(acc, ex, out):
        r = jnp.maximum(acc, 0.0)
        out[0][...] = (r * r).astype(BF16)

    hid = fused_matmul(u, w1, [], [((T, F), BF16, (rows.bm, 1024), lambda i, j, k: (i, j))], relu2,
                       rows=rows, bn=1024, name="mlp_up")[0]
    return gated_matmul(hid, w2, h, mod, layer, 5, rows, bn=1024, bk=2048, name="mlp_down")


def axial_rope_tables(L, rot_dim):
    rows = L // GRID_W
    row = jnp.repeat(jnp.arange(rows, dtype=F32), GRID_W)
    col = jnp.tile(jnp.arange(GRID_W, dtype=F32), rows)
    pos = jnp.stack([row, col], axis=-1)
    n_freq = rot_dim // 4
    inv_freq = ROPE_THETA ** (-jnp.arange(n_freq, dtype=F32) / n_freq)
    ang = pos[:, :, None, None] * inv_freq
    ang = jnp.broadcast_to(ang, (L, 2, 2, n_freq)).reshape(L, rot_dim)
    return jnp.cos(ang), jnp.sin(ang)


def rope_tables_padded(L, rot_dim, bm):
    cos, sin = axial_rope_tables(L, rot_dim)
    cos = jnp.pad(cos, ((0, bm), (0, LANES - rot_dim)), constant_values=1.0)
    sin = jnp.pad(sin, ((0, bm), (0, LANES - rot_dim)))
    return cos, sin


def mla_down(u, w_dq, q_g, w_dkv, kv_g, cos, sin, rows):
    T = rows.n * rows.bm
    bm = rows.bm
    qr = w_dq.shape[1]
    wd = jnp.concatenate([w_dq, w_dkv, jnp.zeros((w_dq.shape[0], LANES - MLA_ROPE), w_dq.dtype)],
                         axis=1).astype(BF16)
    n_all = wd.shape[1]
    c0 = qr + MLA_KV_RANK

    def epi(acc, ex, out):
        qg_ref, kvg_ref, cos_ref, sin_ref = ex
        out[0][...] = _rms(acc[:, :qr], qg_ref[...], NORM_EPS).astype(BF16)
        out[1][...] = _rms(acc[:, qr:c0], kvg_ref[...], NORM_EPS).astype(BF16)
        kr = acc[:, c0:]
        out[2][...] = (kr * cos_ref[...] + _rot_half(kr, MLA_ROPE // 4) * sin_ref[...]).astype(BF16)

    extras = [(q_g.reshape(1, qr), (1, qr), lambda i, j, k: (0, 0)),
              (kv_g.reshape(1, MLA_KV_RANK), (1, MLA_KV_RANK), lambda i, j, k: (0, 0)),
              (cos, (bm, LANES), lambda i, j, k: (rows.pos_block(i), 0)),
              (sin, (bm, LANES), lambda i, j, k: (rows.pos_block(i), 0))]
    outs = [((T, qr), BF16, (bm, qr), lambda i, j, k: (i, 0)),
            ((T, MLA_KV_RANK), BF16, (bm, MLA_KV_RANK), lambda i, j, k: (i, 0)),
            ((T, LANES), BF16, (bm, LANES), lambda i, j, k: (i, 0))]
    return fused_matmul(u, wd, extras, outs, epi, rows=rows, bn=n_all, name="mla_down")


def mla_queries(cq, w_uq, cos, sin, rows, heads):
    T = rows.n * rows.bm
    bm = rows.bm
    qr = w_uq.shape[0]
    hd = MLA_NOPE + MLA_ROPE
    w = w_uq.reshape(qr, heads, hd)
    w = jnp.pad(w, ((0, 0), (0, 0), (0, MLA_HEAD_PAD - hd))).reshape(qr, heads * MLA_HEAD_PAD).astype(BF16)
    scale = hd ** -0.5
    bn = 1024

    def epi(acc, ex, out):
        cos_ref, sin_ref = ex
        for hh in range(bn // MLA_HEAD_PAD):
            c = hh * MLA_HEAD_PAD
            out[0][:, c:c + LANES] = (acc[:, c:c + LANES] * scale).astype(BF16)
            r = acc[:, c + LANES:c + 2 * LANES]
            r = r * cos_ref[...] + _rot_half(r, MLA_ROPE // 4) * sin_ref[...]
            out[0][:, c + LANES:c + 2 * LANES] = (r * scale).astype(BF16)

    extras = [(cos, (bm, LANES), lambda i, j, k: (rows.pos_block(i), 0)),
              (sin, (bm, LANES), lambda i, j, k: (rows.pos_block(i), 0))]
    outs = [((T, heads * MLA_HEAD_PAD), BF16, (bm, bn), lambda i, j, k: (i, j))]
    return fused_matmul(cq, w, extras, outs, epi, rows=rows, bn=bn, name="mla_queries")[0]


def plain_matmul(a, w, rows, bn, *, name, out_dtype=BF16, bias=None):
    T = rows.n * rows.bm
    N = w.shape[1]
    extras = [] if bias is None else [(bias.reshape(1, N), (1, bn), lambda i, j, k: (0, j))]

    def epi(acc, ex, out):
        y = acc if bias is None else acc + ex[0][...]
        out[0][...] = y.astype(out_dtype)

    return fused_matmul(a, w, extras, [((T, N), out_dtype, (rows.bm, bn), lambda i, j, k: (i, j))], epi,
                        rows=rows, bn=bn, name=name)[0]


def mla_attention(q, kv, kr, B, C, L, heads, with_ctx):
    tq = C
    n_lat_t = L // tq
    n_t = n_lat_t + (1 if with_ctx else 0)
    ctx_blk = B * L // C
    S = C + L
    Tq = B * L + (B * C if with_ctx else 0)

    def q_row(b, h, t):
        lat = b * n_lat_t + t
        return jnp.where(t < n_lat_t, lat, ctx_blk + b) if with_ctx else lat

    def kern(q_ref, knl_ref, knc_ref, vl_ref, vc_ref, krl_ref, krc_ref, o_ref, k_s, v_s):
        t = pl.program_id(2)

        @pl.when(t == 0)
        def _():
            k_s[:C, :LANES] = knc_ref[...]
            k_s[:C, LANES:] = krc_ref[...]
            k_s[C:, :LANES] = knl_ref[...]
            k_s[C:, LANES:] = krl_ref[...]
            v_s[:C, :] = vc_ref[...]
            v_s[C:, :] = vl_ref[...]

        def attend(n_keys):
            s = lax.dot_general(q_ref[...], k_s[:n_keys, :], (((1,), (1,)), ((), ())),
                                preferred_element_type=F32)
            p = jnp.exp(s - jnp.max(s, axis=-1, keepdims=True))
            l = jnp.sum(p, axis=-1, keepdims=True)
            o = jnp.dot(p.astype(BF16), v_s[:n_keys, :], preferred_element_type=F32)
            o_ref[...] = (o / l).astype(BF16)

        if with_ctx:
            @pl.when(t < n_lat_t)
            def _():
                attend(S)

            @pl.when(t == n_lat_t)
            def _():
                attend(C)
        else:
            attend(S)

    return pl.pallas_call(
        kern,
        grid=(B, heads, n_t),
        in_specs=[pl.BlockSpec((tq, MLA_HEAD_PAD), lambda b, h, t: (q_row(b, h, t), h)),
                  pl.BlockSpec((L, LANES), lambda b, h, t: (b, 2 * h)),
                  pl.BlockSpec((C, LANES), lambda b, h, t: (ctx_blk + b, 2 * h)),
                  pl.BlockSpec((L, LANES), lambda b, h, t: (b, 2 * h + 1)),
                  pl.BlockSpec((C, LANES), lambda b, h, t: (ctx_blk + b, 2 * h + 1)),
                  pl.BlockSpec((L, LANES), lambda b, h, t: (b, 0)),
                  pl.BlockSpec((C, LANES), lambda b, h, t: (ctx_blk + b, 0))],
        out_specs=pl.BlockSpec((tq, MLA_V), lambda b, h, t: (q_row(b, h, t), h)),
        out_shape=jax.ShapeDtypeStruct((Tq, heads * MLA_V), BF16),
        scratch_shapes=[pltpu.VMEM((S, MLA_HEAD_PAD), BF16), pltpu.VMEM((S, MLA_V), BF16)],
        compiler_params=_params(("parallel", "parallel", "arbitrary")),
        name="mla_attention",
    )(q, kv, kv, kv, kv, kr, kr)


def mla_mixer(h, u, mod, layer, p, geo, need_ctx):
    B, C, L, rows_all, rows_out = geo
    heads = p["w_uq"].shape[1] // (MLA_NOPE + MLA_ROPE)
    cos, sin = rope_tables_padded(L, MLA_ROPE, rows_all.bm)
    cq, ckv, kr = mla_down(u, p["w_dq"], p["q_g"], p["w_dkv"], p["kv_g"], cos, sin, rows_all)
    kv = plain_matmul(ckv, p["w_ukv"].astype(BF16), rows_all, 1024, name="mla_kv")
    q = mla_queries(cq, p["w_uq"], cos, sin, rows_out, heads)
    o = mla_attention(q, kv, kr, B, C, L, heads, need_ctx)
    return gated_matmul(o, p["w_o"].astype(BF16), h, mod, layer, 2, rows_out, name="mla_out")


def diff_qkv(u, w_qkv, cos, sin, rows):
    T = rows.n * rows.bm
    bm = rows.bm
    D = w_qkv.shape[0]
    bn = 1024
    scale = DF_HEAD_DIM ** -0.5
    n_q = D // bn

    def epi(acc, ex, out):
        cos_ref, sin_ref = ex
        j = pl.program_id(1)

        @pl.when(j < 2 * n_q)
        def _():
            mul = jnp.where(j < n_q, scale, 1.0).astype(F32)
            for s in range(bn // LANES):
                x = acc[:, s * LANES:(s + 1) * LANES]
                x = x * cos_ref[...] + _rot_half(x, DF_HEAD_DIM // 4) * sin_ref[...]
                out[0][:, s * LANES:(s + 1) * LANES] = (x * mul).astype(BF16)

        @pl.when(j >= 2 * n_q)
        def _():
            out[0][...] = acc.astype(BF16)

    extras = [(cos, (bm, LANES), lambda i, j, k: (rows.pos_block(i), 0)),
              (sin, (bm, LANES), lambda i, j, k: (rows.pos_block(i), 0))]
    outs = [((T, 3 * D), BF16, (bm, bn), lambda i, j, k: (i, j))]
    return fused_matmul(u, w_qkv.astype(BF16), extras, outs, epi, rows=rows, bn=bn, name="diff_qkv")[0]


def diff_attention(qkv, lambdas, subln_g, lambda_init, B, C, L, D, with_ctx):
    hd = DF_HEAD_DIM
    heads = D // (2 * hd)
    tq = C
    n_lat_t = L // tq
    n_t = n_lat_t + (1 if with_ctx else 0)
    ctx_blk = B * L // C
    S = C + L
    Tq = B * L + (B * C if with_ctx else 0)
    kcol = D // hd
    vcol = 2 * D // (2 * hd)

    def q_row(b, h, t):
        lat = b * n_lat_t + t
        return jnp.where(t < n_lat_t, lat, ctx_blk + b) if with_ctx else lat

    def kern(q1_ref, q2_ref, k1l_ref, k1c_ref, k2l_ref, k2c_ref, vl_ref, vc_ref, lam_ref, g_ref,
             o_ref, k1_s, k2_s, v_s):
        t = pl.program_id(2)

        @pl.when(t == 0)
        def _():
            k1_s[:C, :] = k1c_ref[...]
            k1_s[C:, :] = k1l_ref[...]
            k2_s[:C, :] = k2c_ref[...]
            k2_s[C:, :] = k2l_ref[...]
            v_s[:C, :] = vc_ref[...]
            v_s[C:, :] = vl_ref[...]

        lf = lam_ref[...]
        lam = (jnp.exp(jnp.sum(lf[0:1] * lf[1:2], axis=-1, keepdims=True))
               - jnp.exp(jnp.sum(lf[2:3] * lf[3:4], axis=-1, keepdims=True)) + lambda_init)

        def softmax_av(q_ref, k_s, n_keys):
            s = lax.dot_general(q_ref[...], k_s[:n_keys, :], (((1,), (1,)), ((), ())),
                                preferred_element_type=F32)
            p = jnp.exp(s - jnp.max(s, axis=-1, keepdims=True))
            l = jnp.sum(p, axis=-1, keepdims=True)
            return jnp.dot(p.astype(BF16), v_s[:n_keys, :], preferred_element_type=F32) / l

        def attend(n_keys):
            o = softmax_av(q1_ref, k1_s, n_keys) - lam * softmax_av(q2_ref, k2_s, n_keys)
            o_ref[...] = (_rms(o, g_ref[...], DF_SUBLN_EPS) * (1.0 - lambda_init)).astype(BF16)

        if with_ctx:
            @pl.when(t < n_lat_t)
            def _():
                attend(S)

            @pl.when(t == n_lat_t)
            def _():
                attend(C)
        else:
            attend(S)

    return pl.pallas_call(
        kern,
        grid=(B, heads, n_t),
        in_specs=[pl.BlockSpec((tq, hd), lambda b, h, t: (q_row(b, h, t), 2 * h)),
                  pl.BlockSpec((tq, hd), lambda b, h, t: (q_row(b, h, t), 2 * h + 1)),
                  pl.BlockSpec((L, hd), lambda b, h, t: (b, kcol + 2 * h)),
                  pl.BlockSpec((C, hd), lambda b, h, t: (ctx_blk + b, kcol + 2 * h)),
                  pl.BlockSpec((L, hd), lambda b, h, t: (b, kcol + 2 * h + 1)),
                  pl.BlockSpec((C, hd), lambda b, h, t: (ctx_blk + b, kcol + 2 * h + 1)),
                  pl.BlockSpec((L, 2 * hd), lambda b, h, t: (b, vcol + h)),
                  pl.BlockSpec((C, 2 * hd), lambda b, h, t: (ctx_blk + b, vcol + h)),
                  pl.BlockSpec((4, hd), lambda b, h, t: (0, 0)),
                  pl.BlockSpec((1, 2 * hd), lambda b, h, t: (0, 0))],
        out_specs=pl.BlockSpec((tq, 2 * hd), lambda b, h, t: (q_row(b, h, t), h)),
        out_shape=jax.ShapeDtypeStruct((Tq, D), BF16),
        scratch_shapes=[pltpu.VMEM((S, hd), BF16), pltpu.VMEM((S, hd), BF16), pltpu.VMEM((S, 2 * hd), BF16)],
        compiler_params=_params(("parallel", "parallel", "arbitrary")),
        name="diff_attention",
    )(qkv, qkv, qkv, qkv, qkv, qkv, qkv, qkv, lambdas, subln_g.reshape(1, 2 * hd))


def diff_mixer(h, u, mod, layer, p, geo, need_ctx, lambda_init):
    B, C, L, rows_all, rows_out = geo
    D = h.shape[1]
    cos, sin = rope_tables_padded(L, DF_HEAD_DIM, rows_all.bm)
    qkv = diff_qkv(u, p["w_qkv"], cos, sin, rows_all)
    o = diff_attention(qkv, p["lambdas"], p["subln_g"], lambda_init, B, C, L, D, need_ctx)
    return gated_matmul(o, p["w_o"].astype(BF16), h, mod, layer, 2, rows_out, name="diff_out")


def dft_matrices(Ls):
    n = 2 * Ls
    t0n = min(64, Ls)
    t1n = Ls // t0n
    f = jnp.arange(Ls, dtype=jnp.int32)[:, None]
    a1 = ((f * (jnp.arange(t1n, dtype=jnp.int32) * t0n)[None, :]) % n).astype(F32) * (2.0 * math.pi / n)
    a0 = ((f * jnp.arange(t0n, dtype=jnp.int32)[None, :]) % n).astype(F32) * (2.0 * math.pi / n)
    c1, s1 = jnp.cos(a1)[:, :, None], jnp.sin(a1)[:, :, None]
    c0, s0 = jnp.cos(a0)[:, None, :], jnp.sin(a0)[:, None, :]
    cosm = (c1 * c0 - s1 * s0).reshape(Ls, Ls)
    sinm = (s1 * c0 + c1 * s0).reshape(Ls, Ls)
    nyq = jnp.where(jnp.arange(Ls) % 2 == 0, 1.0, -1.0).astype(F32)[None, :]
    imag = jnp.where(f == 0, nyq, -sinm)
    fwd = jnp.concatenate([cosm, imag], axis=0)
    col = jnp.arange(n)
    cscale = jnp.where((col == 0) | (col == Ls), 1.0 / n, 2.0 / n).astype(F32)
    inv = fwd.T * cscale[None, :]
    return fwd.astype(BF16), inv.astype(BF16)


def hyena_filter_time(Ls, p):
    D = p["f_bias"].shape[0]
    order = p["f_w2"].shape[0]
    bands = (HY_EMB_DIM - 1) // 2
    t = jnp.linspace(0.0, 1.0, Ls, dtype=F32)[:, None]
    w = 2.0 * math.pi * jnp.arange(Ls, dtype=F32)[:, None] / Ls
    f = jnp.linspace(1e-4, bands - 1, bands, dtype=F32)
    feats = jnp.concatenate([t, jnp.cos(f * w), -jnp.sin(f * w)], axis=-1)
    feats = jnp.pad(feats, ((0, 0), (0, HY_EMB_PAD - HY_EMB_DIM)))
    w1 = jnp.pad(p["f_w1"], ((0, HY_EMB_PAD - HY_EMB_DIM), (0, 0)))
    deltas = jnp.abs(jnp.linspace(math.log(HY_TARGET) / HY_SLOW_PCT, math.log(HY_TARGET) / HY_FAST_PCT,
                                  D, dtype=F32))
    deltas2 = jnp.concatenate([deltas, deltas]).reshape(1, 2 * D)

    def ffn_kern(x_ref, w1_ref, b1_ref, w2_ref, b2_ref, w3_ref, b3_ref, fr_ref, o_ref):
        dot = functools.partial(jnp.dot, precision=HIGHEST, preferred_element_type=F32)
        fr = fr_ref[...]
        hcur = jnp.sin(fr[0:1] * (dot(x_ref[...], w1_ref[...]) + b1_ref[...]))
        hcur = jnp.sin(fr[1:2] * (dot(hcur, w2_ref[...]) + b2_ref[...]))
        o_ref[...] = jnp.sin(fr[2:3] * (dot(hcur, w3_ref[...]) + b3_ref[...]))

    full = lambda a: pl.BlockSpec(a.shape, lambda: (0,) * a.ndim)
    ffn_in = [feats, w1, p["f_b1"].reshape(1, order), p["f_w2"], p["f_b2"].reshape(1, order),
              p["f_w3"], p["f_b3"].reshape(1, order), p["f_freq"]]
    hff = pl.pallas_call(
        ffn_kern,
        in_specs=[full(a) for a in ffn_in],
        out_specs=pl.BlockSpec((Ls, order), lambda: (0, 0)),
        out_shape=jax.ShapeDtypeStruct((Ls, order), F32),
        name="hyena_filter_ffn",
    )(*ffn_in)

    bn = 512

    def out_kern(h_ref, w_ref, d_ref, o_ref):
        tt = lax.broadcasted_iota(jnp.int32, (Ls, 1), 0).astype(F32) * (1.0 / (Ls - 1))
        hw = jnp.dot(h_ref[...], w_ref[...], precision=HIGHEST, preferred_element_type=F32)
        o_ref[...] = hw * jnp.exp(-tt * d_ref[...])

    return pl.pallas_call(
        out_kern,
        grid=(2 * D // bn,),
        in_specs=[pl.BlockSpec((Ls, order), lambda j: (0, 0)),
                  pl.BlockSpec((order, bn), lambda j: (0, j)),
                  pl.BlockSpec((1, bn), lambda j: (0, j))],
        out_specs=pl.BlockSpec((Ls, bn), lambda j: (0, j)),
        out_shape=jax.ShapeDtypeStruct((Ls, 2 * D), F32),
        compiler_params=_params(("parallel",)),
        name="hyena_filter_out",
    )(hff, p["f_wout"], deltas2)


def hyena_filter_spectrum(hfb, fwd, Ls, D):
    n = 2 * Ls
    cb = LANES
    ncb = D // cb

    def kern(w_ref, hf_ref, hb_ref, o_ref):
        row = lax.broadcasted_iota(jnp.int32, (Ls, 1), 0)
        hb = jnp.where(row == 0, 0.0, hb_ref[...])
        a = jnp.dot(w_ref[...], hf_ref[...].astype(BF16), preferred_element_type=F32)
        b = jnp.dot(w_ref[...], hb.astype(BF16), preferred_element_type=F32)
        frow = lax.broadcasted_iota(jnp.int32, (n, 1), 0)
        o_ref[...] = a + jnp.where(frow > Ls, -b, b)

    return pl.pallas_call(
        kern,
        grid=(ncb,),
        in_specs=[pl.BlockSpec((n, Ls), lambda j: (0, 0), pipeline_mode=pl.Buffered(1)),
                  pl.BlockSpec((Ls, cb), lambda j: (0, j)),
                  pl.BlockSpec((Ls, cb), lambda j: (0, ncb + j))],
        out_specs=pl.BlockSpec((n, cb), lambda j: (0, j)),
        out_shape=jax.ShapeDtypeStruct((n, D), F32),
        compiler_params=_params(("parallel",)),
        name="hyena_filter_spectrum",
    )(fwd, hfb, hfb)


def _short_conv(x, w, b):
    n = x.shape[0]
    row = lax.broadcasted_iota(jnp.int32, (n, 1), 0)
    prev = jnp.where(row == 0, 0.0, pltpu.roll(x, 1, 0))
    nxt = jnp.where(row == n - 1, 0.0, pltpu.roll(x, n - 1, 0))
    return prev * w[0:1] + x * w[1:2] + nxt * w[2:3] + b


def hyena_segment(z, p, Ls, row_blk0, B, D):
    n = 2 * Ls
    cb = MXU_COLS
    ncb = D // cb
    fwd, inv = dft_matrices(Ls)
    hfb = hyena_filter_time(Ls, p)
    kf = hyena_filter_spectrum(hfb, fwd, Ls, D)
    conv_w, conv_b = p["conv_w"], p["conv_b"].reshape(1, 3 * D)

    def gate_kern(x1_ref, v_ref, cw1_ref, cb1_ref, cwv_ref, cbv_ref, g_ref):
        g_ref[...] = (_short_conv(v_ref[...].astype(F32), cwv_ref[...], cbv_ref[...])
                      * _short_conv(x1_ref[...].astype(F32), cw1_ref[...], cb1_ref[...])).astype(BF16)

    g = pl.pallas_call(
        gate_kern,
        grid=(ncb, B),
        in_specs=[pl.BlockSpec((Ls, cb), lambda j, b: (row_blk0 + b, ncb + j)),
                  pl.BlockSpec((Ls, cb), lambda j, b: (row_blk0 + b, 2 * ncb + j)),
                  pl.BlockSpec((HY_SHORT, cb), lambda j, b: (0, ncb + j)),
                  pl.BlockSpec((1, cb), lambda j, b: (0, ncb + j)),
                  pl.BlockSpec((HY_SHORT, cb), lambda j, b: (0, 2 * ncb + j)),
                  pl.BlockSpec((1, cb), lambda j, b: (0, 2 * ncb + j))],
        out_specs=pl.BlockSpec((Ls, cb), lambda j, b: (b, j)),
        out_shape=jax.ShapeDtypeStruct((B * Ls, D), BF16),
        compiler_params=_params(("parallel", "arbitrary")),
        name="hyena_gate",
    )(z, z, conv_w, conv_b, conv_w, conv_b)

    def fwd_kern(w_ref, g_ref, kf_ref, y_ref):
        u = jnp.dot(w_ref[...], g_ref[...], preferred_element_type=F32)
        ure, uim = u[:Ls], u[Ls:]
        kre, kim = kf_ref[:Ls, :], kf_ref[Ls:, :]
        first = lax.broadcasted_iota(jnp.int32, (Ls, 1), 0) == 0
        y_ref[:Ls, :] = (ure * kre - jnp.where(first, 0.0, uim * kim)).astype(BF16)
        y_ref[Ls:, :] = jnp.where(first, uim * kim, ure * kim + uim * kre).astype(BF16)

    y = pl.pallas_call(
        fwd_kern,
        grid=(ncb, B),
        in_specs=[pl.BlockSpec((n, Ls), lambda j, b: (0, 0), pipeline_mode=pl.Buffered(1)),
                  pl.BlockSpec((Ls, cb), lambda j, b: (b, j)),
                  pl.BlockSpec((n, cb), lambda j, b: (0, j))],
        out_specs=pl.BlockSpec((None, n, cb), lambda j, b: (b, 0, j)),
        out_shape=jax.ShapeDtypeStruct((B, n, D), BF16),
        compiler_params=_params(("parallel", "arbitrary")),
        name="hyena_dft",
    )(fwd, g, kf)

    def inv_kern(w_ref, y_ref, g_ref, x0_ref, cw0_ref, cb0_ref, fb_ref, o_ref):
        conv = jnp.dot(w_ref[...], y_ref[...], preferred_element_type=F32)
        x0 = _short_conv(x0_ref[...].astype(F32), cw0_ref[...], cb0_ref[...])
        o_ref[...] = ((conv + g_ref[...].astype(F32) * fb_ref[...]) * x0).astype(BF16)

    return pl.pallas_call(
        inv_kern,
        grid=(ncb, B),
        in_specs=[pl.BlockSpec((Ls, n), lambda j, b: (0, 0), pipeline_mode=pl.Buffered(1)),
                  pl.BlockSpec((None, n, cb), lambda j, b: (b, 0, j)),
                  pl.BlockSpec((Ls, cb), lambda j, b: (b, j)),
                  pl.BlockSpec((Ls, cb), lambda j, b: (row_blk0 + b, j)),
                  pl.BlockSpec((HY_SHORT, cb), lambda j, b: (0, j)),
                  pl.BlockSpec((1, cb), lambda j, b: (0, j)),
                  pl.BlockSpec((1, cb), lambda j, b: (0, j))],
        out_specs=pl.BlockSpec((Ls, cb), lambda j, b: (b, j)),
        out_shape=jax.ShapeDtypeStruct((B * Ls, D), BF16),
        compiler_params=_params(("parallel", "arbitrary")),
        name="hyena_idft",
    )(inv, y, g, z, conv_w, conv_b, p["f_bias"].reshape(1, D))


def hyena_mixer(h, u, mod, layer, p, geo, need_ctx):
    B, C, L, rows_all, rows_out = geo
    D = h.shape[1]
    z = plain_matmul(u, p["w_in"].astype(BF16), rows_out, 1024, bias=p["b_in"], name="hyena_in")
    y = hyena_segment(z, p, L, 0, B, D)
    if need_ctx:
        y = jnp.concatenate([y, hyena_segment(z, p, C, B * L // C, B, D)], axis=0)
    return gated_matmul(y, p["w_out"].astype(BF16), h, mod, layer, 2, rows_out, bias=p["b_out"], name="hyena_out")


def kernel(x, c, ctx, c_ctx, ada_w, ada_b, norm_g, mlp_w1, mlp_w2, final_norm_g, mla_w_dq, mla_q_norm_g, mla_w_uq, mla_w_dkv, mla_kv_norm_g, mla_w_ukv, mla_w_o, hy_w_in, hy_b_in, hy_conv_w, hy_conv_b, hy_filt_w1, hy_filt_b1, hy_filt_w2, hy_filt_b2, hy_filt_w3, hy_filt_b3, hy_filt_freq, hy_filt_wout, hy_filt_bias, hy_w_out, hy_b_out, df_w_qkv, df_lambda, df_subln_g, df_w_o):
    B, L, D = x.shape
    C = ctx.shape[1]
    depth = ada_w.shape[0]
    bm = min(1024, B * C)
    rows_all = Rows(B, C, L, bm)
    rows_lat = Rows(B, C, L, bm, lat_only=True)

    mod = adaln_table(jnp.concatenate([c_ctx[None, :], c], axis=0), ada_w, ada_b)
    h = jnp.concatenate([x.reshape(B * L, D), ctx.reshape(B * C, D)], axis=0)

    for i in range(depth):
        need_ctx = i < depth - 1
        rows_out = rows_all if need_ctx else rows_lat
        geo = (B, C, L, rows_all, rows_out)
        kind, j = i % N_MIXERS, i // N_MIXERS
        u = norm_modulate(h, norm_g, mod, i, 0, rows_all)
        if kind == 0:
            p = dict(w_dq=mla_w_dq[j], q_g=mla_q_norm_g[j], w_uq=mla_w_uq[j], w_dkv=mla_w_dkv[j],
                     kv_g=mla_kv_norm_g[j], w_ukv=mla_w_ukv[j], w_o=mla_w_o[j])
            h = mla_mixer(h, u, mod, i, p, geo, need_ctx)
        elif kind == 1:
            p = dict(w_in=hy_w_in[j], b_in=hy_b_in[j], conv_w=hy_conv_w[j], conv_b=hy_conv_b[j],
                     f_w1=hy_filt_w1[j], f_b1=hy_filt_b1[j], f_w2=hy_filt_w2[j], f_b2=hy_filt_b2[j],
                     f_w3=hy_filt_w3[j], f_b3=hy_filt_b3[j], f_freq=hy_filt_freq[j], f_wout=hy_filt_wout[j],
                     f_bias=hy_filt_bias[j], w_out=hy_w_out[j], b_out=hy_b_out[j])
            h = hyena_mixer(h, u, mod, i, p, geo, need_ctx)
        else:
            lambda_init = 0.8 - 0.6 * math.exp(-0.3 * i)
            p = dict(w_qkv=df_w_qkv[j], lambdas=df_lambda[j], subln_g=df_subln_g[j], w_o=df_w_o[j])
            h = diff_mixer(h, u, mod, i, p, geo, need_ctx, lambda_init)
        h = mlp(h, norm_g, mod, i, mlp_w1[i].astype(BF16), mlp_w2[i].astype(BF16), rows_out)

    return final_norm(h, final_norm_g, rows_lat).reshape(B, L, D)
```

```python
import functools
import math

import jax
import jax.numpy as jnp
import numpy as np
from jax import lax
from jax.experimental import pallas as pl
from jax.experimental.pallas import tpu as pltpu

F32 = jnp.float32
BF16 = jnp.bfloat16
HIGHEST = lax.Precision.HIGHEST
LOG2E = math.log2(math.e)

GRID_W = 64
ROPE_THETA = 10000.0
NORM_EPS = 1e-6
N_MIXERS = 3

MLA_NOPE = 128
MLA_ROPE = 64
MLA_V = 128
MLA_KV_RANK = 512
MLA_HEAD_PAD = 256

HY_SHORT = 3
HY_EMB_DIM = 33
HY_EMB_PAD = 64
HY_TARGET = 1e-2
HY_FAST_PCT = 0.3
HY_SLOW_PCT = 1.5

DF_HEAD_DIM = 128
DF_SUBLN_EPS = 1e-5

LANES = 128
MXU_COLS = 256
VMEM_LIMIT_MB = 56


def _params(semantics, vmem_mb=VMEM_LIMIT_MB):
    return pltpu.CompilerParams(dimension_semantics=semantics, vmem_limit_bytes=vmem_mb << 20)


class Rows:
    def __init__(self, B, C, L, bm, lat_only=False):
        assert L % bm == 0 and (B * C) % bm == 0
        self.geometry = (B, C, L, lat_only)
        self.bm = bm
        self.tiles_per_batch = L // bm
        self.n_lat = B * L // bm
        self.n = self.n_lat + (0 if lat_only else B * C // bm)

    def with_bm(self, bm):
        B, C, L, lat_only = self.geometry
        return Rows(B, C, L, bm, lat_only)

    def mod_row(self, i):
        return jnp.where(i < self.n_lat, 1 + i // self.tiles_per_batch, 0)

    def pos_block(self, i):
        return jnp.where(i < self.n_lat, i % self.tiles_per_batch, self.tiles_per_batch)


def _mod_spec(layer, chunk, rows, bn, col_of):
    return pl.BlockSpec((None, None, None, 1, bn),
                        lambda *g: (layer, chunk, rows.mod_row(g[0]), 0, col_of(*g)))


def fused_matmul(a, w, extras, outs, epilogue, *, rows, bn, bk=None, name):
    K = a.shape[1]
    N = w.shape[1]
    bm = rows.bm
    bk = bk or K
    nk = K // bk
    assert K % bk == 0 and N % bn == 0
    in_specs = [pl.BlockSpec((bm, bk), lambda i, j, k: (i, k)),
                pl.BlockSpec((bk, bn), lambda i, j, k: (k, j))]
    in_specs += [pl.BlockSpec(bs, im) for _, bs, im in extras]
    n_ex = len(extras)
    n_out = len(outs)

    def kern(*refs):
        a_ref, w_ref = refs[:2]
        ex = refs[2:2 + n_ex]
        out = refs[2 + n_ex:2 + n_ex + n_out]
        part = jnp.dot(a_ref[...], w_ref[...], preferred_element_type=F32)
        if nk == 1:
            epilogue(part, ex, out)
        else:
            acc_ref = refs[-1]
            k = pl.program_id(2)

            @pl.when(k == 0)
            def _():
                acc_ref[...] = part

            @pl.when(k > 0)
            def _():
                acc_ref[...] += part

            @pl.when(k == nk - 1)
            def _():
                epilogue(acc_ref[...], ex, out)

    res = pl.pallas_call(
        kern,
        grid=(rows.n, N // bn, nk),
        in_specs=in_specs,
        out_specs=[pl.BlockSpec(bs, im) for _, _, bs, im in outs],
        out_shape=[jax.ShapeDtypeStruct(s, d) for s, d, _, _ in outs],
        scratch_shapes=[pltpu.VMEM((bm, bn), F32)] if nk > 1 else [],
        compiler_params=_params(("parallel", "arbitrary", "arbitrary")),
        name=name,
    )(a, w, *[e[0] for e in extras])
    return res


def _rot_half(x, quarter):
    n = x.shape[-1]
    lane = lax.broadcasted_iota(jnp.int32, x.shape, x.ndim - 1)
    first = (lane % (2 * quarter)) < quarter
    return jnp.where(first, -pltpu.roll(x, n - quarter, x.ndim - 1), pltpu.roll(x, quarter, x.ndim - 1))


def _rms(x, g, eps):
    return x * lax.rsqrt(jnp.mean(x * x, axis=-1, keepdims=True) + eps) * g


def gated_matmul(a, w, resid, mod, layer, chunk, rows, *, name, bn=1024, bk=None, bias=None):
    T = rows.n * rows.bm
    N = w.shape[1]
    extras = [(resid, (rows.bm, bn), lambda i, j, k: (i, j)),
              (mod,) + _spec_parts(_mod_spec(layer, chunk, rows, bn, lambda i, j, k: j))]
    if bias is not None:
        extras.append((bias.reshape(1, N), (1, bn), lambda i, j, k: (0, j)))

    def epi(acc, ex, out):
        y = acc if bias is None else acc + ex[2][...]
        out[0][...] = ex[0][...] + ex[1][...] * y

    return fused_matmul(a, w, extras, [((T, N), F32, (rows.bm, bn), lambda i, j, k: (i, j))], epi,
                        rows=rows, bn=bn, bk=bk, name=name)[0]


def _spec_parts(spec):
    return spec.block_shape, spec.index_map


def adaln_table(cvec, ada_w, ada_b):
    depth, D, _ = ada_w.shape
    R = cvec.shape[0]

    def kern(c_ref, w_ref, b_ref, o_ref):
        c = c_ref[...]
        s = c / (1.0 + jnp.exp(-c))
        o_ref[...] = jnp.dot(s, w_ref[...], precision=HIGHEST, preferred_element_type=F32) + b_ref[...]

    out = pl.pallas_call(
        kern,
        grid=(depth, 6),
        in_specs=[pl.BlockSpec((R, D), lambda l, j: (0, 0)),
                  pl.BlockSpec((None, D, D), lambda l, j: (l, 0, j)),
                  pl.BlockSpec((None, 1, D), lambda l, j: (l, 0, j))],
        out_specs=pl.BlockSpec((None, None, R, D), lambda l, j: (l, j, 0, 0)),
        out_shape=jax.ShapeDtypeStruct((depth, 6, R, D), F32),
        compiler_params=_params(("parallel", "arbitrary")),
        name="adaln_table",
    )(cvec, ada_w, ada_b.reshape(depth, 1, 6 * D))
    return out.reshape(depth, 6, R, 1, D)


def norm_modulate(h, norm_g, mod, layer, which, rows):
    D = h.shape[1]
    bm = rows.bm
    g4 = norm_g.reshape(norm_g.shape[0], 2, 1, D)

    def kern(x_ref, g_ref, sh_ref, sc_ref, o_ref):
        y = _rms(x_ref[...], g_ref[...], NORM_EPS)
        o_ref[...] = (y * (1.0 + sc_ref[...]) + sh_ref[...]).astype(BF16)

    return pl.pallas_call(
        kern,
        grid=(rows.n,),
        in_specs=[pl.BlockSpec((bm, D), lambda i: (i, 0)),
                  pl.BlockSpec((None, None, 1, D), lambda i: (layer, which, 0, 0)),
                  _mod_spec(layer, 3 * which, rows, D, lambda i: 0),
                  _mod_spec(layer, 3 * which + 1, rows, D, lambda i: 0)],
        out_specs=pl.BlockSpec((bm, D), lambda i: (i, 0)),
        out_shape=jax.ShapeDtypeStruct((rows.n * bm, D), BF16),
        compiler_params=_params(("parallel",)),
        name="norm_modulate",
    )(h, g4, mod, mod)


def final_norm(h, g, rows):
    D = h.shape[1]
    bm = rows.bm

    def kern(x_ref, g_ref, o_ref):
        o_ref[...] = _rms(x_ref[...], g_ref[...], NORM_EPS)

    return pl.pallas_call(
        kern,
        grid=(rows.n,),
        in_specs=[pl.BlockSpec((bm, D), lambda i: (i, 0)),
                  pl.BlockSpec((1, D), lambda i: (0, 0))],
        out_specs=pl.BlockSpec((bm, D), lambda i: (i, 0)),
        out_shape=jax.ShapeDtypeStruct((rows.n * bm, D), F32),
        compiler_params=_params(("parallel",)),
        name="final_norm",
    )(h, g.reshape(1, D))


def mlp(h, norm_g, mod, layer, w1, w2, rows):
    u = norm_modulate(h, norm_g, mod, layer, 1, rows)
    T = rows.n * rows.bm
    F = w1.shape[1]

    def relu2(acc, ex, out):
        r = jnp.maximum(acc, 0.0)
        out[0][...] = (r * r).astype(BF16)

    hid = fused_matmul(u, w1, [], [((T, F), BF16, (rows.bm, 1024), lambda i, j, k: (i, j))], relu2,
                       rows=rows, bn=1024, name="mlp_up")[0]
    return gated_matmul(hid, w2, h, mod, layer, 5, rows.with_bm(min(512, rows.bm)), bn=512, name="mlp_down")


def axial_rope_tables(L, rot_dim):
    rows = L // GRID_W
    row = jnp.repeat(jnp.arange(rows, dtype=F32), GRID_W)
    col = jnp.tile(jnp.arange(GRID_W, dtype=F32), rows)
    pos = jnp.stack([row, col], axis=-1)
    n_freq = rot_dim // 4
    inv_freq = ROPE_THETA ** (-jnp.arange(n_freq, dtype=F32) / n_freq)
    ang = pos[:, :, None, None] * inv_freq
    ang = jnp.broadcast_to(ang, (L, 2, 2, n_freq)).reshape(L, rot_dim)
    return jnp.cos(ang), jnp.sin(ang)


def rope_tables_padded(L, rot_dim, bm):
    cos, sin = axial_rope_tables(L, rot_dim)
    cos = jnp.pad(cos, ((0, bm), (0, LANES - rot_dim)), constant_values=1.0)
    sin = jnp.pad(sin, ((0, bm), (0, LANES - rot_dim)))
    return cos, sin


def mla_down(u, w_dq, q_g, w_dkv, kv_g, cos, sin, rows):
    T = rows.n * rows.bm
    bm = rows.bm
    qr = w_dq.shape[1]
    wd = jnp.concatenate([w_dq, w_dkv, jnp.zeros((w_dq.shape[0], LANES - MLA_ROPE), w_dq.dtype)],
                         axis=1).astype(BF16)
    n_all = wd.shape[1]
    c0 = qr + MLA_KV_RANK

    def epi(acc, ex, out):
        qg_ref, kvg_ref, cos_ref, sin_ref = ex
        out[0][...] = _rms(acc[:, :qr], qg_ref[...], NORM_EPS).astype(BF16)
        out[1][...] = _rms(acc[:, qr:c0], kvg_ref[...], NORM_EPS).astype(BF16)
        kr = acc[:, c0:]
        out[2][...] = (kr * cos_ref[...] + _rot_half(kr, MLA_ROPE // 4) * sin_ref[...]).astype(BF16)

    extras = [(q_g.reshape(1, qr), (1, qr), lambda i, j, k: (0, 0)),
              (kv_g.reshape(1, MLA_KV_RANK), (1, MLA_KV_RANK), lambda i, j, k: (0, 0)),
              (cos, (bm, LANES), lambda i, j, k: (rows.pos_block(i), 0)),
              (sin, (bm, LANES), lambda i, j, k: (rows.pos_block(i), 0))]
    outs = [((T, qr), BF16, (bm, qr), lambda i, j, k: (i, 0)),
            ((T, MLA_KV_RANK), BF16, (bm, MLA_KV_RANK), lambda i, j, k: (i, 0)),
            ((T, LANES), BF16, (bm, LANES), lambda i, j, k: (i, 0))]
    return fused_matmul(u, wd, extras, outs, epi, rows=rows, bn=n_all, name="mla_down")


def mla_queries(cq, w_uq, cos, sin, rows, heads):
    T = rows.n * rows.bm
    bm = rows.bm
    qr = w_uq.shape[0]
    hd = MLA_NOPE + MLA_ROPE
    w = w_uq.reshape(qr, heads, hd)
    w = jnp.pad(w, ((0, 0), (0, 0), (0, MLA_HEAD_PAD - hd))).reshape(qr, heads * MLA_HEAD_PAD).astype(BF16)
    scale = hd ** -0.5 * LOG2E
    bn = 1024

    def epi(acc, ex, out):
        cos_ref, sin_ref = ex
        for hh in range(bn // MLA_HEAD_PAD):
            c = hh * MLA_HEAD_PAD
            out[0][:, c:c + LANES] = (acc[:, c:c + LANES] * scale).astype(BF16)
            r = acc[:, c + LANES:c + 2 * LANES]
            r = r * cos_ref[...] + _rot_half(r, MLA_ROPE // 4) * sin_ref[...]
            out[0][:, c + LANES:c + 2 * LANES] = (r * scale).astype(BF16)

    extras = [(cos, (bm, LANES), lambda i, j, k: (rows.pos_block(i), 0)),
              (sin, (bm, LANES), lambda i, j, k: (rows.pos_block(i), 0))]
    outs = [((T, heads * MLA_HEAD_PAD), BF16, (bm, bn), lambda i, j, k: (i, j))]
    return fused_matmul(cq, w, extras, outs, epi, rows=rows, bn=bn, name="mla_queries")[0]


def plain_matmul(a, w, rows, bn, *, name, out_dtype=BF16, bias=None):
    T = rows.n * rows.bm
    N = w.shape[1]
    extras = [] if bias is None else [(bias.reshape(1, N), (1, bn), lambda i, j, k: (0, j))]

    def epi(acc, ex, out):
        y = acc if bias is None else acc + ex[0][...]
        out[0][...] = y.astype(out_dtype)

    return fused_matmul(a, w, extras, [((T, N), out_dtype, (rows.bm, bn), lambda i, j, k: (i, j))], epi,
                        rows=rows, bn=bn, name=name)[0]


def _key_ranges(n_keys, parts=2):
    tiles = n_keys // MXU_COLS
    if tiles < parts:
        return [(0, n_keys)]
    cuts = [MXU_COLS * ((tiles * p + parts - 1) // parts) for p in range(parts + 1)]
    return list(zip(cuts[:-1], cuts[1:]))


def mla_attention(q, kv, kr, B, C, L, heads, with_ctx):
    tq = C
    hp = 2
    n_lat_t = L // tq
    n_t = n_lat_t + (1 if with_ctx else 0)
    ctx_blk = B * L // C
    S = C + L
    Tq = B * L + (B * C if with_ctx else 0)
    kvw = 2 * LANES

    def q_row(b, g, t):
        lat = b * n_lat_t + t
        return jnp.where(t < n_lat_t, lat, ctx_blk + b) if with_ctx else lat

    def kern(q_ref, kvl_ref, kvc_ref, krl_ref, krc_ref, o_ref, k_s, v_s):
        t = pl.program_id(2)

        @pl.when(t == 0)
        def _():
            for hh in range(hp):
                c0 = hh * kvw
                k_s[hh, :C, :LANES] = kvc_ref[:, c0:c0 + LANES]
                k_s[hh, :C, LANES:] = krc_ref[...]
                k_s[hh, C:, :LANES] = kvl_ref[:, c0:c0 + LANES]
                k_s[hh, C:, LANES:] = krl_ref[...]
                v_s[hh, :C, :MLA_V] = kvc_ref[:, c0 + LANES:c0 + kvw]
                v_s[hh, C:, :MLA_V] = kvl_ref[:, c0 + LANES:c0 + kvw]
                v_s[hh, :, MLA_V:] = jnp.ones((S, MXU_COLS - MLA_V), BF16)

        def attend(n_keys):
            ss = [lax.dot_general(q_ref[:, hh * MLA_HEAD_PAD:(hh + 1) * MLA_HEAD_PAD], k_s[hh, :n_keys, :],
                                  (((1,), (1,)), ((), ())), preferred_element_type=F32) for hh in range(hp)]
            ps = [jnp.exp2(s - jnp.max(s, axis=-1, keepdims=True)).astype(BF16) for s in ss]
            for hh in range(hp):
                ov = jnp.dot(ps[hh], v_s[hh, :n_keys, :], preferred_element_type=F32)
                o_ref[:, hh * MLA_V:(hh + 1) * MLA_V] = (ov[:, :MLA_V] / ov[:, MLA_V:2 * MLA_V]).astype(BF16)

        if with_ctx:
            @pl.when(t < n_lat_t)
            def _():
                attend(S)

            @pl.when(t == n_lat_t)
            def _():
                attend(C)
        else:
            attend(S)

    return pl.pallas_call(
        kern,
        grid=(B, heads // hp, n_t),
        in_specs=[pl.BlockSpec((tq, hp * MLA_HEAD_PAD), lambda b, g, t: (q_row(b, g, t), g)),
                  pl.BlockSpec((L, hp * kvw), lambda b, g, t: (b, g)),
                  pl.BlockSpec((C, hp * kvw), lambda b, g, t: (ctx_blk + b, g)),
                  pl.BlockSpec((L, LANES), lambda b, g, t: (b, 0)),
                  pl.BlockSpec((C, LANES), lambda b, g, t: (ctx_blk + b, 0))],
        out_specs=pl.BlockSpec((tq, hp * MLA_V), lambda b, g, t: (q_row(b, g, t), g)),
        out_shape=jax.ShapeDtypeStruct((Tq, heads * MLA_V), BF16),
        scratch_shapes=[pltpu.VMEM((hp, S, MLA_HEAD_PAD), BF16), pltpu.VMEM((hp, S, MXU_COLS), BF16)],
        compiler_params=_params(("parallel", "parallel", "arbitrary")),
        name="mla_attention",
    )(q, kv, kv, kr, kr)


def mla_mixer(h, u, mod, layer, p, geo, need_ctx):
    B, C, L, rows_all, rows_out = geo
    heads = p["w_uq"].shape[1] // (MLA_NOPE + MLA_ROPE)
    cos, sin = rope_tables_padded(L, MLA_ROPE, rows_all.bm)
    cq, ckv, kr = mla_down(u, p["w_dq"], p["q_g"], p["w_dkv"], p["kv_g"], cos, sin, rows_all)
    kv = plain_matmul(ckv, p["w_ukv"].astype(BF16), rows_all, 1024, name="mla_kv")
    q = mla_queries(cq, p["w_uq"], cos, sin, rows_out, heads)
    o = mla_attention(q, kv, kr, B, C, L, heads, need_ctx)
    return gated_matmul(o, p["w_o"].astype(BF16), h, mod, layer, 2, rows_out, name="mla_out")


def _diff_pair_layout():
    quarter = DF_HEAD_DIM // 4
    n = np.arange(2 * DF_HEAD_DIM)
    hf, c, a, r = n // DF_HEAD_DIM, (n % DF_HEAD_DIM) // (2 * quarter), (n % (2 * quarter)) // quarter, n % quarter
    head_perm = c * DF_HEAD_DIM + a * 2 * quarter + hf * quarter + r
    table_cols = (a * 2 * quarter + r)[:DF_HEAD_DIM]
    return head_perm, table_cols


def diff_rope_tables(L, bm):
    cos, sin = axial_rope_tables(L, DF_HEAD_DIM)
    _, cols = _diff_pair_layout()
    cos = jnp.pad(cos[:, cols], ((0, bm), (0, 0)), constant_values=1.0)
    sin = jnp.pad(sin[:, cols], ((0, bm), (0, 0)))
    return cos, sin


def diff_qkv(u, w_qkv, cos, sin, rows):
    T = rows.n * rows.bm
    bm = rows.bm
    D = w_qkv.shape[0]
    bn = 1024
    hw = 2 * DF_HEAD_DIM
    scale = DF_HEAD_DIM ** -0.5 * LOG2E
    n_q = D // bn
    head_perm, _ = _diff_pair_layout()
    qk_cols = (np.arange(2 * D) // hw * hw)[:, None].reshape(-1, hw) + head_perm[None, :]
    cols = np.concatenate([qk_cols.reshape(-1), np.arange(2 * D, 3 * D)])
    w = w_qkv[:, cols].astype(BF16)

    def epi(acc, ex, out):
        cos_ref, sin_ref = ex
        j = pl.program_id(1)

        @pl.when(j < 2 * n_q)
        def _():
            mul = jnp.where(j < n_q, scale, 1.0).astype(F32)
            cs, sn = cos_ref[...] * mul, sin_ref[...] * mul
            for s in range(bn // hw):
                x1 = acc[:, s * hw:s * hw + LANES]
                x2 = acc[:, s * hw + LANES:(s + 1) * hw]
                out[0][:, s * hw:s * hw + LANES] = (x1 * cs - x2 * sn).astype(BF16)
                out[0][:, s * hw + LANES:(s + 1) * hw] = (x2 * cs + x1 * sn).astype(BF16)

        @pl.when(j >= 2 * n_q)
        def _():
            out[0][...] = acc.astype(BF16)

    extras = [(cos, (bm, LANES), lambda i, j, k: (rows.pos_block(i), 0)),
              (sin, (bm, LANES), lambda i, j, k: (rows.pos_block(i), 0))]
    outs = [((T, 3 * D), BF16, (bm, bn), lambda i, j, k: (i, j))]
    return fused_matmul(u, w, extras, outs, epi, rows=rows, bn=bn, name="diff_qkv")[0]


def diff_attention(qkv, lambdas, subln_g, lambda_init, B, C, L, D, with_ctx):
    hd = DF_HEAD_DIM
    hw = 2 * hd
    heads = D // hw
    tq = C
    n_lat_t = L // tq
    n_t = n_lat_t + (1 if with_ctx else 0)
    ctx_blk = B * L // C
    S = C + L
    Tq = B * L + (B * C if with_ctx else 0)
    kcol = D // hw
    vcol = 2 * D // hw

    def q_row(b, h, t):
        lat = b * n_lat_t + t
        return jnp.where(t < n_lat_t, lat, ctx_blk + b) if with_ctx else lat

    def kern(q_ref, kl_ref, kc_ref, vl_ref, vc_ref, lam_ref, g_ref, o_ref, k1_s, k2_s, v_s):
        t = pl.program_id(2)

        @pl.when(t == 0)
        def _():
            map0 = (lax.broadcasted_iota(jnp.int32, (1, hw), 1) % hd) < hd // 2
            for ref, lo, hi in ((kc_ref, 0, C), (kl_ref, C, S)):
                k = ref[...]
                k1_s[lo:hi, :] = jnp.where(map0, k, jnp.zeros_like(k))
                k2_s[lo:hi, :] = jnp.where(map0, jnp.zeros_like(k), k)
            v_s[:C, :] = vc_ref[...]
            v_s[C:, :] = vl_ref[...]

        lf = lam_ref[...]
        lam = (jnp.exp(jnp.sum(lf[0:1] * lf[1:2], axis=-1, keepdims=True))
               - jnp.exp(jnp.sum(lf[2:3] * lf[3:4], axis=-1, keepdims=True)) + lambda_init)

        def attend(n_keys):
            ss = [lax.dot_general(q_ref[...], k_s[:n_keys, :], (((1,), (1,)), ((), ())),
                                  preferred_element_type=F32) for k_s in (k1_s, k2_s)]
            ms = [jnp.max(s, axis=-1, keepdims=True) for s in ss]
            os_, ls = [0.0, 0.0], [0.0, 0.0]
            for lo in range(0, n_keys, MXU_COLS):
                for c in range(2):
                    p = jnp.exp2(ss[c][:, lo:lo + MXU_COLS] - ms[c])
                    ls[c] = ls[c] + jnp.sum(p[:, :LANES] + p[:, LANES:], axis=-1, keepdims=True)
                    os_[c] = os_[c] + jnp.dot(p.astype(BF16), v_s[lo:lo + MXU_COLS, :],
                                              preferred_element_type=F32)
            o = os_[0] / ls[0] - lam * (os_[1] / ls[1])
            o_ref[...] = (_rms(o, g_ref[...], DF_SUBLN_EPS) * (1.0 - lambda_init)).astype(BF16)

        if with_ctx:
            @pl.when(t < n_lat_t)
            def _():
                attend(S)

            @pl.when(t == n_lat_t)
            def _():
                attend(C)
        else:
            attend(S)

    return pl.pallas_call(
        kern,
        grid=(B, heads, n_t),
        in_specs=[pl.BlockSpec((tq, hw), lambda b, h, t: (q_row(b, h, t), h)),
                  pl.BlockSpec((L, hw), lambda b, h, t: (b, kcol + h)),
                  pl.BlockSpec((C, hw), lambda b, h, t: (ctx_blk + b, kcol + h)),
                  pl.BlockSpec((L, hw), lambda b, h, t: (b, vcol + h)),
                  pl.BlockSpec((C, hw), lambda b, h, t: (ctx_blk + b, vcol + h)),
                  pl.BlockSpec((4, hd), lambda b, h, t: (0, 0)),
                  pl.BlockSpec((1, hw), lambda b, h, t: (0, 0))],
        out_specs=pl.BlockSpec((tq, hw), lambda b, h, t: (q_row(b, h, t), h)),
        out_shape=jax.ShapeDtypeStruct((Tq, D), BF16),
        scratch_shapes=[pltpu.VMEM((S, hw), BF16), pltpu.VMEM((S, hw), BF16), pltpu.VMEM((S, hw), BF16)],
        compiler_params=_params(("parallel", "parallel", "arbitrary")),
        name="diff_attention",
    )(qkv, qkv, qkv, qkv, qkv, lambdas, subln_g.reshape(1, hw))


def diff_mixer(h, u, mod, layer, p, geo, need_ctx, lambda_init):
    B, C, L, rows_all, rows_out = geo
    D = h.shape[1]
    cos, sin = diff_rope_tables(L, rows_all.bm)
    qkv = diff_qkv(u, p["w_qkv"], cos, sin, rows_all)
    o = diff_attention(qkv, p["lambdas"], p["subln_g"], lambda_init, B, C, L, D, need_ctx)
    return gated_matmul(o, p["w_o"].astype(BF16), h, mod, layer, 2, rows_out, name="diff_out")


def dft_matrices(Ls):
    n = 2 * Ls
    t0n = min(64, Ls)
    t1n = Ls // t0n
    f = jnp.arange(Ls, dtype=jnp.int32)[:, None]
    a1 = ((f * (jnp.arange(t1n, dtype=jnp.int32) * t0n)[None, :]) % n).astype(F32) * (2.0 * math.pi / n)
    a0 = ((f * jnp.arange(t0n, dtype=jnp.int32)[None, :]) % n).astype(F32) * (2.0 * math.pi / n)
    c1, s1 = jnp.cos(a1)[:, :, None], jnp.sin(a1)[:, :, None]
    c0, s0 = jnp.cos(a0)[:, None, :], jnp.sin(a0)[:, None, :]
    cosm = (c1 * c0 - s1 * s0).reshape(Ls, Ls)
    sinm = (s1 * c0 + c1 * s0).reshape(Ls, Ls)
    nyq = jnp.where(jnp.arange(Ls) % 2 == 0, 1.0, -1.0).astype(F32)[None, :]
    imag = jnp.where(f == 0, nyq, -sinm)
    fwd = jnp.concatenate([cosm, imag], axis=0)
    col = jnp.arange(n)
    cscale = jnp.where((col == 0) | (col == Ls), 1.0 / n, 2.0 / n).astype(F32)
    inv = fwd.T * cscale[None, :]
    return fwd.astype(BF16), inv.astype(BF16)


def hyena_filter_time(Ls, p):
    D = p["f_bias"].shape[0]
    order = p["f_w2"].shape[0]
    bands = (HY_EMB_DIM - 1) // 2
    t = jnp.linspace(0.0, 1.0, Ls, dtype=F32)[:, None]
    w = 2.0 * math.pi * jnp.arange(Ls, dtype=F32)[:, None] / Ls
    f = jnp.linspace(1e-4, bands - 1, bands, dtype=F32)
    feats = jnp.concatenate([t, jnp.cos(f * w), -jnp.sin(f * w)], axis=-1)
    feats = jnp.pad(feats, ((0, 0), (0, HY_EMB_PAD - HY_EMB_DIM)))
    w1 = jnp.pad(p["f_w1"], ((0, HY_EMB_PAD - HY_EMB_DIM), (0, 0)))
    deltas = jnp.abs(jnp.linspace(math.log(HY_TARGET) / HY_SLOW_PCT, math.log(HY_TARGET) / HY_FAST_PCT,
                                  D, dtype=F32))
    deltas2 = jnp.concatenate([deltas, deltas]).reshape(1, 2 * D)

    def ffn_kern(x_ref, w1_ref, b1_ref, w2_ref, b2_ref, w3_ref, b3_ref, fr_ref, o_ref):
        dot = functools.partial(jnp.dot, precision=HIGHEST, preferred_element_type=F32)
        fr = fr_ref[...]
        hcur = jnp.sin(fr[0:1] * (dot(x_ref[...], w1_ref[...]) + b1_ref[...]))
        hcur = jnp.sin(fr[1:2] * (dot(hcur, w2_ref[...]) + b2_ref[...]))
        o_ref[...] = jnp.sin(fr[2:3] * (dot(hcur, w3_ref[...]) + b3_ref[...]))

    full = lambda a: pl.BlockSpec(a.shape, lambda: (0,) * a.ndim)
    ffn_in = [feats, w1, p["f_b1"].reshape(1, order), p["f_w2"], p["f_b2"].reshape(1, order),
              p["f_w3"], p["f_b3"].reshape(1, order), p["f_freq"]]
    hff = pl.pallas_call(
        ffn_kern,
        in_specs=[full(a) for a in ffn_in],
        out_specs=pl.BlockSpec((Ls, order), lambda: (0, 0)),
        out_shape=jax.ShapeDtypeStruct((Ls, order), F32),
        name="hyena_filter_ffn",
    )(*ffn_in)

    bn = 512

    def out_kern(h_ref, w_ref, d_ref, o_ref):
        tt = lax.broadcasted_iota(jnp.int32, (Ls, 1), 0).astype(F32) * (1.0 / (Ls - 1))
        hw = jnp.dot(h_ref[...], w_ref[...], precision=HIGHEST, preferred_element_type=F32)
        o_ref[...] = hw * jnp.exp(-tt * d_ref[...])

    return pl.pallas_call(
        out_kern,
        grid=(2 * D // bn,),
        in_specs=[pl.BlockSpec((Ls, order), lambda j: (0, 0)),
                  pl.BlockSpec((order, bn), lambda j: (0, j)),
                  pl.BlockSpec((1, bn), lambda j: (0, j))],
        out_specs=pl.BlockSpec((Ls, bn), lambda j: (0, j)),
        out_shape=jax.ShapeDtypeStruct((Ls, 2 * D), F32),
        compiler_params=_params(("parallel",)),
        name="hyena_filter_out",
    )(hff, p["f_wout"], deltas2)


def hyena_filter_spectrum(hfb, fwd, Ls, D):
    n = 2 * Ls
    cb = LANES
    ncb = D // cb

    def kern(w_ref, hf_ref, hb_ref, o_ref):
        row = lax.broadcasted_iota(jnp.int32, (Ls, 1), 0)
        hb = jnp.where(row == 0, 0.0, hb_ref[...])
        a = jnp.dot(w_ref[...], hf_ref[...].astype(BF16), preferred_element_type=F32)
        b = jnp.dot(w_ref[...], hb.astype(BF16), preferred_element_type=F32)
        frow = lax.broadcasted_iota(jnp.int32, (n, 1), 0)
        o_ref[...] = a + jnp.where(frow > Ls, -b, b)

    return pl.pallas_call(
        kern,
        grid=(ncb,),
        in_specs=[pl.BlockSpec((n, Ls), lambda j: (0, 0), pipeline_mode=pl.Buffered(1)),
                  pl.BlockSpec((Ls, cb), lambda j: (0, j)),
                  pl.BlockSpec((Ls, cb), lambda j: (0, ncb + j))],
        out_specs=pl.BlockSpec((n, cb), lambda j: (0, j)),
        out_shape=jax.ShapeDtypeStruct((n, D), F32),
        compiler_params=_params(("parallel",)),
        name="hyena_filter_spectrum",
    )(fwd, hfb, hfb)


def _short_conv(x, w, b):
    n = x.shape[0]
    row = lax.broadcasted_iota(jnp.int32, (n, 1), 0)
    prev = jnp.where(row == 0, 0.0, pltpu.roll(x, 1, 0))
    nxt = jnp.where(row == n - 1, 0.0, pltpu.roll(x, n - 1, 0))
    return prev * w[0:1] + x * w[1:2] + nxt * w[2:3] + b


def hyena_segment(z, p, Ls, row_blk0, B, D):
    n = 2 * Ls
    cb = MXU_COLS
    ncb = D // cb
    fwd, inv = dft_matrices(Ls)
    hfb = hyena_filter_time(Ls, p)
    kf = hyena_filter_spectrum(hfb, fwd, Ls, D)
    conv_w, conv_b = p["conv_w"], p["conv_b"].reshape(1, 3 * D)

    def gate_kern(x1_ref, v_ref, cw1_ref, cb1_ref, cwv_ref, cbv_ref, g_ref):
        g_ref[...] = (_short_conv(v_ref[...].astype(F32), cwv_ref[...], cbv_ref[...])
                      * _short_conv(x1_ref[...].astype(F32), cw1_ref[...], cb1_ref[...])).astype(BF16)

    g = pl.pallas_call(
        gate_kern,
        grid=(ncb, B),
        in_specs=[pl.BlockSpec((Ls, cb), lambda j, b: (row_blk0 + b, ncb + j)),
                  pl.BlockSpec((Ls, cb), lambda j, b: (row_blk0 + b, 2 * ncb + j)),
                  pl.BlockSpec((HY_SHORT, cb), lambda j, b: (0, ncb + j)),
                  pl.BlockSpec((1, cb), lambda j, b: (0, ncb + j)),
                  pl.BlockSpec((HY_SHORT, cb), lambda j, b: (0, 2 * ncb + j)),
                  pl.BlockSpec((1, cb), lambda j, b: (0, 2 * ncb + j))],
        out_specs=pl.BlockSpec((Ls, cb), lambda j, b: (b, j)),
        out_shape=jax.ShapeDtypeStruct((B * Ls, D), BF16),
        compiler_params=_params(("parallel", "arbitrary")),
        name="hyena_gate",
    )(z, z, conv_w, conv_b, conv_w, conv_b)

    def fwd_kern(w_ref, g_ref, kf_ref, y_ref):
        u = jnp.dot(w_ref[...], g_ref[...], preferred_element_type=F32)
        ure, uim = u[:Ls], u[Ls:]
        kre, kim = kf_ref[:Ls, :], kf_ref[Ls:, :]
        first = lax.broadcasted_iota(jnp.int32, (Ls, 1), 0) == 0
        y_ref[:Ls, :] = (ure * kre - jnp.where(first, 0.0, uim * kim)).astype(BF16)
        y_ref[Ls:, :] = jnp.where(first, uim * kim, ure * kim + uim * kre).astype(BF16)

    y = pl.pallas_call(
        fwd_kern,
        grid=(ncb, B),
        in_specs=[pl.BlockSpec((n, Ls), lambda j, b: (0, 0), pipeline_mode=pl.Buffered(1)),
                  pl.BlockSpec((Ls, cb), lambda j, b: (b, j)),
                  pl.BlockSpec((n, cb), lambda j, b: (0, j))],
        out_specs=pl.BlockSpec((None, n, cb), lambda j, b: (b, 0, j)),
        out_shape=jax.ShapeDtypeStruct((B, n, D), BF16),
        compiler_params=_params(("parallel", "arbitrary")),
        name="hyena_dft",
    )(fwd, g, kf)

    def inv_kern(w_ref, y_ref, g_ref, x0_ref, cw0_ref, cb0_ref, fb_ref, o_ref):
        conv = jnp.dot(w_ref[...], y_ref[...], preferred_element_type=F32)
        x0 = _short_conv(x0_ref[...].astype(F32), cw0_ref[...], cb0_ref[...])
        o_ref[...] = ((conv + g_ref[...].astype(F32) * fb_ref[...]) * x0).astype(BF16)

    return pl.pallas_call(
        inv_kern,
        grid=(ncb, B),
        in_specs=[pl.BlockSpec((Ls, n), lambda j, b: (0, 0), pipeline_mode=pl.Buffered(1)),
                  pl.BlockSpec((None, n, cb), lambda j, b: (b, 0, j)),
                  pl.BlockSpec((Ls, cb), lambda j, b: (b, j)),
                  pl.BlockSpec((Ls, cb), lambda j, b: (row_blk0 + b, j)),
                  pl.BlockSpec((HY_SHORT, cb), lambda j, b: (0, j)),
                  pl.BlockSpec((1, cb), lambda j, b: (0, j)),
                  pl.BlockSpec((1, cb), lambda j, b: (0, j))],
        out_specs=pl.BlockSpec((Ls, cb), lambda j, b: (b, j)),
        out_shape=jax.ShapeDtypeStruct((B * Ls, D), BF16),
        compiler_params=_params(("parallel", "arbitrary")),
        name="hyena_idft",
    )(inv, y, g, z, conv_w, conv_b, p["f_bias"].reshape(1, D))


def hyena_mixer(h, u, mod, layer, p, geo, need_ctx):
    B, C, L, rows_all, rows_out = geo
    D = h.shape[1]
    z = plain_matmul(u, p["w_in"].astype(BF16), rows_out, 1024, bias=p["b_in"], name="hyena_in")
    y = hyena_segment(z, p, L, 0, B, D)
    if need_ctx:
        y = jnp.concatenate([y, hyena_segment(z, p, C, B * L // C, B, D)], axis=0)
    return gated_matmul(y, p["w_out"].astype(BF16), h, mod, layer, 2, rows_out, bias=p["b_out"], name="hyena_out")


def kernel(x, c, ctx, c_ctx, ada_w, ada_b, norm_g, mlp_w1, mlp_w2, final_norm_g, mla_w_dq, mla_q_norm_g, mla_w_uq, mla_w_dkv, mla_kv_norm_g, mla_w_ukv, mla_w_o, hy_w_in, hy_b_in, hy_conv_w, hy_conv_b, hy_filt_w1, hy_filt_b1, hy_filt_w2, hy_filt_b2, hy_filt_w3, hy_filt_b3, hy_filt_freq, hy_filt_wout, hy_filt_bias, hy_w_out, hy_b_out, df_w_qkv, df_lambda, df_subln_g, df_w_o):
    B, L, D = x.shape
    C = ctx.shape[1]
    depth = ada_w.shape[0]
    bm = min(1024, B * C)
    rows_all = Rows(B, C, L, bm)
    rows_lat = Rows(B, C, L, bm, lat_only=True)

    mod = adaln_table(jnp.concatenate([c_ctx[None, :], c], axis=0), ada_w, ada_b)
    h = jnp.concatenate([x.reshape(B * L, D), ctx.reshape(B * C, D)], axis=0)

    for i in range(depth):
        need_ctx = i < depth - 1
        rows_out = rows_all if need_ctx else rows_lat
        geo = (B, C, L, rows_all, rows_out)
        kind, j = i % N_MIXERS, i // N_MIXERS
        u = norm_modulate(h, norm_g, mod, i, 0, rows_all)
        if kind == 0:
            p = dict(w_dq=mla_w_dq[j], q_g=mla_q_norm_g[j], w_uq=mla_w_uq[j], w_dkv=mla_w_dkv[j],
                     kv_g=mla_kv_norm_g[j], w_ukv=mla_w_ukv[j], w_o=mla_w_o[j])
            h = mla_mixer(h, u, mod, i, p, geo, need_ctx)
        elif kind == 1:
            p = dict(w_in=hy_w_in[j], b_in=hy_b_in[j], conv_w=hy_conv_w[j], conv_b=hy_conv_b[j],
                     f_w1=hy_filt_w1[j], f_b1=hy_filt_b1[j], f_w2=hy_filt_w2[j], f_b2=hy_filt_b2[j],
                     f_w3=hy_filt_w3[j], f_b3=hy_filt_b3[j], f_freq=hy_filt_freq[j], f_wout=hy_filt_wout[j],
                     f_bias=hy_filt_bias[j], w_out=hy_w_out[j], b_out=hy_b_out[j])
            h = hyena_mixer(h, u, mod, i, p, geo, need_ctx)
        else:
            lambda_init = 0.8 - 0.6 * math.exp(-0.3 * i)
            p = dict(w_qkv=df_w_qkv[j], lambdas=df_lambda[j], subln_g=df_subln_g[j], w_o=df_w_o[j])
            h = diff_mixer(h, u, mod, i, p, geo, need_ctx, lambda_init)
        h = mlp(h, norm_g, mod, i, mlp_w1[i].astype(BF16), mlp_w2[i].astype(BF16), rows_out)

    return final_norm(h, final_norm_g, rows_lat).reshape(B, L, D)
```

```python
import functools
import math

import jax
import jax.numpy as jnp
import numpy as np
from jax import lax
from jax.experimental import pallas as pl
from jax.experimental.pallas import tpu as pltpu

F32 = jnp.float32
BF16 = jnp.bfloat16
HIGHEST = lax.Precision.HIGHEST
LOG2E = math.log2(math.e)

GRID_W = 64
ROPE_THETA = 10000.0
NORM_EPS = 1e-6
N_MIXERS = 3

MLA_NOPE = 128
MLA_ROPE = 64
MLA_V = 128
MLA_KV_RANK = 512
MLA_HEAD_PAD = 256

HY_SHORT = 3
HY_EMB_DIM = 33
HY_EMB_PAD = 64
HY_TARGET = 1e-2
HY_FAST_PCT = 0.3
HY_SLOW_PCT = 1.5

DF_HEAD_DIM = 128
DF_SUBLN_EPS = 1e-5

LANES = 128
MXU_COLS = 256
VMEM_LIMIT_MB = 56


def _params(semantics, vmem_mb=VMEM_LIMIT_MB):
    return pltpu.CompilerParams(dimension_semantics=semantics, vmem_limit_bytes=vmem_mb << 20)


class Rows:
    def __init__(self, B, C, L, bm, lat_only=False):
        assert L % bm == 0 and (B * C) % bm == 0
        self.geometry = (B, C, L, lat_only)
        self.bm = bm
        self.tiles_per_batch = L // bm
        self.n_lat = B * L // bm
        self.n = self.n_lat + (0 if lat_only else B * C // bm)

    def with_bm(self, bm):
        B, C, L, lat_only = self.geometry
        return Rows(B, C, L, bm, lat_only)

    def mod_row(self, i):
        return jnp.where(i < self.n_lat, 1 + i // self.tiles_per_batch, 0)

    def pos_block(self, i):
        return jnp.where(i < self.n_lat, i % self.tiles_per_batch, self.tiles_per_batch)


def _mod_spec(layer, chunk, rows, bn, col_of):
    return pl.BlockSpec((None, None, None, 1, bn),
                        lambda *g: (layer, chunk, rows.mod_row(g[0]), 0, col_of(*g)))


def fused_matmul(a, w, extras, outs, epilogue, *, rows, bn, bk=None, name):
    K = a.shape[1]
    N = w.shape[1]
    bm = rows.bm
    bk = bk or K
    nk = K // bk
    assert K % bk == 0 and N % bn == 0
    in_specs = [pl.BlockSpec((bm, bk), lambda i, j, k: (i, k)),
                pl.BlockSpec((bk, bn), lambda i, j, k: (k, j))]
    in_specs += [pl.BlockSpec(bs, im) for _, bs, im in extras]
    n_ex = len(extras)
    n_out = len(outs)

    def kern(*refs):
        a_ref, w_ref = refs[:2]
        ex = refs[2:2 + n_ex]
        out = refs[2 + n_ex:2 + n_ex + n_out]
        def dot():
            return jnp.dot(a_ref[...], w_ref[...], preferred_element_type=F32)

        if nk == 1:
            epilogue(dot(), ex, out)
        else:
            acc_ref = out[0]
            k = pl.program_id(2)

            @pl.when(k == 0)
            def _():
                acc_ref[...] = dot()

            @pl.when((k > 0) & (k < nk - 1))
            def _():
                acc_ref[...] += dot()

            @pl.when(k == nk - 1)
            def _():
                epilogue(acc_ref[...] + dot(), ex, out)

    res = pl.pallas_call(
        kern,
        grid=(rows.n, N // bn, nk),
        in_specs=in_specs,
        out_specs=[pl.BlockSpec(bs, im) for _, _, bs, im in outs],
        out_shape=[jax.ShapeDtypeStruct(s, d) for s, d, _, _ in outs],
        compiler_params=_params(("parallel", "arbitrary", "arbitrary")),
        name=name,
    )(a, w, *[e[0] for e in extras])
    return res


def _rot_half(x, quarter):
    n = x.shape[-1]
    lane = lax.broadcasted_iota(jnp.int32, x.shape, x.ndim - 1)
    first = (lane % (2 * quarter)) < quarter
    return jnp.where(first, -pltpu.roll(x, n - quarter, x.ndim - 1), pltpu.roll(x, quarter, x.ndim - 1))


def _rms(x, g, eps):
    return x * lax.rsqrt(jnp.mean(x * x, axis=-1, keepdims=True) + eps) * g


def gated_matmul(a, w, resid, mod, layer, chunk, rows, *, name, bn=1024, bk=None, bias=None):
    T = rows.n * rows.bm
    N = w.shape[1]
    extras = [(resid, (rows.bm, bn), lambda i, j, k: (i, j)),
              (mod,) + _spec_parts(_mod_spec(layer, chunk, rows, bn, lambda i, j, k: j))]
    if bias is not None:
        extras.append((bias.reshape(1, N), (1, bn), lambda i, j, k: (0, j)))

    def epi(acc, ex, out):
        y = acc if bias is None else acc + ex[2][...]
        out[0][...] = ex[0][...] + ex[1][...] * y

    return fused_matmul(a, w, extras, [((T, N), F32, (rows.bm, bn), lambda i, j, k: (i, j))], epi,
                        rows=rows, bn=bn, bk=bk, name=name)[0]


def _spec_parts(spec):
    return spec.block_shape, spec.index_map


def adaln_table(cvec, ada_w, ada_b):
    depth, D, _ = ada_w.shape
    R = cvec.shape[0]

    def kern(c_ref, w_ref, b_ref, o_ref):
        c = c_ref[...]
        s = c / (1.0 + jnp.exp(-c))
        o_ref[...] = jnp.dot(s, w_ref[...], precision=HIGHEST, preferred_element_type=F32) + b_ref[...]

    out = pl.pallas_call(
        kern,
        grid=(depth, 6),
        in_specs=[pl.BlockSpec((R, D), lambda l, j: (0, 0)),
                  pl.BlockSpec((None, D, D), lambda l, j: (l, 0, j)),
                  pl.BlockSpec((None, 1, D), lambda l, j: (l, 0, j))],
        out_specs=pl.BlockSpec((None, None, R, D), lambda l, j: (l, j, 0, 0)),
        out_shape=jax.ShapeDtypeStruct((depth, 6, R, D), F32),
        compiler_params=_params(("parallel", "arbitrary")),
        name="adaln_table",
    )(cvec, ada_w, ada_b.reshape(depth, 1, 6 * D))
    return out.reshape(depth, 6, R, 1, D)


def norm_modulate(h, norm_g, mod, layer, which, rows):
    D = h.shape[1]
    bm = rows.bm
    g4 = norm_g.reshape(norm_g.shape[0], 2, 1, D)

    def kern(x_ref, g_ref, sh_ref, sc_ref, o_ref):
        y = _rms(x_ref[...], g_ref[...], NORM_EPS)
        o_ref[...] = (y * (1.0 + sc_ref[...]) + sh_ref[...]).astype(BF16)

    return pl.pallas_call(
        kern,
        grid=(rows.n,),
        in_specs=[pl.BlockSpec((bm, D), lambda i: (i, 0)),
                  pl.BlockSpec((None, None, 1, D), lambda i: (layer, which, 0, 0)),
                  _mod_spec(layer, 3 * which, rows, D, lambda i: 0),
                  _mod_spec(layer, 3 * which + 1, rows, D, lambda i: 0)],
        out_specs=pl.BlockSpec((bm, D), lambda i: (i, 0)),
        out_shape=jax.ShapeDtypeStruct((rows.n * bm, D), BF16),
        compiler_params=_params(("parallel",)),
        name="norm_modulate",
    )(h, g4, mod, mod)


def final_norm(h, g, rows):
    D = h.shape[1]
    bm = rows.bm

    def kern(x_ref, g_ref, o_ref):
        o_ref[...] = _rms(x_ref[...], g_ref[...], NORM_EPS)

    return pl.pallas_call(
        kern,
        grid=(rows.n,),
        in_specs=[pl.BlockSpec((bm, D), lambda i: (i, 0)),
                  pl.BlockSpec((1, D), lambda i: (0, 0))],
        out_specs=pl.BlockSpec((bm, D), lambda i: (i, 0)),
        out_shape=jax.ShapeDtypeStruct((rows.n * bm, D), F32),
        compiler_params=_params(("parallel",)),
        name="final_norm",
    )(h, g.reshape(1, D))


def mlp(h, norm_g, mod, layer, w1, w2, rows):
    u = norm_modulate(h, norm_g, mod, layer, 1, rows)
    T = rows.n * rows.bm
    F = w1.shape[1]

    def relu2(acc, ex, out):
        r = jnp.maximum(acc, 0.0)
        out[0][...] = (r * r).astype(BF16)

    hid = fused_matmul(u, w1, [], [((T, F), BF16, (rows.bm, 1024), lambda i, j, k: (i, j))], relu2,
                       rows=rows, bn=1024, name="mlp_up")[0]
    return gated_matmul(hid, w2, h, mod, layer, 5, rows, bn=w2.shape[1], bk=1024, name="mlp_down")


def axial_rope_tables(L, rot_dim):
    rows = L // GRID_W
    row = jnp.repeat(jnp.arange(rows, dtype=F32), GRID_W)
    col = jnp.tile(jnp.arange(GRID_W, dtype=F32), rows)
    pos = jnp.stack([row, col], axis=-1)
    n_freq = rot_dim // 4
    inv_freq = ROPE_THETA ** (-jnp.arange(n_freq, dtype=F32) / n_freq)
    ang = pos[:, :, None, None] * inv_freq
    ang = jnp.broadcast_to(ang, (L, 2, 2, n_freq)).reshape(L, rot_dim)
    return jnp.cos(ang), jnp.sin(ang)


def rope_tables_padded(L, rot_dim, bm):
    cos, sin = axial_rope_tables(L, rot_dim)
    cos = jnp.pad(cos, ((0, bm), (0, LANES - rot_dim)), constant_values=1.0)
    sin = jnp.pad(sin, ((0, bm), (0, LANES - rot_dim)))
    return cos, sin


def mla_down(u, w_dq, q_g, w_dkv, kv_g, cos, sin, rows):
    T = rows.n * rows.bm
    bm = rows.bm
    qr = w_dq.shape[1]
    wd = jnp.concatenate([w_dq, w_dkv, jnp.zeros((w_dq.shape[0], LANES - MLA_ROPE), w_dq.dtype)],
                         axis=1).astype(BF16)
    n_all = wd.shape[1]
    c0 = qr + MLA_KV_RANK

    def epi(acc, ex, out):
        qg_ref, kvg_ref, cos_ref, sin_ref = ex
        out[0][...] = _rms(acc[:, :qr], qg_ref[...], NORM_EPS).astype(BF16)
        out[1][...] = _rms(acc[:, qr:c0], kvg_ref[...], NORM_EPS).astype(BF16)
        kr = acc[:, c0:]
        out[2][...] = (kr * cos_ref[...] + _rot_half(kr, MLA_ROPE // 4) * sin_ref[...]).astype(BF16)

    extras = [(q_g.reshape(1, qr), (1, qr), lambda i, j, k: (0, 0)),
              (kv_g.reshape(1, MLA_KV_RANK), (1, MLA_KV_RANK), lambda i, j, k: (0, 0)),
              (cos, (bm, LANES), lambda i, j, k: (rows.pos_block(i), 0)),
              (sin, (bm, LANES), lambda i, j, k: (rows.pos_block(i), 0))]
    outs = [((T, qr), BF16, (bm, qr), lambda i, j, k: (i, 0)),
            ((T, MLA_KV_RANK), BF16, (bm, MLA_KV_RANK), lambda i, j, k: (i, 0)),
            ((T, LANES), BF16, (bm, LANES), lambda i, j, k: (i, 0))]
    return fused_matmul(u, wd, extras, outs, epi, rows=rows, bn=n_all, name="mla_down")


def mla_queries(cq, w_uq, cos, sin, rows, heads):
    T = rows.n * rows.bm
    bm = rows.bm
    qr = w_uq.shape[0]
    hd = MLA_NOPE + MLA_ROPE
    w = w_uq.reshape(qr, heads, hd)
    w = jnp.pad(w, ((0, 0), (0, 0), (0, MLA_HEAD_PAD - hd))).reshape(qr, heads * MLA_HEAD_PAD).astype(BF16)
    scale = hd ** -0.5 * LOG2E
    bn = 1024

    def epi(acc, ex, out):
        cos_ref, sin_ref = ex
        for hh in range(bn // MLA_HEAD_PAD):
            c = hh * MLA_HEAD_PAD
            out[0][:, c:c + LANES] = (acc[:, c:c + LANES] * scale).astype(BF16)
            r = acc[:, c + LANES:c + 2 * LANES]
            r = r * cos_ref[...] + _rot_half(r, MLA_ROPE // 4) * sin_ref[...]
            out[0][:, c + LANES:c + 2 * LANES] = (r * scale).astype(BF16)

    extras = [(cos, (bm, LANES), lambda i, j, k: (rows.pos_block(i), 0)),
              (sin, (bm, LANES), lambda i, j, k: (rows.pos_block(i), 0))]
    outs = [((T, heads * MLA_HEAD_PAD), BF16, (bm, bn), lambda i, j, k: (i, j))]
    return fused_matmul(cq, w, extras, outs, epi, rows=rows, bn=bn, name="mla_queries")[0]


def plain_matmul(a, w, rows, bn, *, name, out_dtype=BF16, bias=None):
    T = rows.n * rows.bm
    N = w.shape[1]
    extras = [] if bias is None else [(bias.reshape(1, N), (1, bn), lambda i, j, k: (0, j))]

    def epi(acc, ex, out):
        y = acc if bias is None else acc + ex[0][...]
        out[0][...] = y.astype(out_dtype)

    return fused_matmul(a, w, extras, [((T, N), out_dtype, (rows.bm, bn), lambda i, j, k: (i, j))], epi,
                        rows=rows, bn=bn, name=name)[0]


def _key_ranges(n_keys, parts=2):
    tiles = n_keys // MXU_COLS
    if tiles < parts:
        return [(0, n_keys)]
    cuts = [MXU_COLS * ((tiles * p + parts - 1) // parts) for p in range(parts + 1)]
    return list(zip(cuts[:-1], cuts[1:]))


def mla_attention(q, kv, kr, B, C, L, heads, with_ctx):
    tq = C
    hp = 4
    n_lat_t = L // tq
    n_t = n_lat_t + (1 if with_ctx else 0)
    ctx_blk = B * L // C
    S = C + L
    Tq = B * L + (B * C if with_ctx else 0)
    kvw = 2 * LANES

    def q_row(b, g, t):
        lat = b * n_lat_t + t
        return jnp.where(t < n_lat_t, lat, ctx_blk + b) if with_ctx else lat

    def kern(q_ref, kvl_ref, kvc_ref, krl_ref, krc_ref, o_ref, k_s, v_s):
        t = pl.program_id(2)

        @pl.when(t == 0)
        def _():
            for hh in range(hp):
                c0 = hh * kvw
                k_s[hh, :C, :LANES] = kvc_ref[:, c0:c0 + LANES]
                k_s[hh, :C, LANES:] = krc_ref[...]
                k_s[hh, C:, :LANES] = kvl_ref[:, c0:c0 + LANES]
                k_s[hh, C:, LANES:] = krl_ref[...]
                v_s[hh, :C, :MLA_V] = kvc_ref[:, c0 + LANES:c0 + kvw]
                v_s[hh, C:, :MLA_V] = kvl_ref[:, c0 + LANES:c0 + kvw]
                v_s[hh, :, MLA_V:] = jnp.ones((S, MXU_COLS - MLA_V), BF16)

        def attend(n_keys):
            ss = [lax.dot_general(q_ref[:, hh * MLA_HEAD_PAD:(hh + 1) * MLA_HEAD_PAD], k_s[hh, :n_keys, :],
                                  (((1,), (1,)), ((), ())), preferred_element_type=F32) for hh in range(hp)]
            ps = [jnp.exp2(s - jnp.max(s, axis=-1, keepdims=True)).astype(BF16) for s in ss]
            for hh in range(hp):
                ov = jnp.dot(ps[hh], v_s[hh, :n_keys, :], preferred_element_type=F32)
                o_ref[:, hh * MLA_V:(hh + 1) * MLA_V] = (ov[:, :MLA_V] / ov[:, MLA_V:2 * MLA_V]).astype(BF16)

        if with_ctx:
            @pl.when(t < n_lat_t)
            def _():
                attend(S)

            @pl.when(t == n_lat_t)
            def _():
                attend(C)
        else:
            attend(S)

    return pl.pallas_call(
        kern,
        grid=(B, heads // hp, n_t),
        in_specs=[pl.BlockSpec((tq, hp * MLA_HEAD_PAD), lambda b, g, t: (q_row(b, g, t), g)),
                  pl.BlockSpec((L, hp * kvw), lambda b, g, t: (b, g)),
                  pl.BlockSpec((C, hp * kvw), lambda b, g, t: (ctx_blk + b, g)),
                  pl.BlockSpec((L, LANES), lambda b, g, t: (b, 0)),
                  pl.BlockSpec((C, LANES), lambda b, g, t: (ctx_blk + b, 0))],
        out_specs=pl.BlockSpec((tq, hp * MLA_V), lambda b, g, t: (q_row(b, g, t), g)),
        out_shape=jax.ShapeDtypeStruct((Tq, heads * MLA_V), BF16),
        scratch_shapes=[pltpu.VMEM((hp, S, MLA_HEAD_PAD), BF16), pltpu.VMEM((hp, S, MXU_COLS), BF16)],
        compiler_params=_params(("parallel", "parallel", "arbitrary")),
        name="mla_attention",
    )(q, kv, kv, kr, kr)


def mla_mixer(h, u, mod, layer, p, geo, need_ctx):
    B, C, L, rows_all, rows_out = geo
    heads = p["w_uq"].shape[1] // (MLA_NOPE + MLA_ROPE)
    cos, sin = rope_tables_padded(L, MLA_ROPE, rows_all.bm)
    cq, ckv, kr = mla_down(u, p["w_dq"], p["q_g"], p["w_dkv"], p["kv_g"], cos, sin, rows_all)
    kv = plain_matmul(ckv, p["w_ukv"].astype(BF16), rows_all, 1024, name="mla_kv")
    q = mla_queries(cq, p["w_uq"], cos, sin, rows_out, heads)
    o = mla_attention(q, kv, kr, B, C, L, heads, need_ctx)
    return gated_matmul(o, p["w_o"].astype(BF16), h, mod, layer, 2, rows_out, name="mla_out")


def _diff_pair_layout():
    quarter = DF_HEAD_DIM // 4
    n = np.arange(2 * DF_HEAD_DIM)
    hf, c, a, r = n // DF_HEAD_DIM, (n % DF_HEAD_DIM) // (2 * quarter), (n % (2 * quarter)) // quarter, n % quarter
    head_perm = c * DF_HEAD_DIM + a * 2 * quarter + hf * quarter + r
    table_cols = (a * 2 * quarter + r)[:DF_HEAD_DIM]
    return head_perm, table_cols


def diff_rope_tables(L, bm):
    cos, sin = axial_rope_tables(L, DF_HEAD_DIM)
    _, cols = _diff_pair_layout()
    cos = jnp.pad(cos[:, cols], ((0, bm), (0, 0)), constant_values=1.0)
    sin = jnp.pad(sin[:, cols], ((0, bm), (0, 0)))
    return cos, sin


def diff_qkv(u, w_qkv, cos, sin, rows):
    T = rows.n * rows.bm
    bm = rows.bm
    D = w_qkv.shape[0]
    bn = 1024
    hw = 2 * DF_HEAD_DIM
    scale = DF_HEAD_DIM ** -0.5 * LOG2E
    n_q = D // bn
    head_perm, _ = _diff_pair_layout()
    qk_cols = (np.arange(2 * D) // hw * hw)[:, None].reshape(-1, hw) + head_perm[None, :]
    cols = np.concatenate([qk_cols.reshape(-1), np.arange(2 * D, 3 * D)])
    w = w_qkv[:, cols].astype(BF16)

    def epi(acc, ex, out):
        cos_ref, sin_ref = ex
        j = pl.program_id(1)

        @pl.when(j < 2 * n_q)
        def _():
            mul = jnp.where(j < n_q, scale, 1.0).astype(F32)
            cs, sn = cos_ref[...] * mul, sin_ref[...] * mul
            for s in range(bn // hw):
                x1 = acc[:, s * hw:s * hw + LANES]
                x2 = acc[:, s * hw + LANES:(s + 1) * hw]
                out[0][:, s * hw:s * hw + LANES] = (x1 * cs - x2 * sn).astype(BF16)
                out[0][:, s * hw + LANES:(s + 1) * hw] = (x2 * cs + x1 * sn).astype(BF16)

        @pl.when(j >= 2 * n_q)
        def _():
            out[0][...] = acc.astype(BF16)

    extras = [(cos, (bm, LANES), lambda i, j, k: (rows.pos_block(i), 0)),
              (sin, (bm, LANES), lambda i, j, k: (rows.pos_block(i), 0))]
    outs = [((T, 3 * D), BF16, (bm, bn), lambda i, j, k: (i, j))]
    return fused_matmul(u, w, extras, outs, epi, rows=rows, bn=bn, name="diff_qkv")[0]


def diff_attention(qkv, lambdas, subln_g, lambda_init, B, C, L, D, with_ctx):
    hd = DF_HEAD_DIM
    hw = 2 * hd
    heads = D // hw
    tq = C
    n_lat_t = L // tq
    n_t = n_lat_t + (1 if with_ctx else 0)
    ctx_blk = B * L // C
    S = C + L
    Tq = B * L + (B * C if with_ctx else 0)
    hp = 2
    kcol = D // (hp * hw)
    vcol = 2 * D // (hp * hw)

    def q_row(b, h, t):
        lat = b * n_lat_t + t
        return jnp.where(t < n_lat_t, lat, ctx_blk + b) if with_ctx else lat

    def kern(q_ref, kl_ref, kc_ref, vl_ref, vc_ref, lam_ref, g_ref, o_ref, k_s, v_s):
        t = pl.program_id(2)

        @pl.when(t == 0)
        def _():
            map0 = (lax.broadcasted_iota(jnp.int32, (1, hw), 1) % hd) < hd // 2
            for ref, lo, hi in ((kc_ref, 0, C), (kl_ref, C, S)):
                for hh in range(hp):
                    k = ref[:, hh * hw:(hh + 1) * hw]
                    k_s[2 * hh, lo:hi, :] = jnp.where(map0, k, jnp.zeros_like(k))
                    k_s[2 * hh + 1, lo:hi, :] = jnp.where(map0, jnp.zeros_like(k), k)
            for hh in range(hp):
                v_s[hh, :C, :] = vc_ref[:, hh * hw:(hh + 1) * hw]
                v_s[hh, C:, :] = vl_ref[:, hh * hw:(hh + 1) * hw]

        lf = lam_ref[...]
        lam = (jnp.exp(jnp.sum(lf[0:1] * lf[1:2], axis=-1, keepdims=True))
               - jnp.exp(jnp.sum(lf[2:3] * lf[3:4], axis=-1, keepdims=True)) + lambda_init)

        def attend(n_keys):
            nc = 2 * hp
            ss = [lax.dot_general(q_ref[:, (c // 2) * hw:(c // 2 + 1) * hw], k_s[c, :n_keys, :],
                                  (((1,), (1,)), ((), ())), preferred_element_type=F32) for c in range(nc)]
            ms = [jnp.max(s, axis=-1, keepdims=True) for s in ss]
            os_, ls = [0.0] * nc, [0.0] * nc
            for lo in range(0, n_keys, MXU_COLS):
                for c in range(nc):
                    p = jnp.exp2(ss[c][:, lo:lo + MXU_COLS] - ms[c])
                    ls[c] = ls[c] + jnp.sum(p[:, :LANES] + p[:, LANES:], axis=-1, keepdims=True)
                    os_[c] = os_[c] + jnp.dot(p.astype(BF16), v_s[c // 2, lo:lo + MXU_COLS, :],
                                              preferred_element_type=F32)
            for hh in range(hp):
                o = os_[2 * hh] / ls[2 * hh] - lam * (os_[2 * hh + 1] / ls[2 * hh + 1])
                o_ref[:, hh * hw:(hh + 1) * hw] = (_rms(o, g_ref[...], DF_SUBLN_EPS)
                                                   * (1.0 - lambda_init)).astype(BF16)

        if with_ctx:
            @pl.when(t < n_lat_t)
            def _():
                attend(S)

            @pl.when(t == n_lat_t)
            def _():
                attend(C)
        else:
            attend(S)

    return pl.pallas_call(
        kern,
        grid=(B, heads // hp, n_t),
        in_specs=[pl.BlockSpec((tq, hp * hw), lambda b, g, t: (q_row(b, g, t), g)),
                  pl.BlockSpec((L, hp * hw), lambda b, g, t: (b, kcol + g)),
                  pl.BlockSpec((C, hp * hw), lambda b, g, t: (ctx_blk + b, kcol + g)),
                  pl.BlockSpec((L, hp * hw), lambda b, g, t: (b, vcol + g)),
                  pl.BlockSpec((C, hp * hw), lambda b, g, t: (ctx_blk + b, vcol + g)),
                  pl.BlockSpec((4, hd), lambda b, g, t: (0, 0)),
                  pl.BlockSpec((1, hw), lambda b, g, t: (0, 0))],
        out_specs=pl.BlockSpec((tq, hp * hw), lambda b, g, t: (q_row(b, g, t), g)),
        out_shape=jax.ShapeDtypeStruct((Tq, D), BF16),
        scratch_shapes=[pltpu.VMEM((2 * hp, S, hw), BF16), pltpu.VMEM((hp, S, hw), BF16)],
        compiler_params=_params(("parallel", "parallel", "arbitrary")),
        name="diff_attention",
    )(qkv, qkv, qkv, qkv, qkv, lambdas, subln_g.reshape(1, hw))


def diff_mixer(h, u, mod, layer, p, geo, need_ctx, lambda_init):
    B, C, L, rows_all, rows_out = geo
    D = h.shape[1]
    cos, sin = diff_rope_tables(L, rows_all.bm)
    qkv = diff_qkv(u, p["w_qkv"], cos, sin, rows_all)
    o = diff_attention(qkv, p["lambdas"], p["subln_g"], lambda_init, B, C, L, D, need_ctx)
    return gated_matmul(o, p["w_o"].astype(BF16), h, mod, layer, 2, rows_out, name="diff_out")


def dft_matrices(Ls):
    n = 2 * Ls
    t0n = min(64, Ls)
    t1n = Ls // t0n
    f = jnp.arange(Ls, dtype=jnp.int32)[:, None]
    a1 = ((f * (jnp.arange(t1n, dtype=jnp.int32) * t0n)[None, :]) % n).astype(F32) * (2.0 * math.pi / n)
    a0 = ((f * jnp.arange(t0n, dtype=jnp.int32)[None, :]) % n).astype(F32) * (2.0 * math.pi / n)
    c1, s1 = jnp.cos(a1)[:, :, None], jnp.sin(a1)[:, :, None]
    c0, s0 = jnp.cos(a0)[:, None, :], jnp.sin(a0)[:, None, :]
    cosm = (c1 * c0 - s1 * s0).reshape(Ls, Ls)
    sinm = (s1 * c0 + c1 * s0).reshape(Ls, Ls)
    nyq = jnp.where(jnp.arange(Ls) % 2 == 0, 1.0, -1.0).astype(F32)[None, :]
    imag = jnp.where(f == 0, nyq, -sinm)
    fwd = jnp.concatenate([cosm, imag], axis=0)
    col = jnp.arange(n)
    cscale = jnp.where((col == 0) | (col == Ls), 1.0 / n, 2.0 / n).astype(F32)
    inv = fwd.T * cscale[None, :]
    return fwd.astype(BF16), inv.astype(BF16)


def hyena_filter_time(Ls, p):
    D = p["f_bias"].shape[0]
    order = p["f_w2"].shape[0]
    bands = (HY_EMB_DIM - 1) // 2
    t = jnp.linspace(0.0, 1.0, Ls, dtype=F32)[:, None]
    w = 2.0 * math.pi * jnp.arange(Ls, dtype=F32)[:, None] / Ls
    f = jnp.linspace(1e-4, bands - 1, bands, dtype=F32)
    feats = jnp.concatenate([t, jnp.cos(f * w), -jnp.sin(f * w)], axis=-1)
    feats = jnp.pad(feats, ((0, 0), (0, HY_EMB_PAD - HY_EMB_DIM)))
    w1 = jnp.pad(p["f_w1"], ((0, HY_EMB_PAD - HY_EMB_DIM), (0, 0)))
    deltas = jnp.abs(jnp.linspace(math.log(HY_TARGET) / HY_SLOW_PCT, math.log(HY_TARGET) / HY_FAST_PCT,
                                  D, dtype=F32))
    deltas2 = jnp.concatenate([deltas, deltas]).reshape(1, 2 * D)

    def ffn_kern(x_ref, w1_ref, b1_ref, w2_ref, b2_ref, w3_ref, b3_ref, fr_ref, o_ref):
        dot = functools.partial(jnp.dot, precision=HIGHEST, preferred_element_type=F32)
        fr = fr_ref[...]
        hcur = jnp.sin(fr[0:1] * (dot(x_ref[...], w1_ref[...]) + b1_ref[...]))
        hcur = jnp.sin(fr[1:2] * (dot(hcur, w2_ref[...]) + b2_ref[...]))
        o_ref[...] = jnp.sin(fr[2:3] * (dot(hcur, w3_ref[...]) + b3_ref[...]))

    full = lambda a: pl.BlockSpec(a.shape, lambda: (0,) * a.ndim)
    ffn_in = [feats, w1, p["f_b1"].reshape(1, order), p["f_w2"], p["f_b2"].reshape(1, order),
              p["f_w3"], p["f_b3"].reshape(1, order), p["f_freq"]]
    hff = pl.pallas_call(
        ffn_kern,
        in_specs=[full(a) for a in ffn_in],
        out_specs=pl.BlockSpec((Ls, order), lambda: (0, 0)),
        out_shape=jax.ShapeDtypeStruct((Ls, order), F32),
        name="hyena_filter_ffn",
    )(*ffn_in)

    bn = 512

    def out_kern(h_ref, w_ref, d_ref, o_ref):
        tt = lax.broadcasted_iota(jnp.int32, (Ls, 1), 0).astype(F32) * (1.0 / (Ls - 1))
        hw = jnp.dot(h_ref[...], w_ref[...], precision=HIGHEST, preferred_element_type=F32)
        o_ref[...] = hw * jnp.exp(-tt * d_ref[...])

    return pl.pallas_call(
        out_kern,
        grid=(2 * D // bn,),
        in_specs=[pl.BlockSpec((Ls, order), lambda j: (0, 0)),
                  pl.BlockSpec((order, bn), lambda j: (0, j)),
                  pl.BlockSpec((1, bn), lambda j: (0, j))],
        out_specs=pl.BlockSpec((Ls, bn), lambda j: (0, j)),
        out_shape=jax.ShapeDtypeStruct((Ls, 2 * D), F32),
        compiler_params=_params(("parallel",)),
        name="hyena_filter_out",
    )(hff, p["f_wout"], deltas2)


def hyena_filter_spectrum(hfb, fwd, Ls, D):
    n = 2 * Ls
    cb = LANES
    ncb = D // cb

    def kern(w_ref, hf_ref, hb_ref, o_ref):
        row = lax.broadcasted_iota(jnp.int32, (Ls, 1), 0)
        hb = jnp.where(row == 0, 0.0, hb_ref[...])
        a = jnp.dot(w_ref[...], hf_ref[...].astype(BF16), preferred_element_type=F32)
        b = jnp.dot(w_ref[...], hb.astype(BF16), preferred_element_type=F32)
        frow = lax.broadcasted_iota(jnp.int32, (n, 1), 0)
        o_ref[...] = a + jnp.where(frow > Ls, -b, b)

    return pl.pallas_call(
        kern,
        grid=(ncb,),
        in_specs=[pl.BlockSpec((n, Ls), lambda j: (0, 0), pipeline_mode=pl.Buffered(1)),
                  pl.BlockSpec((Ls, cb), lambda j: (0, j)),
                  pl.BlockSpec((Ls, cb), lambda j: (0, ncb + j))],
        out_specs=pl.BlockSpec((n, cb), lambda j: (0, j)),
        out_shape=jax.ShapeDtypeStruct((n, D), F32),
        compiler_params=_params(("parallel",)),
        name="hyena_filter_spectrum",
    )(fwd, hfb, hfb)


def _short_conv(x, w, b):
    n = x.shape[0]
    row = lax.broadcasted_iota(jnp.int32, (n, 1), 0)
    prev = jnp.where(row == 0, 0.0, pltpu.roll(x, 1, 0))
    nxt = jnp.where(row == n - 1, 0.0, pltpu.roll(x, n - 1, 0))
    return prev * w[0:1] + x * w[1:2] + nxt * w[2:3] + b


def hyena_segment(z, p, Ls, row_blk0, B, D):
    n = 2 * Ls
    cb = MXU_COLS
    ncb = D // cb
    fwd, inv = dft_matrices(Ls)
    hfb = hyena_filter_time(Ls, p)
    kf = hyena_filter_spectrum(hfb, fwd, Ls, D)
    conv_w, conv_b = p["conv_w"], p["conv_b"].reshape(1, 3 * D)

    def gate_kern(x1_ref, v_ref, cw1_ref, cb1_ref, cwv_ref, cbv_ref, g_ref):
        g_ref[...] = (_short_conv(v_ref[...].astype(F32), cwv_ref[...], cbv_ref[...])
                      * _short_conv(x1_ref[...].astype(F32), cw1_ref[...], cb1_ref[...])).astype(BF16)

    g = pl.pallas_call(
        gate_kern,
        grid=(ncb, B),
        in_specs=[pl.BlockSpec((Ls, cb), lambda j, b: (row_blk0 + b, ncb + j)),
                  pl.BlockSpec((Ls, cb), lambda j, b: (row_blk0 + b, 2 * ncb + j)),
                  pl.BlockSpec((HY_SHORT, cb), lambda j, b: (0, ncb + j)),
                  pl.BlockSpec((1, cb), lambda j, b: (0, ncb + j)),
                  pl.BlockSpec((HY_SHORT, cb), lambda j, b: (0, 2 * ncb + j)),
                  pl.BlockSpec((1, cb), lambda j, b: (0, 2 * ncb + j))],
        out_specs=pl.BlockSpec((Ls, cb), lambda j, b: (b, j)),
        out_shape=jax.ShapeDtypeStruct((B * Ls, D), BF16),
        compiler_params=_params(("parallel", "arbitrary")),
        name="hyena_gate",
    )(z, z, conv_w, conv_b, conv_w, conv_b)

    def fwd_kern(w_ref, g_ref, kf_ref, y_ref):
        u = jnp.dot(w_ref[...], g_ref[...], preferred_element_type=F32)
        ure, uim = u[:Ls], u[Ls:]
        kre, kim = kf_ref[:Ls, :], kf_ref[Ls:, :]
        first = lax.broadcasted_iota(jnp.int32, (Ls, 1), 0) == 0
        y_ref[:Ls, :] = (ure * kre - jnp.where(first, 0.0, uim * kim)).astype(BF16)
        y_ref[Ls:, :] = jnp.where(first, uim * kim, ure * kim + uim * kre).astype(BF16)

    y = pl.pallas_call(
        fwd_kern,
        grid=(ncb, B),
        in_specs=[pl.BlockSpec((n, Ls), lambda j, b: (0, 0), pipeline_mode=pl.Buffered(1)),
                  pl.BlockSpec((Ls, cb), lambda j, b: (b, j)),
                  pl.BlockSpec((n, cb), lambda j, b: (0, j))],
        out_specs=pl.BlockSpec((None, n, cb), lambda j, b: (b, 0, j)),
        out_shape=jax.ShapeDtypeStruct((B, n, D), BF16),
        compiler_params=_params(("parallel", "arbitrary")),
        name="hyena_dft",
    )(fwd, g, kf)

    def inv_kern(w_ref, y_ref, g_ref, x0_ref, cw0_ref, cb0_ref, fb_ref, o_ref):
        conv = jnp.dot(w_ref[...], y_ref[...], preferred_element_type=F32)
        x0 = _short_conv(x0_ref[...].astype(F32), cw0_ref[...], cb0_ref[...])
        o_ref[...] = ((conv + g_ref[...].astype(F32) * fb_ref[...]) * x0).astype(BF16)

    return pl.pallas_call(
        inv_kern,
        grid=(ncb, B),
        in_specs=[pl.BlockSpec((Ls, n), lambda j, b: (0, 0), pipeline_mode=pl.Buffered(1)),
                  pl.BlockSpec((None, n, cb), lambda j, b: (b, 0, j)),
                  pl.BlockSpec((Ls, cb), lambda j, b: (b, j)),
                  pl.BlockSpec((Ls, cb), lambda j, b: (row_blk0 + b, j)),
                  pl.BlockSpec((HY_SHORT, cb), lambda j, b: (0, j)),
                  pl.BlockSpec((1, cb), lambda j, b: (0, j)),
                  pl.BlockSpec((1, cb), lambda j, b: (0, j))],
        out_specs=pl.BlockSpec((Ls, cb), lambda j, b: (b, j)),
        out_shape=jax.ShapeDtypeStruct((B * Ls, D), BF16),
        compiler_params=_params(("parallel", "arbitrary")),
        name="hyena_idft",
    )(inv, y, g, z, conv_w, conv_b, p["f_bias"].reshape(1, D))


def hyena_mixer(h, u, mod, layer, p, geo, need_ctx):
    B, C, L, rows_all, rows_out = geo
    D = h.shape[1]
    z = plain_matmul(u, p["w_in"].astype(BF16), rows_out, 1024, bias=p["b_in"], name="hyena_in")
    y = hyena_segment(z, p, L, 0, B, D)
    if need_ctx:
        y = jnp.concatenate([y, hyena_segment(z, p, C, B * L // C, B, D)], axis=0)
    return gated_matmul(y, p["w_out"].astype(BF16), h, mod, layer, 2, rows_out, bias=p["b_out"], name="hyena_out")


def kernel(x, c, ctx, c_ctx, ada_w, ada_b, norm_g, mlp_w1, mlp_w2, final_norm_g, mla_w_dq, mla_q_norm_g, mla_w_uq, mla_w_dkv, mla_kv_norm_g, mla_w_ukv, mla_w_o, hy_w_in, hy_b_in, hy_conv_w, hy_conv_b, hy_filt_w1, hy_filt_b1, hy_filt_w2, hy_filt_b2, hy_filt_w3, hy_filt_b3, hy_filt_freq, hy_filt_wout, hy_filt_bias, hy_w_out, hy_b_out, df_w_qkv, df_lambda, df_subln_g, df_w_o):
    B, L, D = x.shape
    C = ctx.shape[1]
    depth = ada_w.shape[0]
    bm = min(1024, B * C)
    rows_all = Rows(B, C, L, bm)
    rows_lat = Rows(B, C, L, bm, lat_only=True)

    mod = adaln_table(jnp.concatenate([c_ctx[None, :], c], axis=0), ada_w, ada_b)
    h = jnp.concatenate([x.reshape(B * L, D), ctx.reshape(B * C, D)], axis=0)

    for i in range(depth):
        need_ctx = i < depth - 1
        rows_out = rows_all if need_ctx else rows_lat
        geo = (B, C, L, rows_all, rows_out)
        kind, j = i % N_MIXERS, i // N_MIXERS
        u = norm_modulate(h, norm_g, mod, i, 0, rows_all)
        if kind == 0:
            p = dict(w_dq=mla_w_dq[j], q_g=mla_q_norm_g[j], w_uq=mla_w_uq[j], w_dkv=mla_w_dkv[j],
                     kv_g=mla_kv_norm_g[j], w_ukv=mla_w_ukv[j], w_o=mla_w_o[j])
            h = mla_mixer(h, u, mod, i, p, geo, need_ctx)
        elif kind == 1:
            p = dict(w_in=hy_w_in[j], b_in=hy_b_in[j], conv_w=hy_conv_w[j], conv_b=hy_conv_b[j],
                     f_w1=hy_filt_w1[j], f_b1=hy_filt_b1[j], f_w2=hy_filt_w2[j], f_b2=hy_filt_b2[j],
                     f_w3=hy_filt_w3[j], f_b3=hy_filt_b3[j], f_freq=hy_filt_freq[j], f_wout=hy_filt_wout[j],
                     f_bias=hy_filt_bias[j], w_out=hy_w_out[j], b_out=hy_b_out[j])
            h = hyena_mixer(h, u, mod, i, p, geo, need_ctx)
        else:
            lambda_init = 0.8 - 0.6 * math.exp(-0.3 * i)
            p = dict(w_qkv=df_w_qkv[j], lambdas=df_lambda[j], subln_g=df_subln_g[j], w_o=df_w_o[j])
            h = diff_mixer(h, u, mod, i, p, geo, need_ctx, lambda_init)
        h = mlp(h, norm_g, mod, i, mlp_w1[i].astype(BF16), mlp_w2[i].astype(BF16), rows_out)

    return final_norm(h, final_norm_g, rows_lat).reshape(B, L, D)
```

```python
import functools
import math

import jax
import jax.numpy as jnp
import numpy as np
from jax import lax
from jax.experimental import pallas as pl
from jax.experimental.pallas import tpu as pltpu

F32 = jnp.float32
BF16 = jnp.bfloat16
HIGHEST = lax.Precision.HIGHEST
LOG2E = math.log2(math.e)

GRID_W = 64
ROPE_THETA = 10000.0
NORM_EPS = 1e-6
N_MIXERS = 3

MLA_NOPE = 128
MLA_ROPE = 64
MLA_V = 128
MLA_KV_RANK = 512
MLA_HEAD_PAD = 256

HY_SHORT = 3
HY_EMB_DIM = 33
HY_EMB_PAD = 64
HY_TARGET = 1e-2
HY_FAST_PCT = 0.3
HY_SLOW_PCT = 1.5

DF_HEAD_DIM = 128
DF_SUBLN_EPS = 1e-5

LANES = 128
MXU_COLS = 256
VMEM_LIMIT_MB = 56


def _params(semantics, vmem_mb=VMEM_LIMIT_MB):
    return pltpu.CompilerParams(dimension_semantics=semantics, vmem_limit_bytes=vmem_mb << 20)


class Rows:
    def __init__(self, B, C, L, bm, lat_only=False):
        assert L % bm == 0 and (B * C) % bm == 0
        self.geometry = (B, C, L, lat_only)
        self.bm = bm
        self.tiles_per_batch = L // bm
        self.n_lat = B * L // bm
        self.n = self.n_lat + (0 if lat_only else B * C // bm)

    def with_bm(self, bm):
        B, C, L, lat_only = self.geometry
        return Rows(B, C, L, bm, lat_only)

    def mod_row(self, i):
        return jnp.where(i < self.n_lat, 1 + i // self.tiles_per_batch, 0)

    def pos_block(self, i):
        return jnp.where(i < self.n_lat, i % self.tiles_per_batch, self.tiles_per_batch)


def _mod_spec(layer, chunk, rows, bn, col_of):
    return pl.BlockSpec((None, None, None, 1, bn),
                        lambda *g: (layer, chunk, rows.mod_row(g[0]), 0, col_of(*g)))


def _row_operand(arr, rows, width, col_of):
    bm = rows.bm
    n_lat = rows.n_lat
    if isinstance(arr, tuple) and rows.n == n_lat:
        arr = arr[0]
    if not isinstance(arr, tuple):
        return [(arr, (bm, width), lambda *g: (g[0], col_of(*g)))], lambda refs: refs[0][...]
    specs = [(arr[0], (bm, width), lambda *g: (jnp.minimum(g[0], n_lat - 1), col_of(*g))),
             (arr[1], (bm, width), lambda *g: (jnp.maximum(g[0] - n_lat, 0), col_of(*g)))]
    return specs, lambda refs: jnp.where(pl.program_id(0) < n_lat, refs[0][...], refs[1][...])


def fused_matmul(a, w, extras, outs, epilogue, *, rows, bn, bk=None, name):
    K, N = w.shape
    bk = bk or K
    nk = K // bk
    assert K % bk == 0 and N % bn == 0
    a_specs, read_a = _row_operand(a, rows, bk, lambda i, j, k: k)
    n_a = len(a_specs)
    in_specs = [pl.BlockSpec(bs, im) for _, bs, im in a_specs]
    in_specs += [pl.BlockSpec((bk, bn), lambda i, j, k: (k, j))]
    in_specs += [pl.BlockSpec(bs, im) for _, bs, im in extras]
    n_ex = len(extras)
    n_out = len(outs)

    def kern(*refs):
        w_ref = refs[n_a]
        ex = refs[n_a + 1:n_a + 1 + n_ex]
        out = refs[n_a + 1 + n_ex:n_a + 1 + n_ex + n_out]

        def dot():
            return jnp.dot(read_a(refs[:n_a]), w_ref[...], preferred_element_type=F32)

        if nk == 1:
            epilogue(dot(), ex, out)
        else:
            acc_ref = out[0]
            k = pl.program_id(2)

            @pl.when(k == 0)
            def _():
                acc_ref[...] = dot()

            @pl.when((k > 0) & (k < nk - 1))
            def _():
                acc_ref[...] += dot()

            @pl.when(k == nk - 1)
            def _():
                epilogue(acc_ref[...] + dot(), ex, out)

    res = pl.pallas_call(
        kern,
        grid=(rows.n, N // bn, nk),
        in_specs=in_specs,
        out_specs=[pl.BlockSpec(bs, im) for _, _, bs, im in outs],
        out_shape=[jax.ShapeDtypeStruct(s, d) for s, d, _, _ in outs],
        compiler_params=_params(("parallel", "arbitrary", "arbitrary")),
        name=name,
    )(*[e[0] for e in a_specs], w, *[e[0] for e in extras])
    return res


def _rot_half(x, quarter):
    n = x.shape[-1]
    lane = lax.broadcasted_iota(jnp.int32, x.shape, x.ndim - 1)
    first = (lane % (2 * quarter)) < quarter
    return jnp.where(first, -pltpu.roll(x, n - quarter, x.ndim - 1), pltpu.roll(x, quarter, x.ndim - 1))


def _rms(x, g, eps):
    return x * lax.rsqrt(jnp.mean(x * x, axis=-1, keepdims=True) + eps) * g


def gated_matmul(a, w, resid, mod, layer, chunk, rows, *, name, bn=1024, bk=None, bias=None, norm=None):
    T = rows.n * rows.bm
    bm = rows.bm
    N = w.shape[1]
    res_specs, read_res = _row_operand(resid, rows, bn, lambda i, j, k: j)
    n_res = len(res_specs)
    extras = res_specs + [(mod,) + _spec_parts(_mod_spec(layer, chunk, rows, bn, lambda i, j, k: j))]
    if bias is not None:
        extras.append((bias.reshape(1, N), (1, bn), lambda i, j, k: (0, j)))
    n_fix = len(extras)
    tile = lambda i, j, k: (i, j)
    outs = [((T, N), F32, (bm, bn), tile)]
    if norm is not None:
        assert bn == N
        if norm[0] == "modulate":
            _, g4, nl, which = norm
            extras += [(g4, (None, None, 1, N), lambda i, j, k: (nl, which, 0, 0)),
                       (mod,) + _spec_parts(_mod_spec(nl, 3 * which, rows, N, lambda i, j, k: 0)),
                       (mod,) + _spec_parts(_mod_spec(nl, 3 * which + 1, rows, N, lambda i, j, k: 0))]
            outs.append(((T, N), BF16, (bm, bn), tile))
        else:
            extras.append((norm[1].reshape(1, N), (1, N), lambda i, j, k: (0, 0)))

    def epi(acc, ex, out):
        y = acc if bias is None else acc + ex[n_res + 1][...]
        hn = read_res(ex[:n_res]) + ex[n_res][...] * y
        if norm is None:
            out[0][...] = hn
        elif norm[0] == "modulate":
            g_ref, sh_ref, sc_ref = ex[n_fix:]
            out[0][...] = hn
            out[1][...] = (_rms(hn, g_ref[...], NORM_EPS) * (1.0 + sc_ref[...]) + sh_ref[...]).astype(BF16)
        else:
            out[0][...] = _rms(hn, ex[n_fix][...], NORM_EPS)

    res = fused_matmul(a, w, extras, outs, epi, rows=rows, bn=bn, bk=bk, name=name)
    return res if len(res) > 1 else res[0]


def _spec_parts(spec):
    return spec.block_shape, spec.index_map


def h_width(h):
    return (h[0] if isinstance(h, tuple) else h).shape[1]


def _out_rows(rows):
    return rows.with_bm(min(512, rows.bm))


def adaln_table(cvec, ada_w, ada_b):
    depth, D, _ = ada_w.shape
    R = cvec.shape[0]

    def kern(c_ref, w_ref, b_ref, o_ref):
        c = c_ref[...]
        s = c / (1.0 + jnp.exp(-c))
        o_ref[...] = jnp.dot(s, w_ref[...], precision=HIGHEST, preferred_element_type=F32) + b_ref[...]

    out = pl.pallas_call(
        kern,
        grid=(depth, 6),
        in_specs=[pl.BlockSpec((R, D), lambda l, j: (0, 0)),
                  pl.BlockSpec((None, D, D), lambda l, j: (l, 0, j)),
                  pl.BlockSpec((None, 1, D), lambda l, j: (l, 0, j))],
        out_specs=pl.BlockSpec((None, None, R, D), lambda l, j: (l, j, 0, 0)),
        out_shape=jax.ShapeDtypeStruct((depth, 6, R, D), F32),
        compiler_params=_params(("parallel", "arbitrary")),
        name="adaln_table",
    )(cvec, ada_w, ada_b.reshape(depth, 1, 6 * D))
    return out.reshape(depth, 6, R, 1, D)


def norm_modulate(h, norm_g4, mod, layer, which, rows):
    D = norm_g4.shape[-1]
    bm = rows.bm
    h_specs, read_h = _row_operand(h, rows, D, lambda i: 0)
    n_h = len(h_specs)

    def kern(*refs):
        g_ref, sh_ref, sc_ref, o_ref = refs[n_h:]
        y = _rms(read_h(refs[:n_h]), g_ref[...], NORM_EPS)
        o_ref[...] = (y * (1.0 + sc_ref[...]) + sh_ref[...]).astype(BF16)

    return pl.pallas_call(
        kern,
        grid=(rows.n,),
        in_specs=[pl.BlockSpec(bs, im) for _, bs, im in h_specs]
        + [pl.BlockSpec((None, None, 1, D), lambda i: (layer, which, 0, 0)),
           _mod_spec(layer, 3 * which, rows, D, lambda i: 0),
           _mod_spec(layer, 3 * which + 1, rows, D, lambda i: 0)],
        out_specs=pl.BlockSpec((bm, D), lambda i: (i, 0)),
        out_shape=jax.ShapeDtypeStruct((rows.n * bm, D), BF16),
        compiler_params=_params(("parallel",)),
        name="norm_modulate",
    )(*[e[0] for e in h_specs], norm_g4, mod, mod)


def mlp(h, u, mod, layer, w1, w2, rows, norm):
    T = rows.n * rows.bm
    F = w1.shape[1]

    def relu2(acc, ex, out):
        r = jnp.maximum(acc, 0.0)
        out[0][...] = (r * r).astype(BF16)

    hid = fused_matmul(u, w1, [], [((T, F), BF16, (rows.bm, 1024), lambda i, j, k: (i, j))], relu2,
                       rows=rows, bn=1024, name="mlp_up")[0]
    return gated_matmul(hid, w2, h, mod, layer, 5, rows, bn=w2.shape[1], bk=1024, norm=norm, name="mlp_down")


def axial_rope_tables(L, rot_dim):
    rows = L // GRID_W
    row = jnp.repeat(jnp.arange(rows, dtype=F32), GRID_W)
    col = jnp.tile(jnp.arange(GRID_W, dtype=F32), rows)
    pos = jnp.stack([row, col], axis=-1)
    n_freq = rot_dim // 4
    inv_freq = ROPE_THETA ** (-jnp.arange(n_freq, dtype=F32) / n_freq)
    ang = pos[:, :, None, None] * inv_freq
    ang = jnp.broadcast_to(ang, (L, 2, 2, n_freq)).reshape(L, rot_dim)
    return jnp.cos(ang), jnp.sin(ang)


def rope_tables_padded(L, rot_dim, bm):
    cos, sin = axial_rope_tables(L, rot_dim)
    cos = jnp.pad(cos, ((0, bm), (0, LANES - rot_dim)), constant_values=1.0)
    sin = jnp.pad(sin, ((0, bm), (0, LANES - rot_dim)))
    return cos, sin


def mla_down(u, w_dq, q_g, w_dkv, kv_g, cos, sin, rows):
    T = rows.n * rows.bm
    bm = rows.bm
    qr = w_dq.shape[1]
    wd = jnp.concatenate([w_dq, w_dkv, jnp.zeros((w_dq.shape[0], LANES - MLA_ROPE), w_dq.dtype)],
                         axis=1).astype(BF16)
    n_all = wd.shape[1]
    c0 = qr + MLA_KV_RANK

    def epi(acc, ex, out):
        qg_ref, kvg_ref, cos_ref, sin_ref = ex
        out[0][...] = _rms(acc[:, :qr], qg_ref[...], NORM_EPS).astype(BF16)
        out[1][...] = _rms(acc[:, qr:c0], kvg_ref[...], NORM_EPS).astype(BF16)
        kr = acc[:, c0:]
        out[2][...] = (kr * cos_ref[...] + _rot_half(kr, MLA_ROPE // 4) * sin_ref[...]).astype(BF16)

    extras = [(q_g.reshape(1, qr), (1, qr), lambda i, j, k: (0, 0)),
              (kv_g.reshape(1, MLA_KV_RANK), (1, MLA_KV_RANK), lambda i, j, k: (0, 0)),
              (cos, (bm, LANES), lambda i, j, k: (rows.pos_block(i), 0)),
              (sin, (bm, LANES), lambda i, j, k: (rows.pos_block(i), 0))]
    outs = [((T, qr), BF16, (bm, qr), lambda i, j, k: (i, 0)),
            ((T, MLA_KV_RANK), BF16, (bm, MLA_KV_RANK), lambda i, j, k: (i, 0)),
            ((T, LANES), BF16, (bm, LANES), lambda i, j, k: (i, 0))]
    return fused_matmul(u, wd, extras, outs, epi, rows=rows, bn=n_all, name="mla_down")


def mla_queries(cq, w_uq, cos, sin, rows, heads):
    T = rows.n * rows.bm
    bm = rows.bm
    qr = w_uq.shape[0]
    hd = MLA_NOPE + MLA_ROPE
    w = w_uq.reshape(qr, heads, hd)
    w = jnp.pad(w, ((0, 0), (0, 0), (0, MLA_HEAD_PAD - hd))).reshape(qr, heads * MLA_HEAD_PAD).astype(BF16)
    scale = hd ** -0.5 * LOG2E
    bn = 2048

    def epi(acc, ex, out):
        cos_ref, sin_ref = ex
        for hh in range(bn // MLA_HEAD_PAD):
            c = hh * MLA_HEAD_PAD
            out[0][:, c:c + LANES] = (acc[:, c:c + LANES] * scale).astype(BF16)
            r = acc[:, c + LANES:c + 2 * LANES]
            r = r * cos_ref[...] + _rot_half(r, MLA_ROPE // 4) * sin_ref[...]
            out[0][:, c + LANES:c + 2 * LANES] = (r * scale).astype(BF16)

    extras = [(cos, (bm, LANES), lambda i, j, k: (rows.pos_block(i), 0)),
              (sin, (bm, LANES), lambda i, j, k: (rows.pos_block(i), 0))]
    outs = [((T, heads * MLA_HEAD_PAD), BF16, (bm, bn), lambda i, j, k: (i, j))]
    return fused_matmul(cq, w, extras, outs, epi, rows=rows, bn=bn, name="mla_queries")[0]


def plain_matmul(a, w, rows, bn, *, name, out_dtype=BF16, bias=None):
    T = rows.n * rows.bm
    N = w.shape[1]
    extras = [] if bias is None else [(bias.reshape(1, N), (1, bn), lambda i, j, k: (0, j))]

    def epi(acc, ex, out):
        y = acc if bias is None else acc + ex[0][...]
        out[0][...] = y.astype(out_dtype)

    return fused_matmul(a, w, extras, [((T, N), out_dtype, (rows.bm, bn), lambda i, j, k: (i, j))], epi,
                        rows=rows, bn=bn, name=name)[0]


def mla_attention(q, kv, kr, B, C, L, heads, with_ctx):
    tq = C
    hp = 4
    n_lat_t = L // tq
    n_t = n_lat_t + (1 if with_ctx else 0)
    ctx_blk = B * L // C
    S = C + L
    Tq = B * L + (B * C if with_ctx else 0)
    kvw = 2 * LANES

    def q_row(b, g, t):
        lat = b * n_lat_t + t
        return jnp.where(t < n_lat_t, lat, ctx_blk + b) if with_ctx else lat

    def kern(q_ref, kvl_ref, kvc_ref, krl_ref, krc_ref, o_ref, k_s, v_s):
        t = pl.program_id(2)

        @pl.when(t == 0)
        def _():
            for hh in range(hp):
                c0 = hh * kvw
                k_s[hh, :C, :LANES] = kvc_ref[:, c0:c0 + LANES]
                k_s[hh, :C, LANES:] = krc_ref[...]
                k_s[hh, C:, :LANES] = kvl_ref[:, c0:c0 + LANES]
                k_s[hh, C:, LANES:] = krl_ref[...]
                v_s[hh, :C, :MLA_V] = kvc_ref[:, c0 + LANES:c0 + kvw]
                v_s[hh, C:, :MLA_V] = kvl_ref[:, c0 + LANES:c0 + kvw]
                v_s[hh, :, MLA_V:] = jnp.ones((S, MXU_COLS - MLA_V), BF16)

        def attend(n_keys):
            ss = [lax.dot_general(q_ref[:, hh * MLA_HEAD_PAD:(hh + 1) * MLA_HEAD_PAD], k_s[hh, :n_keys, :],
                                  (((1,), (1,)), ((), ())), preferred_element_type=F32) for hh in range(hp)]
            ps = [jnp.exp2(s - jnp.max(s, axis=-1, keepdims=True)).astype(BF16) for s in ss]
            for hh in range(hp):
                ov = jnp.dot(ps[hh], v_s[hh, :n_keys, :], preferred_element_type=F32)
                o_ref[:, hh * MLA_V:(hh + 1) * MLA_V] = (ov[:, :MLA_V] / ov[:, MLA_V:2 * MLA_V]).astype(BF16)

        if with_ctx:
            @pl.when(t < n_lat_t)
            def _():
                attend(S)

            @pl.when(t == n_lat_t)
            def _():
                attend(C)
        else:
            attend(S)

    return pl.pallas_call(
        kern,
        grid=(B, heads // hp, n_t),
        in_specs=[pl.BlockSpec((tq, hp * MLA_HEAD_PAD), lambda b, g, t: (q_row(b, g, t), g)),
                  pl.BlockSpec((L, hp * kvw), lambda b, g, t: (b, g)),
                  pl.BlockSpec((C, hp * kvw), lambda b, g, t: (ctx_blk + b, g)),
                  pl.BlockSpec((L, LANES), lambda b, g, t: (b, 0)),
                  pl.BlockSpec((C, LANES), lambda b, g, t: (ctx_blk + b, 0))],
        out_specs=pl.BlockSpec((tq, hp * MLA_V), lambda b, g, t: (q_row(b, g, t), g)),
        out_shape=jax.ShapeDtypeStruct((Tq, heads * MLA_V), BF16),
        scratch_shapes=[pltpu.VMEM((hp, S, MLA_HEAD_PAD), BF16), pltpu.VMEM((hp, S, MXU_COLS), BF16)],
        compiler_params=_params(("parallel", "parallel", "arbitrary")),
        name="mla_attention",
    )(q, kv, kv, kr, kr)


def mla_mixer(h, u, mod, layer, p, geo, need_ctx, norm):
    B, C, L, rows_all, rows_out = geo
    heads = p["w_uq"].shape[1] // (MLA_NOPE + MLA_ROPE)
    cos, sin = rope_tables_padded(L, MLA_ROPE, rows_all.bm)
    cq, ckv, kr = mla_down(u, p["w_dq"], p["q_g"], p["w_dkv"], p["kv_g"], cos, sin, rows_all)
    kv = plain_matmul(ckv, p["w_ukv"].astype(BF16), rows_all, p["w_ukv"].shape[1], name="mla_kv")
    q = mla_queries(cq, p["w_uq"], cos, sin, rows_out, heads)
    o = mla_attention(q, kv, kr, B, C, L, heads, need_ctx)
    return gated_matmul(o, p["w_o"].astype(BF16), h, mod, layer, 2, _out_rows(rows_out), bn=h_width(h), norm=norm,
                        name="mla_out")


def _diff_pair_layout():
    quarter = DF_HEAD_DIM // 4
    n = np.arange(2 * DF_HEAD_DIM)
    hf, c, a, r = n // DF_HEAD_DIM, (n % DF_HEAD_DIM) // (2 * quarter), (n % (2 * quarter)) // quarter, n % quarter
    head_perm = c * DF_HEAD_DIM + a * 2 * quarter + hf * quarter + r
    table_cols = (a * 2 * quarter + r)[:DF_HEAD_DIM]
    return head_perm, table_cols


def diff_rope_tables(L, bm):
    cos, sin = axial_rope_tables(L, DF_HEAD_DIM)
    _, cols = _diff_pair_layout()
    cos = jnp.pad(cos[:, cols], ((0, bm), (0, 0)), constant_values=1.0)
    sin = jnp.pad(sin[:, cols], ((0, bm), (0, 0)))
    return cos, sin


def diff_qkv(u, w_qkv, cos, sin, rows):
    T = rows.n * rows.bm
    bm = rows.bm
    D = w_qkv.shape[0]
    bn = 1024
    hw = 2 * DF_HEAD_DIM
    scale = DF_HEAD_DIM ** -0.5 * LOG2E
    n_q = D // bn
    head_perm, _ = _diff_pair_layout()
    qk_cols = (np.arange(2 * D) // hw * hw)[:, None].reshape(-1, hw) + head_perm[None, :]
    cols = np.concatenate([qk_cols.reshape(-1), np.arange(2 * D, 3 * D)])
    w = w_qkv[:, cols].astype(BF16)

    def epi(acc, ex, out):
        cos_ref, sin_ref = ex
        j = pl.program_id(1)

        @pl.when(j < 2 * n_q)
        def _():
            mul = jnp.where(j < n_q, scale, 1.0).astype(F32)
            cs, sn = cos_ref[...] * mul, sin_ref[...] * mul
            for s in range(bn // hw):
                x1 = acc[:, s * hw:s * hw + LANES]
                x2 = acc[:, s * hw + LANES:(s + 1) * hw]
                out[0][:, s * hw:s * hw + LANES] = (x1 * cs - x2 * sn).astype(BF16)
                out[0][:, s * hw + LANES:(s + 1) * hw] = (x2 * cs + x1 * sn).astype(BF16)

        @pl.when(j >= 2 * n_q)
        def _():
            out[0][...] = acc.astype(BF16)

    extras = [(cos, (bm, LANES), lambda i, j, k: (rows.pos_block(i), 0)),
              (sin, (bm, LANES), lambda i, j, k: (rows.pos_block(i), 0))]
    outs = [((T, 3 * D), BF16, (bm, bn), lambda i, j, k: (i, j))]
    return fused_matmul(u, w, extras, outs, epi, rows=rows, bn=bn, name="diff_qkv")[0]


def diff_attention(qkv, lambdas, subln_g, lambda_init, B, C, L, D, with_ctx):
    hd = DF_HEAD_DIM
    hw = 2 * hd
    heads = D // hw
    tq = C
    n_lat_t = L // tq
    n_t = n_lat_t + (1 if with_ctx else 0)
    ctx_blk = B * L // C
    S = C + L
    Tq = B * L + (B * C if with_ctx else 0)
    hp = 2
    kcol = D // (hp * hw)
    vcol = 2 * D // (hp * hw)

    def q_row(b, h, t):
        lat = b * n_lat_t + t
        return jnp.where(t < n_lat_t, lat, ctx_blk + b) if with_ctx else lat

    def kern(q_ref, kl_ref, kc_ref, vl_ref, vc_ref, lam_ref, g_ref, o_ref, k_s, v_s):
        t = pl.program_id(2)

        @pl.when(t == 0)
        def _():
            map0 = (lax.broadcasted_iota(jnp.int32, (1, hw), 1) % hd) < hd // 2
            for ref, lo, hi in ((kc_ref, 0, C), (kl_ref, C, S)):
                for hh in range(hp):
                    k = ref[:, hh * hw:(hh + 1) * hw]
                    k_s[2 * hh, lo:hi, :] = jnp.where(map0, k, jnp.zeros_like(k))
                    k_s[2 * hh + 1, lo:hi, :] = jnp.where(map0, jnp.zeros_like(k), k)
            for hh in range(hp):
                v_s[hh, :C, :] = vc_ref[:, hh * hw:(hh + 1) * hw]
                v_s[hh, C:, :] = vl_ref[:, hh * hw:(hh + 1) * hw]

        lf = lam_ref[...]
        lam = (jnp.exp(jnp.sum(lf[0:1] * lf[1:2], axis=-1, keepdims=True))
               - jnp.exp(jnp.sum(lf[2:3] * lf[3:4], axis=-1, keepdims=True)) + lambda_init)

        def attend(n_keys):
            nc = 2 * hp
            ss = [lax.dot_general(q_ref[:, (c // 2) * hw:(c // 2 + 1) * hw], k_s[c, :n_keys, :],
                                  (((1,), (1,)), ((), ())), preferred_element_type=F32) for c in range(nc)]
            ms = [jnp.max(s, axis=-1, keepdims=True) for s in ss]
            os_, ls = [0.0] * nc, [0.0] * nc
            for lo in range(0, n_keys, MXU_COLS):
                for c in range(nc):
                    p = jnp.exp2(ss[c][:, lo:lo + MXU_COLS] - ms[c])
                    ls[c] = ls[c] + jnp.sum(p[:, :LANES] + p[:, LANES:], axis=-1, keepdims=True)
                    os_[c] = os_[c] + jnp.dot(p.astype(BF16), v_s[c // 2, lo:lo + MXU_COLS, :],
                                              preferred_element_type=F32)
            for hh in range(hp):
                o = os_[2 * hh] / ls[2 * hh] - lam * (os_[2 * hh + 1] / ls[2 * hh + 1])
                o_ref[:, hh * hw:(hh + 1) * hw] = (_rms(o, g_ref[...], DF_SUBLN_EPS)
                                                   * (1.0 - lambda_init)).astype(BF16)

        if with_ctx:
            @pl.when(t < n_lat_t)
            def _():
                attend(S)

            @pl.when(t == n_lat_t)
            def _():
                attend(C)
        else:
            attend(S)

    return pl.pallas_call(
        kern,
        grid=(B, heads // hp, n_t),
        in_specs=[pl.BlockSpec((tq, hp * hw), lambda b, g, t: (q_row(b, g, t), g)),
                  pl.BlockSpec((L, hp * hw), lambda b, g, t: (b, kcol + g)),
                  pl.BlockSpec((C, hp * hw), lambda b, g, t: (ctx_blk + b, kcol + g)),
                  pl.BlockSpec((L, hp * hw), lambda b, g, t: (b, vcol + g)),
                  pl.BlockSpec((C, hp * hw), lambda b, g, t: (ctx_blk + b, vcol + g)),
                  pl.BlockSpec((4, hd), lambda b, g, t: (0, 0)),
                  pl.BlockSpec((1, hw), lambda b, g, t: (0, 0))],
        out_specs=pl.BlockSpec((tq, hp * hw), lambda b, g, t: (q_row(b, g, t), g)),
        out_shape=jax.ShapeDtypeStruct((Tq, D), BF16),
        scratch_shapes=[pltpu.VMEM((2 * hp, S, hw), BF16), pltpu.VMEM((hp, S, hw), BF16)],
        compiler_params=_params(("parallel", "parallel", "arbitrary")),
        name="diff_attention",
    )(qkv, qkv, qkv, qkv, qkv, lambdas, subln_g.reshape(1, hw))


def diff_mixer(h, u, mod, layer, p, geo, need_ctx, lambda_init, norm):
    B, C, L, rows_all, rows_out = geo
    D = h_width(h)
    cos, sin = diff_rope_tables(L, rows_all.bm)
    qkv = diff_qkv(u, p["w_qkv"], cos, sin, rows_all)
    o = diff_attention(qkv, p["lambdas"], p["subln_g"], lambda_init, B, C, L, D, need_ctx)
    return gated_matmul(o, p["w_o"].astype(BF16), h, mod, layer, 2, _out_rows(rows_out), bn=D, norm=norm,
                        name="diff_out")


def dft_matrices(Ls):
    n = 2 * Ls
    t0n = min(64, Ls)
    t1n = Ls // t0n
    f = jnp.arange(Ls, dtype=jnp.int32)[:, None]
    a1 = ((f * (jnp.arange(t1n, dtype=jnp.int32) * t0n)[None, :]) % n).astype(F32) * (2.0 * math.pi / n)
    a0 = ((f * jnp.arange(t0n, dtype=jnp.int32)[None, :]) % n).astype(F32) * (2.0 * math.pi / n)
    c1, s1 = jnp.cos(a1)[:, :, None], jnp.sin(a1)[:, :, None]
    c0, s0 = jnp.cos(a0)[:, None, :], jnp.sin(a0)[:, None, :]
    cosm = (c1 * c0 - s1 * s0).reshape(Ls, Ls)
    sinm = (s1 * c0 + c1 * s0).reshape(Ls, Ls)
    nyq = jnp.where(jnp.arange(Ls) % 2 == 0, 1.0, -1.0).astype(F32)[None, :]
    imag = jnp.where(f == 0, nyq, -sinm)
    fwd = jnp.concatenate([cosm, imag], axis=0)
    col = jnp.arange(n)
    cscale = jnp.where((col == 0) | (col == Ls), 1.0 / n, 2.0 / n).astype(F32)
    inv = fwd.T * cscale[None, :]
    return fwd.astype(BF16), inv.astype(BF16)


def hyena_filter_time(Ls, p):
    D = p["f_bias"].shape[0]
    order = p["f_w2"].shape[0]
    bands = (HY_EMB_DIM - 1) // 2
    t = jnp.linspace(0.0, 1.0, Ls, dtype=F32)[:, None]
    w = 2.0 * math.pi * jnp.arange(Ls, dtype=F32)[:, None] / Ls
    f = jnp.linspace(1e-4, bands - 1, bands, dtype=F32)
    feats = jnp.concatenate([t, jnp.cos(f * w), -jnp.sin(f * w)], axis=-1)
    feats = jnp.pad(feats, ((0, 0), (0, HY_EMB_PAD - HY_EMB_DIM)))
    w1 = jnp.pad(p["f_w1"], ((0, HY_EMB_PAD - HY_EMB_DIM), (0, 0)))
    deltas = jnp.abs(jnp.linspace(math.log(HY_TARGET) / HY_SLOW_PCT, math.log(HY_TARGET) / HY_FAST_PCT,
                                  D, dtype=F32))
    deltas2 = jnp.concatenate([deltas, deltas]).reshape(1, 2 * D)

    def ffn_kern(x_ref, w1_ref, b1_ref, w2_ref, b2_ref, w3_ref, b3_ref, fr_ref, o_ref):
        dot = functools.partial(jnp.dot, precision=HIGHEST, preferred_element_type=F32)
        fr = fr_ref[...]
        hcur = jnp.sin(fr[0:1] * (dot(x_ref[...], w1_ref[...]) + b1_ref[...]))
        hcur = jnp.sin(fr[1:2] * (dot(hcur, w2_ref[...]) + b2_ref[...]))
        o_ref[...] = jnp.sin(fr[2:3] * (dot(hcur, w3_ref[...]) + b3_ref[...]))

    full = lambda a: pl.BlockSpec(a.shape, lambda: (0,) * a.ndim)
    ffn_in = [feats, w1, p["f_b1"].reshape(1, order), p["f_w2"], p["f_b2"].reshape(1, order),
              p["f_w3"], p["f_b3"].reshape(1, order), p["f_freq"]]
    hff = pl.pallas_call(
        ffn_kern,
        in_specs=[full(a) for a in ffn_in],
        out_specs=pl.BlockSpec((Ls, order), lambda: (0, 0)),
        out_shape=jax.ShapeDtypeStruct((Ls, order), F32),
        name="hyena_filter_ffn",
    )(*ffn_in)

    bn = 512

    def out_kern(h_ref, w_ref, d_ref, o_ref):
        tt = lax.broadcasted_iota(jnp.int32, (Ls, 1), 0).astype(F32) * (1.0 / (Ls - 1))
        hw = jnp.dot(h_ref[...], w_ref[...], precision=HIGHEST, preferred_element_type=F32)
        o_ref[...] = hw * jnp.exp(-tt * d_ref[...])

    return pl.pallas_call(
        out_kern,
        grid=(2 * D // bn,),
        in_specs=[pl.BlockSpec((Ls, order), lambda j: (0, 0)),
                  pl.BlockSpec((order, bn), lambda j: (0, j)),
                  pl.BlockSpec((1, bn), lambda j: (0, j))],
        out_specs=pl.BlockSpec((Ls, bn), lambda j: (0, j)),
        out_shape=jax.ShapeDtypeStruct((Ls, 2 * D), F32),
        compiler_params=_params(("parallel",)),
        name="hyena_filter_out",
    )(hff, p["f_wout"], deltas2)


def hyena_filter_spectrum(hfb, fwd, Ls, D):
    n = 2 * Ls
    cb = LANES
    ncb = D // cb

    def kern(w_ref, hf_ref, hb_ref, o_ref):
        row = lax.broadcasted_iota(jnp.int32, (Ls, 1), 0)
        hb = jnp.where(row == 0, 0.0, hb_ref[...])
        a = jnp.dot(w_ref[...], hf_ref[...].astype(BF16), preferred_element_type=F32)
        b = jnp.dot(w_ref[...], hb.astype(BF16), preferred_element_type=F32)
        frow = lax.broadcasted_iota(jnp.int32, (n, 1), 0)
        o_ref[...] = a + jnp.where(frow > Ls, -b, b)

    return pl.pallas_call(
        kern,
        grid=(ncb,),
        in_specs=[pl.BlockSpec((n, Ls), lambda j: (0, 0), pipeline_mode=pl.Buffered(1)),
                  pl.BlockSpec((Ls, cb), lambda j: (0, j)),
                  pl.BlockSpec((Ls, cb), lambda j: (0, ncb + j))],
        out_specs=pl.BlockSpec((n, cb), lambda j: (0, j)),
        out_shape=jax.ShapeDtypeStruct((n, D), F32),
        compiler_params=_params(("parallel",)),
        name="hyena_filter_spectrum",
    )(fwd, hfb, hfb)


def _short_conv(x, w, b):
    n = x.shape[0]
    row = lax.broadcasted_iota(jnp.int32, (n, 1), 0)
    prev = jnp.where(row == 0, 0.0, pltpu.roll(x, 1, 0))
    nxt = jnp.where(row == n - 1, 0.0, pltpu.roll(x, n - 1, 0))
    return prev * w[0:1] + x * w[1:2] + nxt * w[2:3] + b


def hyena_segment(z, p, Ls, row_blk0, B, D):
    n = 2 * Ls
    cb = MXU_COLS
    ncb = D // cb
    fwd, inv = dft_matrices(Ls)
    hfb = hyena_filter_time(Ls, p)
    kf = hyena_filter_spectrum(hfb, fwd, Ls, D)
    conv_w, conv_b = p["conv_w"], p["conv_b"].reshape(1, 3 * D)

    def gate_kern(x1_ref, v_ref, cw1_ref, cb1_ref, cwv_ref, cbv_ref, g_ref):
        g_ref[...] = (_short_conv(v_ref[...].astype(F32), cwv_ref[...], cbv_ref[...])
                      * _short_conv(x1_ref[...].astype(F32), cw1_ref[...], cb1_ref[...])).astype(BF16)

    g = pl.pallas_call(
        gate_kern,
        grid=(ncb, B),
        in_specs=[pl.BlockSpec((Ls, cb), lambda j, b: (row_blk0 + b, ncb + j)),
                  pl.BlockSpec((Ls, cb), lambda j, b: (row_blk0 + b, 2 * ncb + j)),
                  pl.BlockSpec((HY_SHORT, cb), lambda j, b: (0, ncb + j)),
                  pl.BlockSpec((1, cb), lambda j, b: (0, ncb + j)),
                  pl.BlockSpec((HY_SHORT, cb), lambda j, b: (0, 2 * ncb + j)),
                  pl.BlockSpec((1, cb), lambda j, b: (0, 2 * ncb + j))],
        out_specs=pl.BlockSpec((Ls, cb), lambda j, b: (b, j)),
        out_shape=jax.ShapeDtypeStruct((B * Ls, D), BF16),
        compiler_params=_params(("parallel", "arbitrary")),
        name="hyena_gate",
    )(z, z, conv_w, conv_b, conv_w, conv_b)

    def fwd_kern(w_ref, g_ref, kf_ref, y_ref):
        u = jnp.dot(w_ref[...], g_ref[...], preferred_element_type=F32)
        ure, uim = u[:Ls], u[Ls:]
        kre, kim = kf_ref[:Ls, :], kf_ref[Ls:, :]
        first = lax.broadcasted_iota(jnp.int32, (Ls, 1), 0) == 0
        y_ref[:Ls, :] = (ure * kre - jnp.where(first, 0.0, uim * kim)).astype(BF16)
        y_ref[Ls:, :] = jnp.where(first, uim * kim, ure * kim + uim * kre).astype(BF16)

    y = pl.pallas_call(
        fwd_kern,
        grid=(ncb, B),
        in_specs=[pl.BlockSpec((n, Ls), lambda j, b: (0, 0), pipeline_mode=pl.Buffered(1)),
                  pl.BlockSpec((Ls, cb), lambda j, b: (b, j)),
                  pl.BlockSpec((n, cb), lambda j, b: (0, j))],
        out_specs=pl.BlockSpec((None, n, cb), lambda j, b: (b, 0, j)),
        out_shape=jax.ShapeDtypeStruct((B, n, D), BF16),
        compiler_params=_params(("parallel", "arbitrary")),
        name="hyena_dft",
    )(fwd, g, kf)

    def inv_kern(w_ref, y_ref, g_ref, x0_ref, cw0_ref, cb0_ref, fb_ref, o_ref):
        conv = jnp.dot(w_ref[...], y_ref[...], preferred_element_type=F32)
        x0 = _short_conv(x0_ref[...].astype(F32), cw0_ref[...], cb0_ref[...])
        o_ref[...] = ((conv + g_ref[...].astype(F32) * fb_ref[...]) * x0).astype(BF16)

    return pl.pallas_call(
        inv_kern,
        grid=(ncb, B),
        in_specs=[pl.BlockSpec((Ls, n), lambda j, b: (0, 0), pipeline_mode=pl.Buffered(1)),
                  pl.BlockSpec((None, n, cb), lambda j, b: (b, 0, j)),
                  pl.BlockSpec((Ls, cb), lambda j, b: (b, j)),
                  pl.BlockSpec((Ls, cb), lambda j, b: (row_blk0 + b, j)),
                  pl.BlockSpec((HY_SHORT, cb), lambda j, b: (0, j)),
                  pl.BlockSpec((1, cb), lambda j, b: (0, j)),
                  pl.BlockSpec((1, cb), lambda j, b: (0, j))],
        out_specs=pl.BlockSpec((Ls, cb), lambda j, b: (b, j)),
        out_shape=jax.ShapeDtypeStruct((B * Ls, D), BF16),
        compiler_params=_params(("parallel", "arbitrary")),
        name="hyena_idft",
    )(inv, y, g, z, conv_w, conv_b, p["f_bias"].reshape(1, D))


def hyena_mixer(h, u, mod, layer, p, geo, need_ctx, norm):
    B, C, L, rows_all, rows_out = geo
    D = h_width(h)
    z = plain_matmul(u, p["w_in"].astype(BF16), rows_out, 1024, bias=p["b_in"], name="hyena_in")
    y = hyena_segment(z, p, L, 0, B, D)
    if need_ctx:
        y = (y, hyena_segment(z, p, C, B * L // C, B, D))
    return gated_matmul(y, p["w_out"].astype(BF16), h, mod, layer, 2, _out_rows(rows_out), bn=D, bias=p["b_out"],
                        norm=norm, name="hyena_out")


def kernel(x, c, ctx, c_ctx, ada_w, ada_b, norm_g, mlp_w1, mlp_w2, final_norm_g, mla_w_dq, mla_q_norm_g, mla_w_uq, mla_w_dkv, mla_kv_norm_g, mla_w_ukv, mla_w_o, hy_w_in, hy_b_in, hy_conv_w, hy_conv_b, hy_filt_w1, hy_filt_b1, hy_filt_w2, hy_filt_b2, hy_filt_w3, hy_filt_b3, hy_filt_freq, hy_filt_wout, hy_filt_bias, hy_w_out, hy_b_out, df_w_qkv, df_lambda, df_subln_g, df_w_o):
    B, L, D = x.shape
    C = ctx.shape[1]
    depth = ada_w.shape[0]
    bm = min(1024, B * C)
    rows_all = Rows(B, C, L, bm)
    rows_lat = Rows(B, C, L, bm, lat_only=True)

    mod = adaln_table(jnp.concatenate([c_ctx[None, :], c], axis=0), ada_w, ada_b)
    norm_g4 = norm_g.reshape(depth, 2, 1, D)
    h = (x.reshape(B * L, D), ctx.reshape(B * C, D))
    u = norm_modulate(h, norm_g4, mod, 0, 0, rows_all)

    for i in range(depth):
        need_ctx = i < depth - 1
        rows_out = rows_all if need_ctx else rows_lat
        geo = (B, C, L, rows_all, rows_out)
        kind, j = i % N_MIXERS, i // N_MIXERS
        mlp_norm = ("modulate", norm_g4, i, 1)
        if kind == 0:
            p = dict(w_dq=mla_w_dq[j], q_g=mla_q_norm_g[j], w_uq=mla_w_uq[j], w_dkv=mla_w_dkv[j],
                     kv_g=mla_kv_norm_g[j], w_ukv=mla_w_ukv[j], w_o=mla_w_o[j])
            h, u = mla_mixer(h, u, mod, i, p, geo, need_ctx, mlp_norm)
        elif kind == 1:
            p = dict(w_in=hy_w_in[j], b_in=hy_b_in[j], conv_w=hy_conv_w[j], conv_b=hy_conv_b[j],
                     f_w1=hy_filt_w1[j], f_b1=hy_filt_b1[j], f_w2=hy_filt_w2[j], f_b2=hy_filt_b2[j],
                     f_w3=hy_filt_w3[j], f_b3=hy_filt_b3[j], f_freq=hy_filt_freq[j], f_wout=hy_filt_wout[j],
                     f_bias=hy_filt_bias[j], w_out=hy_w_out[j], b_out=hy_b_out[j])
            h, u = hyena_mixer(h, u, mod, i, p, geo, need_ctx, mlp_norm)
        else:
            lambda_init = 0.8 - 0.6 * math.exp(-0.3 * i)
            p = dict(w_qkv=df_w_qkv[j], lambdas=df_lambda[j], subln_g=df_subln_g[j], w_o=df_w_o[j])
            h, u = diff_mixer(h, u, mod, i, p, geo, need_ctx, lambda_init, mlp_norm)
        next_norm = ("modulate", norm_g4, i + 1, 0) if need_ctx else ("final", final_norm_g)
        res = mlp(h, u, mod, i, mlp_w1[i].astype(BF16), mlp_w2[i].astype(BF16), rows_out, next_norm)
        if need_ctx:
            h, u = res

    return res.reshape(B, L, D)
```

```python
import functools
import math

import jax
import jax.numpy as jnp
import numpy as np
from jax import lax
from jax.experimental import pallas as pl
from jax.experimental.pallas import tpu as pltpu

F32 = jnp.float32
BF16 = jnp.bfloat16
HIGHEST = lax.Precision.HIGHEST
LOG2E = math.log2(math.e)

GRID_W = 64
ROPE_THETA = 10000.0
NORM_EPS = 1e-6
N_MIXERS = 3

MLA_NOPE = 128
MLA_ROPE = 64
MLA_V = 128
MLA_KV_RANK = 512
MLA_HEAD_PAD = 256
MLA_Q_TILE = 512

HY_SHORT = 3
HY_EMB_DIM = 33
HY_EMB_PAD = 64
HY_TARGET = 1e-2
HY_FAST_PCT = 0.3
HY_SLOW_PCT = 1.5

DF_HEAD_DIM = 128
DF_SUBLN_EPS = 1e-5
DF_Q_TILE = 512

LANES = 128
MXU_COLS = 256
VMEM_LIMIT_MB = 56


def _params(semantics, vmem_mb=VMEM_LIMIT_MB):
    return pltpu.CompilerParams(dimension_semantics=semantics, vmem_limit_bytes=vmem_mb << 20)


class Rows:
    def __init__(self, B, C, L, bm, lat_only=False):
        assert L % bm == 0 and (B * C) % bm == 0
        self.geometry = (B, C, L, lat_only)
        self.bm = bm
        self.tiles_per_batch = L // bm
        self.n_lat = B * L // bm
        self.n = self.n_lat + (0 if lat_only else B * C // bm)

    def with_bm(self, bm):
        B, C, L, lat_only = self.geometry
        return Rows(B, C, L, bm, lat_only)

    def mod_row(self, i):
        return jnp.where(i < self.n_lat, 1 + i // self.tiles_per_batch, 0)

    def pos_block(self, i):
        return jnp.where(i < self.n_lat, i % self.tiles_per_batch, self.tiles_per_batch)


def _mod_spec(layer, chunk, rows, bn, col_of):
    return pl.BlockSpec((None, None, None, 1, bn),
                        lambda *g: (layer, chunk, rows.mod_row(g[0]), 0, col_of(*g)))


def _w_shape(w):
    return w[0].shape[1:] if isinstance(w, tuple) else w.shape


def _row_operand(arr, rows, width, col_of):
    bm = rows.bm
    n_lat = rows.n_lat
    if isinstance(arr, tuple) and rows.n == n_lat:
        arr = arr[0]
    if not isinstance(arr, tuple):
        return [(arr, (bm, width), lambda *g: (g[0], col_of(*g)))], lambda refs: refs[0][...]
    specs = [(arr[0], (bm, width), lambda *g: (jnp.minimum(g[0], n_lat - 1), col_of(*g))),
             (arr[1], (bm, width), lambda *g: (jnp.maximum(g[0] - n_lat, 0), col_of(*g)))]
    return specs, lambda refs: jnp.where(pl.program_id(0) < n_lat, refs[0][...], refs[1][...])


def fused_matmul(a, w, extras, outs, epilogue, *, rows, bn, bk=None, name):
    K, N = _w_shape(w)
    bk = bk or K
    nk = K // bk
    assert K % bk == 0 and N % bn == 0
    a_specs, read_a = _row_operand(a, rows, bk, lambda i, j, k: k)
    n_a = len(a_specs)
    in_specs = [pl.BlockSpec(bs, im) for _, bs, im in a_specs]
    if isinstance(w, tuple):
        w, w_layer = w
        in_specs += [pl.BlockSpec((None, bk, bn), lambda i, j, k: (w_layer, k, j))]
    else:
        in_specs += [pl.BlockSpec((bk, bn), lambda i, j, k: (k, j))]
    in_specs += [pl.BlockSpec(bs, im) for _, bs, im in extras]
    n_ex = len(extras)
    n_out = len(outs)

    def kern(*refs):
        w_ref = refs[n_a]
        ex = refs[n_a + 1:n_a + 1 + n_ex]
        out = refs[n_a + 1 + n_ex:n_a + 1 + n_ex + n_out]

        def dot():
            return jnp.dot(read_a(refs[:n_a]), w_ref[...], preferred_element_type=F32)

        if nk == 1:
            epilogue(dot(), ex, out)
        else:
            acc_ref = out[0]
            k = pl.program_id(2)

            @pl.when(k == 0)
            def _():
                acc_ref[...] = dot()

            @pl.when((k > 0) & (k < nk - 1))
            def _():
                acc_ref[...] += dot()

            @pl.when(k == nk - 1)
            def _():
                epilogue(acc_ref[...] + dot(), ex, out)

    res = pl.pallas_call(
        kern,
        grid=(rows.n, N // bn, nk),
        in_specs=in_specs,
        out_specs=[pl.BlockSpec(bs, im) for _, _, bs, im in outs],
        out_shape=[jax.ShapeDtypeStruct(s, d) for s, d, _, _ in outs],
        compiler_params=_params(("parallel", "arbitrary", "arbitrary")),
        name=name,
    )(*[e[0] for e in a_specs], w, *[e[0] for e in extras])
    return res


def _rot_half(x, quarter):
    n = x.shape[-1]
    lane = lax.broadcasted_iota(jnp.int32, x.shape, x.ndim - 1)
    first = (lane % (2 * quarter)) < quarter
    return jnp.where(first, -pltpu.roll(x, n - quarter, x.ndim - 1), pltpu.roll(x, quarter, x.ndim - 1))


def _rms(x, g, eps):
    return x * lax.rsqrt(jnp.mean(x * x, axis=-1, keepdims=True) + eps) * g


def gated_matmul(a, w, resid, mod, layer, chunk, rows, *, name, bn=1024, bk=None, bias=None, norm=None):
    T = rows.n * rows.bm
    bm = rows.bm
    N = _w_shape(w)[1]
    res_specs, read_res = _row_operand(resid, rows, bn, lambda i, j, k: j)
    n_res = len(res_specs)
    extras = res_specs + [(mod,) + _spec_parts(_mod_spec(layer, chunk, rows, bn, lambda i, j, k: j))]
    if bias is not None:
        extras.append((bias.reshape(1, N), (1, bn), lambda i, j, k: (0, j)))
    n_fix = len(extras)
    tile = lambda i, j, k: (i, j)
    outs = [((T, N), F32, (bm, bn), tile)]
    if norm is not None:
        assert bn == N
        if norm[0] == "modulate":
            _, g4, nl, which = norm
            extras += [(g4, (None, None, 1, N), lambda i, j, k: (nl, which, 0, 0)),
                       (mod,) + _spec_parts(_mod_spec(nl, 3 * which, rows, N, lambda i, j, k: 0)),
                       (mod,) + _spec_parts(_mod_spec(nl, 3 * which + 1, rows, N, lambda i, j, k: 0))]
            outs.append(((T, N), BF16, (bm, bn), tile))
        else:
            extras.append((norm[1].reshape(1, N), (1, N), lambda i, j, k: (0, 0)))

    def epi(acc, ex, out):
        y = acc if bias is None else acc + ex[n_res + 1][...]
        hn = read_res(ex[:n_res]) + ex[n_res][...] * y
        if norm is None:
            out[0][...] = hn
        elif norm[0] == "modulate":
            g_ref, sh_ref, sc_ref = ex[n_fix:]
            out[0][...] = hn
            out[1][...] = (_rms(hn, g_ref[...], NORM_EPS) * (1.0 + sc_ref[...]) + sh_ref[...]).astype(BF16)
        else:
            out[0][...] = _rms(hn, ex[n_fix][...], NORM_EPS)

    res = fused_matmul(a, w, extras, outs, epi, rows=rows, bn=bn, bk=bk, name=name)
    return res if len(res) > 1 else res[0]


def _spec_parts(spec):
    return spec.block_shape, spec.index_map


def h_width(h):
    return (h[0] if isinstance(h, tuple) else h).shape[1]


def _out_rows(rows):
    return rows.with_bm(min(512, rows.bm))


def adaln_table(cvec, ada_w, ada_b):
    depth, D, _ = ada_w.shape
    R = cvec.shape[0]

    def kern(c_ref, w_ref, b_ref, o_ref):
        c = c_ref[...]
        s = c / (1.0 + jnp.exp(-c))
        o_ref[...] = jnp.dot(s, w_ref[...], precision=HIGHEST, preferred_element_type=F32) + b_ref[...]

    out = pl.pallas_call(
        kern,
        grid=(depth, 6),
        in_specs=[pl.BlockSpec((R, D), lambda l, j: (0, 0)),
                  pl.BlockSpec((None, D, D), lambda l, j: (l, 0, j)),
                  pl.BlockSpec((None, 1, D), lambda l, j: (l, 0, j))],
        out_specs=pl.BlockSpec((None, None, R, D), lambda l, j: (l, j, 0, 0)),
        out_shape=jax.ShapeDtypeStruct((depth, 6, R, D), F32),
        compiler_params=_params(("parallel", "arbitrary")),
        name="adaln_table",
    )(cvec, ada_w, ada_b.reshape(depth, 1, 6 * D))
    return out.reshape(depth, 6, R, 1, D)


def norm_modulate(h, norm_g4, mod, layer, which, rows):
    D = norm_g4.shape[-1]
    bm = rows.bm
    h_specs, read_h = _row_operand(h, rows, D, lambda i: 0)
    n_h = len(h_specs)

    def kern(*refs):
        g_ref, sh_ref, sc_ref, o_ref = refs[n_h:]
        y = _rms(read_h(refs[:n_h]), g_ref[...], NORM_EPS)
        o_ref[...] = (y * (1.0 + sc_ref[...]) + sh_ref[...]).astype(BF16)

    return pl.pallas_call(
        kern,
        grid=(rows.n,),
        in_specs=[pl.BlockSpec(bs, im) for _, bs, im in h_specs]
        + [pl.BlockSpec((None, None, 1, D), lambda i: (layer, which, 0, 0)),
           _mod_spec(layer, 3 * which, rows, D, lambda i: 0),
           _mod_spec(layer, 3 * which + 1, rows, D, lambda i: 0)],
        out_specs=pl.BlockSpec((bm, D), lambda i: (i, 0)),
        out_shape=jax.ShapeDtypeStruct((rows.n * bm, D), BF16),
        compiler_params=_params(("parallel",)),
        name="norm_modulate",
    )(*[e[0] for e in h_specs], norm_g4, mod, mod)


def mlp(h, u, mod, layer, w1, w2, rows, norm):
    T = rows.n * rows.bm
    F, D = _w_shape(w2)

    def relu2(acc, ex, out):
        r = jnp.maximum(acc, 0.0)
        out[0][...] = (r * r).astype(BF16)

    hid = fused_matmul(u, w1, [], [((T, F), BF16, (rows.bm, 1024), lambda i, j, k: (i, j))], relu2,
                       rows=rows, bn=1024, name="mlp_up")[0]
    return gated_matmul(hid, w2, h, mod, layer, 5, rows, bn=D, bk=1024, norm=norm, name="mlp_down")


def axial_rope_tables(L, rot_dim):
    rows = L // GRID_W
    row = jnp.repeat(jnp.arange(rows, dtype=F32), GRID_W)
    col = jnp.tile(jnp.arange(GRID_W, dtype=F32), rows)
    pos = jnp.stack([row, col], axis=-1)
    n_freq = rot_dim // 4
    inv_freq = ROPE_THETA ** (-jnp.arange(n_freq, dtype=F32) / n_freq)
    ang = pos[:, :, None, None] * inv_freq
    ang = jnp.broadcast_to(ang, (L, 2, 2, n_freq)).reshape(L, rot_dim)
    return jnp.cos(ang), jnp.sin(ang)


def rope_tables_padded(L, rot_dim, bm):
    cos, sin = axial_rope_tables(L, rot_dim)
    cos = jnp.pad(cos, ((0, bm), (0, LANES - rot_dim)), constant_values=1.0)
    sin = jnp.pad(sin, ((0, bm), (0, LANES - rot_dim)))
    return cos, sin


def mla_down(u, w_dq, q_g, w_dkv, kv_g, cos, sin, rows):
    T = rows.n * rows.bm
    bm = rows.bm
    qr = w_dq.shape[1]
    wd = jnp.concatenate([w_dq, w_dkv, jnp.zeros((w_dq.shape[0], LANES - MLA_ROPE), w_dq.dtype)],
                         axis=1).astype(BF16)
    n_all = wd.shape[1]
    c0 = qr + MLA_KV_RANK

    def epi(acc, ex, out):
        qg_ref, kvg_ref, cos_ref, sin_ref = ex
        out[0][...] = _rms(acc[:, :qr], qg_ref[...], NORM_EPS).astype(BF16)
        out[1][...] = _rms(acc[:, qr:c0], kvg_ref[...], NORM_EPS).astype(BF16)
        kr = acc[:, c0:]
        out[2][...] = (kr * cos_ref[...] + _rot_half(kr, MLA_ROPE // 4) * sin_ref[...]).astype(BF16)

    extras = [(q_g.reshape(1, qr), (1, qr), lambda i, j, k: (0, 0)),
              (kv_g.reshape(1, MLA_KV_RANK), (1, MLA_KV_RANK), lambda i, j, k: (0, 0)),
              (cos, (bm, LANES), lambda i, j, k: (rows.pos_block(i), 0)),
              (sin, (bm, LANES), lambda i, j, k: (rows.pos_block(i), 0))]
    outs = [((T, qr), BF16, (bm, qr), lambda i, j, k: (i, 0)),
            ((T, MLA_KV_RANK), BF16, (bm, MLA_KV_RANK), lambda i, j, k: (i, 0)),
            ((T, LANES), BF16, (bm, LANES), lambda i, j, k: (i, 0))]
    return fused_matmul(u, wd, extras, outs, epi, rows=rows, bn=n_all, name="mla_down")


def mla_queries(cq, w_uq, cos, sin, rows, heads):
    T = rows.n * rows.bm
    bm = rows.bm
    qr = w_uq.shape[0]
    hd = MLA_NOPE + MLA_ROPE
    w = w_uq.reshape(qr, heads, hd)
    w = jnp.pad(w, ((0, 0), (0, 0), (0, MLA_HEAD_PAD - hd))).reshape(qr, heads * MLA_HEAD_PAD).astype(BF16)
    scale = hd ** -0.5 * LOG2E
    bn = 2048

    def epi(acc, ex, out):
        cos_ref, sin_ref = ex
        for hh in range(bn // MLA_HEAD_PAD):
            c = hh * MLA_HEAD_PAD
            out[0][:, c:c + LANES] = (acc[:, c:c + LANES] * scale).astype(BF16)
            r = acc[:, c + LANES:c + 2 * LANES]
            r = r * cos_ref[...] + _rot_half(r, MLA_ROPE // 4) * sin_ref[...]
            out[0][:, c + LANES:c + 2 * LANES] = (r * scale).astype(BF16)

    extras = [(cos, (bm, LANES), lambda i, j, k: (rows.pos_block(i), 0)),
              (sin, (bm, LANES), lambda i, j, k: (rows.pos_block(i), 0))]
    outs = [((T, heads * MLA_HEAD_PAD), BF16, (bm, bn), lambda i, j, k: (i, j))]
    return fused_matmul(cq, w, extras, outs, epi, rows=rows, bn=bn, name="mla_queries")[0]


def plain_matmul(a, w, rows, bn, *, name, out_dtype=BF16, bias=None):
    T = rows.n * rows.bm
    N = w.shape[1]
    extras = [] if bias is None else [(bias.reshape(1, N), (1, bn), lambda i, j, k: (0, j))]

    def epi(acc, ex, out):
        y = acc if bias is None else acc + ex[0][...]
        out[0][...] = y.astype(out_dtype)

    return fused_matmul(a, w, extras, [((T, N), out_dtype, (rows.bm, bn), lambda i, j, k: (i, j))], epi,
                        rows=rows, bn=bn, name=name)[0]


def mla_attention(q, kv, kr, B, C, L, heads, ctx_queries):
    hp = 4
    tq = C if ctx_queries else MLA_Q_TILE
    n_t = 1 if ctx_queries else L // tq
    ctx_blk = B * L // C
    S = C if ctx_queries else C + L
    kvw = 2 * LANES
    q_row = (lambda b, g, t: ctx_blk + b) if ctx_queries else (lambda b, g, t: b * n_t + t)

    def kern(*refs):
        if ctx_queries:
            q_ref, kvc_ref, krc_ref, o_ref, k_s, v_s = refs
            parts = [(kvc_ref, krc_ref, 0, C)]
        else:
            q_ref, kvl_ref, kvc_ref, krl_ref, krc_ref, o_ref, k_s, v_s = refs
            parts = [(kvc_ref, krc_ref, 0, C), (kvl_ref, krl_ref, C, S)]

        @pl.when(pl.program_id(2) == 0)
        def _():
            for hh in range(hp):
                c0 = hh * kvw
                for kv_ref, kr_ref, lo, hi in parts:
                    k_s[hh, lo:hi, :LANES] = kv_ref[:, c0:c0 + LANES]
                    k_s[hh, lo:hi, LANES:] = kr_ref[...]
                    v_s[hh, lo:hi, :MLA_V] = kv_ref[:, c0 + LANES:c0 + kvw]
                v_s[hh, :, MLA_V:] = jnp.ones((S, MXU_COLS - MLA_V), BF16)

        ss = [lax.dot_general(q_ref[:, hh * MLA_HEAD_PAD:(hh + 1) * MLA_HEAD_PAD], k_s[hh],
                              (((1,), (1,)), ((), ())), preferred_element_type=F32) for hh in range(hp)]
        ps = [jnp.exp2(s - jnp.max(s, axis=-1, keepdims=True)).astype(BF16) for s in ss]
        for hh in range(hp):
            ov = jnp.dot(ps[hh], v_s[hh], preferred_element_type=F32)
            o_ref[:, hh * MLA_V:(hh + 1) * MLA_V] = (ov[:, :MLA_V] / ov[:, MLA_V:2 * MLA_V]).astype(BF16)

    lat_specs = [pl.BlockSpec((L, hp * kvw), lambda b, g, t: (b, g)),
                 pl.BlockSpec((L, LANES), lambda b, g, t: (b, 0))]
    ctx_specs = [pl.BlockSpec((C, hp * kvw), lambda b, g, t: (ctx_blk + b, g)),
                 pl.BlockSpec((C, LANES), lambda b, g, t: (ctx_blk + b, 0))]
    q_spec = pl.BlockSpec((tq, hp * MLA_HEAD_PAD), lambda b, g, t: (q_row(b, g, t), g))
    if ctx_queries:
        in_specs, args = [q_spec] + ctx_specs, (q, kv, kr)
    else:
        in_specs, args = [q_spec, lat_specs[0], ctx_specs[0], lat_specs[1], ctx_specs[1]], (q, kv, kv, kr, kr)
    return pl.pallas_call(
        kern,
        grid=(B, heads // hp, n_t),
        in_specs=in_specs,
        out_specs=pl.BlockSpec((tq, hp * MLA_V), lambda b, g, t: (b * n_t + t, g)),
        out_shape=jax.ShapeDtypeStruct((B * n_t * tq, heads * MLA_V), BF16),
        scratch_shapes=[pltpu.VMEM((hp, S, MLA_HEAD_PAD), BF16), pltpu.VMEM((hp, S, MXU_COLS), BF16)],
        compiler_params=_params(("parallel", "parallel", "arbitrary")),
        name="mla_attention_ctx" if ctx_queries else "mla_attention",
    )(*args)


def mla_mixer(h, u, mod, layer, p, geo, need_ctx, norm):
    B, C, L, rows_all, rows_out = geo
    heads = p["w_uq"].shape[1] // (MLA_NOPE + MLA_ROPE)
    cos, sin = rope_tables_padded(L, MLA_ROPE, rows_all.bm)
    cq, ckv, kr = mla_down(u, p["w_dq"], p["q_g"], p["w_dkv"], p["kv_g"], cos, sin, rows_all)
    kv = plain_matmul(ckv, p["w_ukv"].astype(BF16), rows_all, p["w_ukv"].shape[1], name="mla_kv")
    q = mla_queries(cq, p["w_uq"], cos, sin, rows_out, heads)
    o = mla_attention(q, kv, kr, B, C, L, heads, False)
    if need_ctx:
        o = (o, mla_attention(q, kv, kr, B, C, L, heads, True))
    return gated_matmul(o, p["w_o"].astype(BF16), h, mod, layer, 2, _out_rows(rows_out), bn=h_width(h), norm=norm,
                        name="mla_out")


def _diff_pair_layout():
    quarter = DF_HEAD_DIM // 4
    n = np.arange(2 * DF_HEAD_DIM)
    hf, c, a, r = n // DF_HEAD_DIM, (n % DF_HEAD_DIM) // (2 * quarter), (n % (2 * quarter)) // quarter, n % quarter
    head_perm = c * DF_HEAD_DIM + a * 2 * quarter + hf * quarter + r
    table_cols = (a * 2 * quarter + r)[:DF_HEAD_DIM]
    return head_perm, table_cols


def diff_rope_tables(L, bm):
    cos, sin = axial_rope_tables(L, DF_HEAD_DIM)
    _, cols = _diff_pair_layout()
    cos = jnp.pad(cos[:, cols], ((0, bm), (0, 0)), constant_values=1.0)
    sin = jnp.pad(sin[:, cols], ((0, bm), (0, 0)))
    return cos, sin


def diff_qkv(u, w_qkv, cos, sin, rows):
    T = rows.n * rows.bm
    bm = rows.bm
    D = w_qkv.shape[0]
    bn = 1024
    hw = 2 * DF_HEAD_DIM
    scale = DF_HEAD_DIM ** -0.5 * LOG2E
    n_q = D // bn
    head_perm, _ = _diff_pair_layout()
    qk_cols = (np.arange(2 * D) // hw * hw)[:, None].reshape(-1, hw) + head_perm[None, :]
    cols = np.concatenate([qk_cols.reshape(-1), np.arange(2 * D, 3 * D)])
    w = w_qkv[:, cols].astype(BF16)

    def epi(acc, ex, out):
        cos_ref, sin_ref = ex
        j = pl.program_id(1)

        @pl.when(j < 2 * n_q)
        def _():
            mul = jnp.where(j < n_q, scale, 1.0).astype(F32)
            cs, sn = cos_ref[...] * mul, sin_ref[...] * mul
            for s in range(bn // hw):
                x1 = acc[:, s * hw:s * hw + LANES]
                x2 = acc[:, s * hw + LANES:(s + 1) * hw]
                out[0][:, s * hw:s * hw + LANES] = (x1 * cs - x2 * sn).astype(BF16)
                out[0][:, s * hw + LANES:(s + 1) * hw] = (x2 * cs + x1 * sn).astype(BF16)

        @pl.when(j >= 2 * n_q)
        def _():
            out[0][...] = acc.astype(BF16)

    extras = [(cos, (bm, LANES), lambda i, j, k: (rows.pos_block(i), 0)),
              (sin, (bm, LANES), lambda i, j, k: (rows.pos_block(i), 0))]
    outs = [((T, 3 * D), BF16, (bm, bn), lambda i, j, k: (i, j))]
    return fused_matmul(u, w, extras, outs, epi, rows=rows, bn=bn, name="diff_qkv")[0]


def diff_attention(qkv, lambdas, subln_g, lambda_init, B, C, L, D, ctx_queries):
    hd = DF_HEAD_DIM
    hw = 2 * hd
    heads = D // hw
    hp = 2
    tq = C if ctx_queries else DF_Q_TILE
    n_t = 1 if ctx_queries else L // tq
    ctx_blk = B * L // C
    S = C if ctx_queries else C + L
    kcol = D // (hp * hw)
    vcol = 2 * D // (hp * hw)
    q_row = (lambda b, g, t: ctx_blk + b) if ctx_queries else (lambda b, g, t: b * n_t + t)

    def kern(*refs):
        if ctx_queries:
            q_ref, kc_ref, vc_ref, lam_ref, g_ref, o_ref, k_s, v_s = refs
            parts = [(kc_ref, vc_ref, 0, C)]
        else:
            q_ref, kl_ref, kc_ref, vl_ref, vc_ref, lam_ref, g_ref, o_ref, k_s, v_s = refs
            parts = [(kc_ref, vc_ref, 0, C), (kl_ref, vl_ref, C, S)]

        @pl.when(pl.program_id(2) == 0)
        def _():
            map0 = (lax.broadcasted_iota(jnp.int32, (1, hw), 1) % hd) < hd // 2
            for k_ref, v_ref, lo, hi in parts:
                for hh in range(hp):
                    k = k_ref[:, hh * hw:(hh + 1) * hw]
                    k_s[2 * hh, lo:hi, :] = jnp.where(map0, k, jnp.zeros_like(k))
                    k_s[2 * hh + 1, lo:hi, :] = jnp.where(map0, jnp.zeros_like(k), k)
                    v_s[hh, lo:hi, :] = v_ref[:, hh * hw:(hh + 1) * hw]

        lf = lam_ref[...]
        lam = (jnp.exp(jnp.sum(lf[0:1] * lf[1:2], axis=-1, keepdims=True))
               - jnp.exp(jnp.sum(lf[2:3] * lf[3:4], axis=-1, keepdims=True)) + lambda_init)

        nc = 2 * hp
        ss = [lax.dot_general(q_ref[:, (c // 2) * hw:(c // 2 + 1) * hw], k_s[c],
                              (((1,), (1,)), ((), ())), preferred_element_type=F32) for c in range(nc)]
        ms = [jnp.max(s, axis=-1, keepdims=True) for s in ss]
        os_, ls = [0.0] * nc, [0.0] * nc
        for lo in range(0, S, MXU_COLS):
            for c in range(nc):
                p = jnp.exp2(ss[c][:, lo:lo + MXU_COLS] - ms[c])
                ls[c] = ls[c] + jnp.sum(p[:, :LANES] + p[:, LANES:], axis=-1, keepdims=True)
                os_[c] = os_[c] + jnp.dot(p.astype(BF16), v_s[c // 2, lo:lo + MXU_COLS, :],
                                          preferred_element_type=F32)
        for hh in range(hp):
            o = os_[2 * hh] / ls[2 * hh] - lam * (os_[2 * hh + 1] / ls[2 * hh + 1])
            o_ref[:, hh * hw:(hh + 1) * hw] = (_rms(o, g_ref[...], DF_SUBLN_EPS)
                                               * (1.0 - lambda_init)).astype(BF16)

    bw = hp * hw
    q_spec = pl.BlockSpec((tq, bw), lambda b, g, t: (q_row(b, g, t), g))
    lat_specs = [pl.BlockSpec((L, bw), lambda b, g, t: (b, kcol + g)),
                 pl.BlockSpec((L, bw), lambda b, g, t: (b, vcol + g))]
    ctx_specs = [pl.BlockSpec((C, bw), lambda b, g, t: (ctx_blk + b, kcol + g)),
                 pl.BlockSpec((C, bw), lambda b, g, t: (ctx_blk + b, vcol + g))]
    par_specs = [pl.BlockSpec((4, hd), lambda b, g, t: (0, 0)), pl.BlockSpec((1, hw), lambda b, g, t: (0, 0))]
    if ctx_queries:
        in_specs, args = [q_spec] + ctx_specs, (qkv, qkv, qkv)
    else:
        in_specs = [q_spec, lat_specs[0], ctx_specs[0], lat_specs[1], ctx_specs[1]]
        args = (qkv, qkv, qkv, qkv, qkv)
    return pl.pallas_call(
        kern,
        grid=(B, heads // hp, n_t),
        in_specs=in_specs + par_specs,
        out_specs=pl.BlockSpec((tq, bw), lambda b, g, t: (b * n_t + t, g)),
        out_shape=jax.ShapeDtypeStruct((B * n_t * tq, D), BF16),
        scratch_shapes=[pltpu.VMEM((2 * hp, S, hw), BF16), pltpu.VMEM((hp, S, hw), BF16)],
        compiler_params=_params(("parallel", "parallel", "arbitrary")),
        name="diff_attention_ctx" if ctx_queries else "diff_attention",
    )(*args, lambdas, subln_g.reshape(1, hw))


def diff_mixer(h, u, mod, layer, p, geo, need_ctx, lambda_init, norm):
    B, C, L, rows_all, rows_out = geo
    D = h_width(h)
    cos, sin = diff_rope_tables(L, rows_all.bm)
    qkv = diff_qkv(u, p["w_qkv"], cos, sin, rows_all)
    o = diff_attention(qkv, p["lambdas"], p["subln_g"], lambda_init, B, C, L, D, False)
    if need_ctx:
        o = (o, diff_attention(qkv, p["lambdas"], p["subln_g"], lambda_init, B, C, L, D, True))
    return gated_matmul(o, p["w_o"].astype(BF16), h, mod, layer, 2, _out_rows(rows_out), bn=D, norm=norm,
                        name="diff_out")


def dft_matrices(Ls):
    n = 2 * Ls
    t0n = min(64, Ls)
    t1n = Ls // t0n
    f = jnp.arange(Ls, dtype=jnp.int32)[:, None]
    a1 = ((f * (jnp.arange(t1n, dtype=jnp.int32) * t0n)[None, :]) % n).astype(F32) * (2.0 * math.pi / n)
    a0 = ((f * jnp.arange(t0n, dtype=jnp.int32)[None, :]) % n).astype(F32) * (2.0 * math.pi / n)
    c1, s1 = jnp.cos(a1)[:, :, None], jnp.sin(a1)[:, :, None]
    c0, s0 = jnp.cos(a0)[:, None, :], jnp.sin(a0)[:, None, :]
    cosm = (c1 * c0 - s1 * s0).reshape(Ls, Ls)
    sinm = (s1 * c0 + c1 * s0).reshape(Ls, Ls)
    nyq = jnp.where(jnp.arange(Ls) % 2 == 0, 1.0, -1.0).astype(F32)[None, :]
    imag = jnp.where(f == 0, nyq, -sinm)
    fwd = jnp.concatenate([cosm, imag], axis=0)
    col = jnp.arange(n)
    cscale = jnp.where((col == 0) | (col == Ls), 1.0 / n, 2.0 / n).astype(F32)
    inv = fwd.T * cscale[None, :]
    return fwd.astype(BF16), inv.astype(BF16)


def hyena_filter_time(Ls, p):
    D = p["f_bias"].shape[0]
    order = p["f_w2"].shape[0]
    bands = (HY_EMB_DIM - 1) // 2
    t = jnp.linspace(0.0, 1.0, Ls, dtype=F32)[:, None]
    w = 2.0 * math.pi * jnp.arange(Ls, dtype=F32)[:, None] / Ls
    f = jnp.linspace(1e-4, bands - 1, bands, dtype=F32)
    feats = jnp.concatenate([t, jnp.cos(f * w), -jnp.sin(f * w)], axis=-1)
    feats = jnp.pad(feats, ((0, 0), (0, HY_EMB_PAD - HY_EMB_DIM)))
    w1 = jnp.pad(p["f_w1"], ((0, HY_EMB_PAD - HY_EMB_DIM), (0, 0)))
    deltas = jnp.abs(jnp.linspace(math.log(HY_TARGET) / HY_SLOW_PCT, math.log(HY_TARGET) / HY_FAST_PCT,
                                  D, dtype=F32))
    deltas2 = jnp.concatenate([deltas, deltas]).reshape(1, 2 * D)

    def ffn_kern(x_ref, w1_ref, b1_ref, w2_ref, b2_ref, w3_ref, b3_ref, fr_ref, o_ref):
        dot = functools.partial(jnp.dot, precision=HIGHEST, preferred_element_type=F32)
        fr = fr_ref[...]
        hcur = jnp.sin(fr[0:1] * (dot(x_ref[...], w1_ref[...]) + b1_ref[...]))
        hcur = jnp.sin(fr[1:2] * (dot(hcur, w2_ref[...]) + b2_ref[...]))
        o_ref[...] = jnp.sin(fr[2:3] * (dot(hcur, w3_ref[...]) + b3_ref[...]))

    full = lambda a: pl.BlockSpec(a.shape, lambda: (0,) * a.ndim)
    ffn_in = [feats, w1, p["f_b1"].reshape(1, order), p["f_w2"], p["f_b2"].reshape(1, order),
              p["f_w3"], p["f_b3"].reshape(1, order), p["f_freq"]]
    hff = pl.pallas_call(
        ffn_kern,
        in_specs=[full(a) for a in ffn_in],
        out_specs=pl.BlockSpec((Ls, order), lambda: (0, 0)),
        out_shape=jax.ShapeDtypeStruct((Ls, order), F32),
        name="hyena_filter_ffn",
    )(*ffn_in)

    bn = 512

    def out_kern(h_ref, w_ref, d_ref, o_ref):
        tt = lax.broadcasted_iota(jnp.int32, (Ls, 1), 0).astype(F32) * (1.0 / (Ls - 1))
        hw = jnp.dot(h_ref[...], w_ref[...], precision=HIGHEST, preferred_element_type=F32)
        o_ref[...] = hw * jnp.exp(-tt * d_ref[...])

    return pl.pallas_call(
        out_kern,
        grid=(2 * D // bn,),
        in_specs=[pl.BlockSpec((Ls, order), lambda j: (0, 0)),
                  pl.BlockSpec((order, bn), lambda j: (0, j)),
                  pl.BlockSpec((1, bn), lambda j: (0, j))],
        out_specs=pl.BlockSpec((Ls, bn), lambda j: (0, j)),
        out_shape=jax.ShapeDtypeStruct((Ls, 2 * D), F32),
        compiler_params=_params(("parallel",)),
        name="hyena_filter_out",
    )(hff, p["f_wout"], deltas2)


def hyena_filter_spectrum(hfb, f_bias, fwd, Ls, D):
    n = 2 * Ls
    cb = LANES
    ncb = D // cb

    def kern(w_ref, hf_ref, hb_ref, fb_ref, o_ref):
        row = lax.broadcasted_iota(jnp.int32, (Ls, 1), 0)
        hb = jnp.where(row == 0, 0.0, hb_ref[...])
        a = jnp.dot(w_ref[...], hf_ref[...].astype(BF16), preferred_element_type=F32)
        b = jnp.dot(w_ref[...], hb.astype(BF16), preferred_element_type=F32)
        frow = lax.broadcasted_iota(jnp.int32, (n, 1), 0)
        real = frow <= Ls
        o_ref[...] = a + jnp.where(real, b + fb_ref[...], -b)

    return pl.pallas_call(
        kern,
        grid=(ncb,),
        in_specs=[pl.BlockSpec((n, Ls), lambda j: (0, 0), pipeline_mode=pl.Buffered(1)),
                  pl.BlockSpec((Ls, cb), lambda j: (0, j)),
                  pl.BlockSpec((Ls, cb), lambda j: (0, ncb + j)),
                  pl.BlockSpec((1, cb), lambda j: (0, j))],
        out_specs=pl.BlockSpec((n, cb), lambda j: (0, j)),
        out_shape=jax.ShapeDtypeStruct((n, D), F32),
        compiler_params=_params(("parallel",)),
        name="hyena_filter_spectrum",
    )(fwd, hfb, hfb, f_bias.reshape(1, D))


def _short_conv(x, w, b):
    n = x.shape[0]
    row = lax.broadcasted_iota(jnp.int32, (n, 1), 0)
    prev = jnp.where(row == 0, 0.0, pltpu.roll(x, 1, 0))
    nxt = jnp.where(row == n - 1, 0.0, pltpu.roll(x, n - 1, 0))
    return prev * w[0:1] + x * w[1:2] + nxt * w[2:3] + b


def hyena_segment(z, p, Ls, row_blk0, B, D):
    n = 2 * Ls
    cb = MXU_COLS
    ncb = D // cb
    fwd, inv = dft_matrices(Ls)
    hfb = hyena_filter_time(Ls, p)
    kf = hyena_filter_spectrum(hfb, p["f_bias"], fwd, Ls, D)
    conv_w, conv_b = p["conv_w"], p["conv_b"].reshape(1, 3 * D)

    def fwd_kern(w_ref, x1_ref, v_ref, cw1_ref, cb1_ref, cwv_ref, cbv_ref, kf_ref, y_ref):
        g = (_short_conv(v_ref[...].astype(F32), cwv_ref[...], cbv_ref[...])
             * _short_conv(x1_ref[...].astype(F32), cw1_ref[...], cb1_ref[...])).astype(BF16)
        u = jnp.dot(w_ref[...], g, preferred_element_type=F32)
        ure, uim = u[:Ls], u[Ls:]
        kre, kim = kf_ref[:Ls, :], kf_ref[Ls:, :]
        first = lax.broadcasted_iota(jnp.int32, (Ls, 1), 0) == 0
        y_ref[:Ls, :] = (ure * kre - jnp.where(first, 0.0, uim * kim)).astype(BF16)
        y_ref[Ls:, :] = jnp.where(first, uim * kim, ure * kim + uim * kre).astype(BF16)

    y = pl.pallas_call(
        fwd_kern,
        grid=(ncb, B),
        in_specs=[pl.BlockSpec((n, Ls), lambda j, b: (0, 0), pipeline_mode=pl.Buffered(1)),
                  pl.BlockSpec((Ls, cb), lambda j, b: (row_blk0 + b, ncb + j)),
                  pl.BlockSpec((Ls, cb), lambda j, b: (row_blk0 + b, 2 * ncb + j)),
                  pl.BlockSpec((HY_SHORT, cb), lambda j, b: (0, ncb + j)),
                  pl.BlockSpec((1, cb), lambda j, b: (0, ncb + j)),
                  pl.BlockSpec((HY_SHORT, cb), lambda j, b: (0, 2 * ncb + j)),
                  pl.BlockSpec((1, cb), lambda j, b: (0, 2 * ncb + j)),
                  pl.BlockSpec((n, cb), lambda j, b: (0, j))],
        out_specs=pl.BlockSpec((None, n, cb), lambda j, b: (b, 0, j)),
        out_shape=jax.ShapeDtypeStruct((B, n, D), BF16),
        compiler_params=_params(("parallel", "arbitrary")),
        name="hyena_dft",
    )(fwd, z, z, conv_w, conv_b, conv_w, conv_b, kf)

    def inv_kern(w_ref, y_ref, x0_ref, cw0_ref, cb0_ref, o_ref):
        conv = jnp.dot(w_ref[...], y_ref[...], preferred_element_type=F32)
        o_ref[...] = (conv * _short_conv(x0_ref[...].astype(F32), cw0_ref[...], cb0_ref[...])).astype(BF16)

    return pl.pallas_call(
        inv_kern,
        grid=(ncb, B),
        in_specs=[pl.BlockSpec((Ls, n), lambda j, b: (0, 0), pipeline_mode=pl.Buffered(1)),
                  pl.BlockSpec((None, n, cb), lambda j, b: (b, 0, j)),
                  pl.BlockSpec((Ls, cb), lambda j, b: (row_blk0 + b, j)),
                  pl.BlockSpec((HY_SHORT, cb), lambda j, b: (0, j)),
                  pl.BlockSpec((1, cb), lambda j, b: (0, j))],
        out_specs=pl.BlockSpec((Ls, cb), lambda j, b: (b, j)),
        out_shape=jax.ShapeDtypeStruct((B * Ls, D), BF16),
        compiler_params=_params(("parallel", "arbitrary")),
        name="hyena_idft",
    )(inv, y, z, conv_w, conv_b)


def hyena_mixer(h, u, mod, layer, p, geo, need_ctx, norm):
    B, C, L, rows_all, rows_out = geo
    D = h_width(h)
    z = plain_matmul(u, p["w_in"].astype(BF16), rows_out, 1024, bias=p["b_in"], name="hyena_in")
    y = hyena_segment(z, p, L, 0, B, D)
    if need_ctx:
        y = (y, hyena_segment(z, p, C, B * L // C, B, D))
    return gated_matmul(y, p["w_out"].astype(BF16), h, mod, layer, 2, _out_rows(rows_out), bn=D, bias=p["b_out"],
                        norm=norm, name="hyena_out")


def kernel(x, c, ctx, c_ctx, ada_w, ada_b, norm_g, mlp_w1, mlp_w2, final_norm_g, mla_w_dq, mla_q_norm_g, mla_w_uq, mla_w_dkv, mla_kv_norm_g, mla_w_ukv, mla_w_o, hy_w_in, hy_b_in, hy_conv_w, hy_conv_b, hy_filt_w1, hy_filt_b1, hy_filt_w2, hy_filt_b2, hy_filt_w3, hy_filt_b3, hy_filt_freq, hy_filt_wout, hy_filt_bias, hy_w_out, hy_b_out, df_w_qkv, df_lambda, df_subln_g, df_w_o):
    B, L, D = x.shape
    C = ctx.shape[1]
    depth = ada_w.shape[0]
    bm = min(1024, B * C)
    rows_all = Rows(B, C, L, bm)
    rows_lat = Rows(B, C, L, bm, lat_only=True)

    mod = adaln_table(jnp.concatenate([c_ctx[None, :], c], axis=0), ada_w, ada_b)
    norm_g4 = norm_g.reshape(depth, 2, 1, D)
    w1s, w2s = mlp_w1.astype(BF16), mlp_w2.astype(BF16)
    h = (x.reshape(B * L, D), ctx.reshape(B * C, D))
    u = norm_modulate(h, norm_g4, mod, 0, 0, rows_all)

    for i in range(depth):
        need_ctx = i < depth - 1
        rows_out = rows_all if need_ctx else rows_lat
        geo = (B, C, L, rows_all, rows_out)
        kind, j = i % N_MIXERS, i // N_MIXERS
        mlp_norm = ("modulate", norm_g4, i, 1)
        if kind == 0:
            p = dict(w_dq=mla_w_dq[j], q_g=mla_q_norm_g[j], w_uq=mla_w_uq[j], w_dkv=mla_w_dkv[j],
                     kv_g=mla_kv_norm_g[j], w_ukv=mla_w_ukv[j], w_o=mla_w_o[j])
            h, u = mla_mixer(h, u, mod, i, p, geo, need_ctx, mlp_norm)
        elif kind == 1:
            p = dict(w_in=hy_w_in[j], b_in=hy_b_in[j], conv_w=hy_conv_w[j], conv_b=hy_conv_b[j],
                     f_w1=hy_filt_w1[j], f_b1=hy_filt_b1[j], f_w2=hy_filt_w2[j], f_b2=hy_filt_b2[j],
                     f_w3=hy_filt_w3[j], f_b3=hy_filt_b3[j], f_freq=hy_filt_freq[j], f_wout=hy_filt_wout[j],
                     f_bias=hy_filt_bias[j], w_out=hy_w_out[j], b_out=hy_b_out[j])
            h, u = hyena_mixer(h, u, mod, i, p, geo, need_ctx, mlp_norm)
        else:
            lambda_init = 0.8 - 0.6 * math.exp(-0.3 * i)
            p = dict(w_qkv=df_w_qkv[j], lambdas=df_lambda[j], subln_g=df_subln_g[j], w_o=df_w_o[j])
            h, u = diff_mixer(h, u, mod, i, p, geo, need_ctx, lambda_init, mlp_norm)
        next_norm = ("modulate", norm_g4, i + 1, 0) if need_ctx else ("final", final_norm_g)
        res = mlp(h, u, mod, i, (w1s, i), (w2s, i), rows_out, next_norm)
        if need_ctx:
            h, u = res

    return res.reshape(B, L, D)
```

```python
import functools
import math

import jax
import jax.numpy as jnp
import numpy as np
from jax import lax
from jax.experimental import pallas as pl
from jax.experimental.pallas import tpu as pltpu

F32 = jnp.float32
BF16 = jnp.bfloat16
HIGHEST = lax.Precision.HIGHEST
LOG2E = math.log2(math.e)

GRID_W = 64
ROPE_THETA = 10000.0
NORM_EPS = 1e-6
N_MIXERS = 3

MLA_NOPE = 128
MLA_ROPE = 64
MLA_V = 128
MLA_KV_RANK = 512
MLA_HEAD_PAD = 256
MLA_Q_TILE = 512

HY_SHORT = 3
HY_EMB_DIM = 33
HY_EMB_PAD = 64
HY_TARGET = 1e-2
HY_FAST_PCT = 0.3
HY_SLOW_PCT = 1.5

DF_HEAD_DIM = 128
DF_SUBLN_EPS = 1e-5
DF_Q_TILE = 512

LANES = 128
MXU_COLS = 256
VMEM_LIMIT_MB = 56


def _params(semantics, vmem_mb=VMEM_LIMIT_MB):
    return pltpu.CompilerParams(dimension_semantics=semantics, vmem_limit_bytes=vmem_mb << 20)


class Rows:
    def __init__(self, B, C, L, bm, lat_only=False):
        assert L % bm == 0 and (B * C) % bm == 0
        self.geometry = (B, C, L, lat_only)
        self.bm = bm
        self.tiles_per_batch = L // bm
        self.n_lat = B * L // bm
        self.n = self.n_lat + (0 if lat_only else B * C // bm)

    def with_bm(self, bm):
        B, C, L, lat_only = self.geometry
        return Rows(B, C, L, bm, lat_only)

    def mod_row(self, i):
        return jnp.where(i < self.n_lat, 1 + i // self.tiles_per_batch, 0)

    def pos_block(self, i):
        return jnp.where(i < self.n_lat, i % self.tiles_per_batch, self.tiles_per_batch)


def _mod_spec(layer, chunk, rows, bn, col_of):
    return pl.BlockSpec((None, None, None, 1, bn),
                        lambda *g: (layer, chunk, rows.mod_row(g[0]), 0, col_of(*g)))


def _w_shape(w):
    return w[0].shape[1:] if isinstance(w, tuple) else w.shape


def _row_operand(arr, rows, width, col_of):
    bm = rows.bm
    n_lat = rows.n_lat
    if isinstance(arr, tuple) and rows.n == n_lat:
        arr = arr[0]
    if not isinstance(arr, tuple):
        return [(arr, (bm, width), lambda *g: (g[0], col_of(*g)))], lambda refs: refs[0][...]
    specs = [(arr[0], (bm, width), lambda *g: (jnp.minimum(g[0], n_lat - 1), col_of(*g))),
             (arr[1], (bm, width), lambda *g: (jnp.maximum(g[0] - n_lat, 0), col_of(*g)))]
    return specs, lambda refs: jnp.where(pl.program_id(0) < n_lat, refs[0][...], refs[1][...])


def fused_matmul(a, w, extras, outs, epilogue, *, rows, bn, bk=None, name):
    K, N = _w_shape(w)
    bk = bk or K
    nk = K // bk
    assert K % bk == 0 and N % bn == 0
    a_specs, read_a = _row_operand(a, rows, bk, lambda i, j, k: k)
    n_a = len(a_specs)
    in_specs = [pl.BlockSpec(bs, im) for _, bs, im in a_specs]
    if isinstance(w, tuple):
        w, w_layer = w
        in_specs += [pl.BlockSpec((None, bk, bn), lambda i, j, k: (w_layer, k, j))]
    else:
        in_specs += [pl.BlockSpec((bk, bn), lambda i, j, k: (k, j))]
    in_specs += [pl.BlockSpec(bs, im) for _, bs, im in extras]
    n_ex = len(extras)
    n_out = len(outs)

    def kern(*refs):
        w_ref = refs[n_a]
        ex = refs[n_a + 1:n_a + 1 + n_ex]
        out = refs[n_a + 1 + n_ex:n_a + 1 + n_ex + n_out]

        def dot():
            return jnp.dot(read_a(refs[:n_a]), w_ref[...], preferred_element_type=F32)

        if nk == 1:
            epilogue(dot(), ex, out)
        else:
            acc_ref = out[0]
            k = pl.program_id(2)

            @pl.when(k == 0)
            def _():
                acc_ref[...] = dot()

            @pl.when((k > 0) & (k < nk - 1))
            def _():
                acc_ref[...] += dot()

            @pl.when(k == nk - 1)
            def _():
                epilogue(acc_ref[...] + dot(), ex, out)

    res = pl.pallas_call(
        kern,
        grid=(rows.n, N // bn, nk),
        in_specs=in_specs,
        out_specs=[pl.BlockSpec(bs, im) for _, _, bs, im in outs],
        out_shape=[jax.ShapeDtypeStruct(s, d) for s, d, _, _ in outs],
        compiler_params=_params(("parallel", "arbitrary", "arbitrary")),
        name=name,
    )(*[e[0] for e in a_specs], w, *[e[0] for e in extras])
    return res


def _rot_half(x, quarter):
    n = x.shape[-1]
    lane = lax.broadcasted_iota(jnp.int32, x.shape, x.ndim - 1)
    first = (lane % (2 * quarter)) < quarter
    return jnp.where(first, -pltpu.roll(x, n - quarter, x.ndim - 1), pltpu.roll(x, quarter, x.ndim - 1))


def _rms(x, g, eps):
    return x * lax.rsqrt(jnp.mean(x * x, axis=-1, keepdims=True) + eps) * g


def gated_matmul(a, w, resid, mod, layer, chunk, rows, *, name, bn=1024, bk=None, bias=None, norm=None):
    T = rows.n * rows.bm
    bm = rows.bm
    N = _w_shape(w)[1]
    res_specs, read_res = _row_operand(resid, rows, bn, lambda i, j, k: j)
    n_res = len(res_specs)
    extras = res_specs + [(mod,) + _spec_parts(_mod_spec(layer, chunk, rows, bn, lambda i, j, k: j))]
    if bias is not None:
        extras.append((bias.reshape(1, N), (1, bn), lambda i, j, k: (0, j)))
    n_fix = len(extras)
    tile = lambda i, j, k: (i, j)
    outs = [((T, N), F32, (bm, bn), tile)]
    if norm is not None:
        assert bn == N
        if norm[0] == "modulate":
            _, g4, nl, which = norm
            extras += [(g4, (None, None, 1, N), lambda i, j, k: (nl, which, 0, 0)),
                       (mod,) + _spec_parts(_mod_spec(nl, 3 * which, rows, N, lambda i, j, k: 0)),
                       (mod,) + _spec_parts(_mod_spec(nl, 3 * which + 1, rows, N, lambda i, j, k: 0))]
            outs.append(((T, N), BF16, (bm, bn), tile))
        else:
            extras.append((norm[1].reshape(1, N), (1, N), lambda i, j, k: (0, 0)))

    def epi(acc, ex, out):
        y = acc if bias is None else acc + ex[n_res + 1][...]
        hn = read_res(ex[:n_res]) + ex[n_res][...] * y
        if norm is None:
            out[0][...] = hn
        elif norm[0] == "modulate":
            g_ref, sh_ref, sc_ref = ex[n_fix:]
            out[0][...] = hn
            out[1][...] = (_rms(hn, g_ref[...], NORM_EPS) * (1.0 + sc_ref[...]) + sh_ref[...]).astype(BF16)
        else:
            out[0][...] = _rms(hn, ex[n_fix][...], NORM_EPS)

    res = fused_matmul(a, w, extras, outs, epi, rows=rows, bn=bn, bk=bk, name=name)
    return res if len(res) > 1 else res[0]


def _spec_parts(spec):
    return spec.block_shape, spec.index_map


def h_width(h):
    return (h[0] if isinstance(h, tuple) else h).shape[1]


def _out_rows(rows):
    return rows.with_bm(min(512, rows.bm))


def adaln_table(cvec, ada_w, ada_b):
    depth, D, _ = ada_w.shape
    R = cvec.shape[0]

    def kern(c_ref, w_ref, b_ref, o_ref):
        c = c_ref[...]
        s = c / (1.0 + jnp.exp(-c))
        o_ref[...] = jnp.dot(s, w_ref[...], precision=HIGHEST, preferred_element_type=F32) + b_ref[...]

    out = pl.pallas_call(
        kern,
        grid=(depth, 6),
        in_specs=[pl.BlockSpec((R, D), lambda l, j: (0, 0)),
                  pl.BlockSpec((None, D, D), lambda l, j: (l, 0, j)),
                  pl.BlockSpec((None, 1, D), lambda l, j: (l, 0, j))],
        out_specs=pl.BlockSpec((None, None, R, D), lambda l, j: (l, j, 0, 0)),
        out_shape=jax.ShapeDtypeStruct((depth, 6, R, D), F32),
        compiler_params=_params(("parallel", "arbitrary")),
        name="adaln_table",
    )(cvec, ada_w, ada_b.reshape(depth, 1, 6 * D))
    return out.reshape(depth, 6, R, 1, D)


def norm_modulate(h, norm_g4, mod, layer, which, rows):
    D = norm_g4.shape[-1]
    bm = rows.bm
    h_specs, read_h = _row_operand(h, rows, D, lambda i: 0)
    n_h = len(h_specs)

    def kern(*refs):
        g_ref, sh_ref, sc_ref, o_ref = refs[n_h:]
        y = _rms(read_h(refs[:n_h]), g_ref[...], NORM_EPS)
        o_ref[...] = (y * (1.0 + sc_ref[...]) + sh_ref[...]).astype(BF16)

    return pl.pallas_call(
        kern,
        grid=(rows.n,),
        in_specs=[pl.BlockSpec(bs, im) for _, bs, im in h_specs]
        + [pl.BlockSpec((None, None, 1, D), lambda i: (layer, which, 0, 0)),
           _mod_spec(layer, 3 * which, rows, D, lambda i: 0),
           _mod_spec(layer, 3 * which + 1, rows, D, lambda i: 0)],
        out_specs=pl.BlockSpec((bm, D), lambda i: (i, 0)),
        out_shape=jax.ShapeDtypeStruct((rows.n * bm, D), BF16),
        compiler_params=_params(("parallel",)),
        name="norm_modulate",
    )(*[e[0] for e in h_specs], norm_g4, mod, mod)


def mlp(h, u, mod, layer, w1, w2, rows, norm):
    T = rows.n * rows.bm
    F, D = _w_shape(w2)

    def relu2(acc, ex, out):
        r = jnp.maximum(acc, 0.0)
        out[0][...] = (r * r).astype(BF16)

    hid = fused_matmul(u, w1, [], [((T, F), BF16, (rows.bm, 1024), lambda i, j, k: (i, j))], relu2,
                       rows=rows, bn=1024, name="mlp_up")[0]
    return gated_matmul(hid, w2, h, mod, layer, 5, rows, bn=D, bk=1024, norm=norm, name="mlp_down")


def axial_rope_tables(L, rot_dim):
    rows = L // GRID_W
    row = jnp.repeat(jnp.arange(rows, dtype=F32), GRID_W)
    col = jnp.tile(jnp.arange(GRID_W, dtype=F32), rows)
    pos = jnp.stack([row, col], axis=-1)
    n_freq = rot_dim // 4
    inv_freq = ROPE_THETA ** (-jnp.arange(n_freq, dtype=F32) / n_freq)
    ang = pos[:, :, None, None] * inv_freq
    ang = jnp.broadcast_to(ang, (L, 2, 2, n_freq)).reshape(L, rot_dim)
    return jnp.cos(ang), jnp.sin(ang)


def rope_tables_padded(L, rot_dim, bm):
    cos, sin = axial_rope_tables(L, rot_dim)
    cos = jnp.pad(cos, ((0, bm), (0, LANES - rot_dim)), constant_values=1.0)
    sin = jnp.pad(sin, ((0, bm), (0, LANES - rot_dim)))
    return cos, sin


def mla_down(u, w_dq, q_g, w_dkv, kv_g, cos, sin, rows):
    T = rows.n * rows.bm
    bm = rows.bm
    qr = w_dq.shape[1]
    wd = jnp.concatenate([w_dq, w_dkv, jnp.zeros((w_dq.shape[0], LANES - MLA_ROPE), w_dq.dtype)],
                         axis=1).astype(BF16)
    n_all = wd.shape[1]
    c0 = qr + MLA_KV_RANK

    def epi(acc, ex, out):
        qg_ref, kvg_ref, cos_ref, sin_ref = ex
        out[0][...] = _rms(acc[:, :qr], qg_ref[...], NORM_EPS).astype(BF16)
        out[1][...] = _rms(acc[:, qr:c0], kvg_ref[...], NORM_EPS).astype(BF16)
        kr = acc[:, c0:]
        out[2][...] = (kr * cos_ref[...] + _rot_half(kr, MLA_ROPE // 4) * sin_ref[...]).astype(BF16)

    extras = [(q_g.reshape(1, qr), (1, qr), lambda i, j, k: (0, 0)),
              (kv_g.reshape(1, MLA_KV_RANK), (1, MLA_KV_RANK), lambda i, j, k: (0, 0)),
              (cos, (bm, LANES), lambda i, j, k: (rows.pos_block(i), 0)),
              (sin, (bm, LANES), lambda i, j, k: (rows.pos_block(i), 0))]
    outs = [((T, qr), BF16, (bm, qr), lambda i, j, k: (i, 0)),
            ((T, MLA_KV_RANK), BF16, (bm, MLA_KV_RANK), lambda i, j, k: (i, 0)),
            ((T, LANES), BF16, (bm, LANES), lambda i, j, k: (i, 0))]
    return fused_matmul(u, wd, extras, outs, epi, rows=rows, bn=n_all, name="mla_down")


def mla_queries(cq, w_uq, cos, sin, rows, heads):
    T = rows.n * rows.bm
    bm = rows.bm
    qr = w_uq.shape[0]
    hd = MLA_NOPE + MLA_ROPE
    w = w_uq.reshape(qr, heads, hd)
    w = jnp.pad(w, ((0, 0), (0, 0), (0, MLA_HEAD_PAD - hd))).reshape(qr, heads * MLA_HEAD_PAD).astype(BF16)
    scale = hd ** -0.5 * LOG2E
    bn = 2048

    def epi(acc, ex, out):
        cos_ref, sin_ref = ex
        for hh in range(bn // MLA_HEAD_PAD):
            c = hh * MLA_HEAD_PAD
            out[0][:, c:c + LANES] = (acc[:, c:c + LANES] * scale).astype(BF16)
            r = acc[:, c + LANES:c + 2 * LANES]
            r = r * cos_ref[...] + _rot_half(r, MLA_ROPE // 4) * sin_ref[...]
            out[0][:, c + LANES:c + 2 * LANES] = (r * scale).astype(BF16)

    extras = [(cos, (bm, LANES), lambda i, j, k: (rows.pos_block(i), 0)),
              (sin, (bm, LANES), lambda i, j, k: (rows.pos_block(i), 0))]
    outs = [((T, heads * MLA_HEAD_PAD), BF16, (bm, bn), lambda i, j, k: (i, j))]
    return fused_matmul(cq, w, extras, outs, epi, rows=rows, bn=bn, name="mla_queries")[0]


def plain_matmul(a, w, rows, bn, *, name, out_dtype=BF16, bias=None):
    T = rows.n * rows.bm
    N = w.shape[1]
    extras = [] if bias is None else [(bias.reshape(1, N), (1, bn), lambda i, j, k: (0, j))]

    def epi(acc, ex, out):
        y = acc if bias is None else acc + ex[0][...]
        out[0][...] = y.astype(out_dtype)

    return fused_matmul(a, w, extras, [((T, N), out_dtype, (rows.bm, bn), lambda i, j, k: (i, j))], epi,
                        rows=rows, bn=bn, name=name)[0]


def mla_attention(q, kv, kr, B, C, L, heads, ctx_queries):
    hp = 4
    tq = C if ctx_queries else MLA_Q_TILE
    n_t = 1 if ctx_queries else L // tq
    ctx_blk = B * L // C
    S = C if ctx_queries else C + L
    kvw = 2 * LANES
    q_row = (lambda b, g, t: ctx_blk + b) if ctx_queries else (lambda b, g, t: b * n_t + t)

    def kern(*refs):
        if ctx_queries:
            q_ref, kvc_ref, krc_ref, o_ref, k_s, v_s = refs
            parts = [(kvc_ref, krc_ref, 0, C)]
        else:
            q_ref, kvl_ref, kvc_ref, krl_ref, krc_ref, o_ref, k_s, v_s = refs
            parts = [(kvc_ref, krc_ref, 0, C), (kvl_ref, krl_ref, C, S)]

        @pl.when(pl.program_id(2) == 0)
        def _():
            for hh in range(hp):
                c0 = hh * kvw
                for kv_ref, kr_ref, lo, hi in parts:
                    k_s[hh, lo:hi, :LANES] = kv_ref[:, c0:c0 + LANES]
                    k_s[hh, lo:hi, LANES:] = kr_ref[...]
                    v_s[hh, lo:hi, :MLA_V] = kv_ref[:, c0 + LANES:c0 + kvw]
                v_s[hh, :, MLA_V:] = jnp.ones((S, MXU_COLS - MLA_V), BF16)

        ss = [lax.dot_general(q_ref[:, hh * MLA_HEAD_PAD:(hh + 1) * MLA_HEAD_PAD], k_s[hh],
                              (((1,), (1,)), ((), ())), preferred_element_type=F32) for hh in range(hp)]
        ps = [jnp.exp2(s - jnp.max(s, axis=-1, keepdims=True)).astype(BF16) for s in ss]
        for hh in range(hp):
            ov = jnp.dot(ps[hh], v_s[hh], preferred_element_type=F32)
            o_ref[:, hh * MLA_V:(hh + 1) * MLA_V] = (ov[:, :MLA_V] / ov[:, MLA_V:2 * MLA_V]).astype(BF16)

    lat_specs = [pl.BlockSpec((L, hp * kvw), lambda b, g, t: (b, g)),
                 pl.BlockSpec((L, LANES), lambda b, g, t: (b, 0))]
    ctx_specs = [pl.BlockSpec((C, hp * kvw), lambda b, g, t: (ctx_blk + b, g)),
                 pl.BlockSpec((C, LANES), lambda b, g, t: (ctx_blk + b, 0))]
    q_spec = pl.BlockSpec((tq, hp * MLA_HEAD_PAD), lambda b, g, t: (q_row(b, g, t), g))
    if ctx_queries:
        in_specs, args = [q_spec] + ctx_specs, (q, kv, kr)
    else:
        in_specs, args = [q_spec, lat_specs[0], ctx_specs[0], lat_specs[1], ctx_specs[1]], (q, kv, kv, kr, kr)
    return pl.pallas_call(
        kern,
        grid=(B, heads // hp, n_t),
        in_specs=in_specs,
        out_specs=pl.BlockSpec((tq, hp * MLA_V), lambda b, g, t: (b * n_t + t, g)),
        out_shape=jax.ShapeDtypeStruct((B * n_t * tq, heads * MLA_V), BF16),
        scratch_shapes=[pltpu.VMEM((hp, S, MLA_HEAD_PAD), BF16), pltpu.VMEM((hp, S, MXU_COLS), BF16)],
        compiler_params=_params(("parallel", "parallel", "arbitrary")),
        name="mla_attention_ctx" if ctx_queries else "mla_attention",
    )(*args)


def mla_mixer(h, u, mod, layer, p, geo, need_ctx, norm):
    B, C, L, rows_all, rows_out = geo
    heads = p["w_uq"].shape[1] // (MLA_NOPE + MLA_ROPE)
    cos, sin = rope_tables_padded(L, MLA_ROPE, rows_all.bm)
    cq, ckv, kr = mla_down(u, p["w_dq"], p["q_g"], p["w_dkv"], p["kv_g"], cos, sin, rows_all)
    kv = plain_matmul(ckv, p["w_ukv"].astype(BF16), rows_all, p["w_ukv"].shape[1], name="mla_kv")
    q = mla_queries(cq, p["w_uq"], cos, sin, rows_out, heads)
    o = mla_attention(q, kv, kr, B, C, L, heads, False)
    if need_ctx:
        o = (o, mla_attention(q, kv, kr, B, C, L, heads, True))
    return gated_matmul(o, p["w_o"].astype(BF16), h, mod, layer, 2, _out_rows(rows_out), bn=h_width(h), norm=norm,
                        name="mla_out")


def _diff_pair_layout():
    quarter = DF_HEAD_DIM // 4
    n = np.arange(2 * DF_HEAD_DIM)
    hf, c, a, r = n // DF_HEAD_DIM, (n % DF_HEAD_DIM) // (2 * quarter), (n % (2 * quarter)) // quarter, n % quarter
    head_perm = c * DF_HEAD_DIM + a * 2 * quarter + hf * quarter + r
    table_cols = (a * 2 * quarter + r)[:DF_HEAD_DIM]
    return head_perm, table_cols


def diff_rope_tables(L, bm):
    cos, sin = axial_rope_tables(L, DF_HEAD_DIM)
    _, cols = _diff_pair_layout()
    cos = jnp.pad(cos[:, cols], ((0, bm), (0, 0)), constant_values=1.0)
    sin = jnp.pad(sin[:, cols], ((0, bm), (0, 0)))
    return cos, sin


def diff_qkv(u, w_qkv, cos, sin, rows):
    T = rows.n * rows.bm
    bm = rows.bm
    D = w_qkv.shape[0]
    bn = 1024
    hw = 2 * DF_HEAD_DIM
    scale = DF_HEAD_DIM ** -0.5 * LOG2E
    n_q = D // bn
    head_perm, _ = _diff_pair_layout()
    qk_cols = (np.arange(2 * D) // hw * hw)[:, None].reshape(-1, hw) + head_perm[None, :]
    cols = np.concatenate([qk_cols.reshape(-1), np.arange(2 * D, 3 * D)])
    w = w_qkv[:, cols].astype(BF16)

    def epi(acc, ex, out):
        cos_ref, sin_ref = ex
        mul = jnp.where(pl.program_id(1) < n_q, scale, 1.0).astype(F32)
        cs, sn = cos_ref[...] * mul, sin_ref[...] * mul
        for s in range(bn // hw):
            x1 = acc[:, s * hw:s * hw + LANES]
            x2 = acc[:, s * hw + LANES:(s + 1) * hw]
            out[0][:, s * hw:s * hw + LANES] = (x1 * cs - x2 * sn).astype(BF16)
            out[0][:, s * hw + LANES:(s + 1) * hw] = (x2 * cs + x1 * sn).astype(BF16)

    def table_block(i, j, k):
        return jnp.where(j < 2 * n_q, rows.pos_block(i), rows.tiles_per_batch), 0

    extras = [(cos, (bm, LANES), table_block), (sin, (bm, LANES), table_block)]
    outs = [((T, 3 * D), BF16, (bm, bn), lambda i, j, k: (i, j))]
    return fused_matmul(u, w, extras, outs, epi, rows=rows, bn=bn, name="diff_qkv")[0]


def diff_attention(qkv, lambdas, subln_g, lambda_init, B, C, L, D, ctx_queries):
    hd = DF_HEAD_DIM
    hw = 2 * hd
    heads = D // hw
    hp = 2
    tq = C if ctx_queries else DF_Q_TILE
    n_t = 1 if ctx_queries else L // tq
    ctx_blk = B * L // C
    S = C if ctx_queries else C + L
    kcol = D // (hp * hw)
    vcol = 2 * D // (hp * hw)
    q_row = (lambda b, g, t: ctx_blk + b) if ctx_queries else (lambda b, g, t: b * n_t + t)

    def kern(*refs):
        if ctx_queries:
            q_ref, kc_ref, vc_ref, lam_ref, g_ref, o_ref, k_s, v_s = refs
            parts = [(kc_ref, vc_ref, 0, C)]
        else:
            q_ref, kl_ref, kc_ref, vl_ref, vc_ref, lam_ref, g_ref, o_ref, k_s, v_s = refs
            parts = [(kc_ref, vc_ref, 0, C), (kl_ref, vl_ref, C, S)]

        @pl.when(pl.program_id(2) == 0)
        def _():
            map0 = (lax.broadcasted_iota(jnp.int32, (1, hw), 1) % hd) < hd // 2
            for k_ref, v_ref, lo, hi in parts:
                for hh in range(hp):
                    k = k_ref[:, hh * hw:(hh + 1) * hw]
                    k_s[2 * hh, lo:hi, :] = jnp.where(map0, k, jnp.zeros_like(k))
                    k_s[2 * hh + 1, lo:hi, :] = jnp.where(map0, jnp.zeros_like(k), k)
                    v_s[hh, lo:hi, :] = v_ref[:, hh * hw:(hh + 1) * hw]

        lf = lam_ref[...]
        lam = (jnp.exp(jnp.sum(lf[0:1] * lf[1:2], axis=-1, keepdims=True))
               - jnp.exp(jnp.sum(lf[2:3] * lf[3:4], axis=-1, keepdims=True)) + lambda_init)

        nc = 2 * hp
        ss = [lax.dot_general(q_ref[:, (c // 2) * hw:(c // 2 + 1) * hw], k_s[c],
                              (((1,), (1,)), ((), ())), preferred_element_type=F32) for c in range(nc)]
        ms = [jnp.max(s, axis=-1, keepdims=True) for s in ss]
        os_, ls = [0.0] * nc, [0.0] * nc
        for lo in range(0, S, MXU_COLS):
            for c in range(nc):
                p = jnp.exp2(ss[c][:, lo:lo + MXU_COLS] - ms[c])
                ls[c] = ls[c] + jnp.sum(p[:, :LANES] + p[:, LANES:], axis=-1, keepdims=True)
                os_[c] = os_[c] + jnp.dot(p.astype(BF16), v_s[c // 2, lo:lo + MXU_COLS, :],
                                          preferred_element_type=F32)
        for hh in range(hp):
            o = os_[2 * hh] / ls[2 * hh] - lam * (os_[2 * hh + 1] / ls[2 * hh + 1])
            o_ref[:, hh * hw:(hh + 1) * hw] = (_rms(o, g_ref[...], DF_SUBLN_EPS)
                                               * (1.0 - lambda_init)).astype(BF16)

    bw = hp * hw
    q_spec = pl.BlockSpec((tq, bw), lambda b, g, t: (q_row(b, g, t), g))
    lat_specs = [pl.BlockSpec((L, bw), lambda b, g, t: (b, kcol + g)),
                 pl.BlockSpec((L, bw), lambda b, g, t: (b, vcol + g))]
    ctx_specs = [pl.BlockSpec((C, bw), lambda b, g, t: (ctx_blk + b, kcol + g)),
                 pl.BlockSpec((C, bw), lambda b, g, t: (ctx_blk + b, vcol + g))]
    par_specs = [pl.BlockSpec((4, hd), lambda b, g, t: (0, 0)), pl.BlockSpec((1, hw), lambda b, g, t: (0, 0))]
    if ctx_queries:
        in_specs, args = [q_spec] + ctx_specs, (qkv, qkv, qkv)
    else:
        in_specs = [q_spec, lat_specs[0], ctx_specs[0], lat_specs[1], ctx_specs[1]]
        args = (qkv, qkv, qkv, qkv, qkv)
    return pl.pallas_call(
        kern,
        grid=(B, heads // hp, n_t),
        in_specs=in_specs + par_specs,
        out_specs=pl.BlockSpec((tq, bw), lambda b, g, t: (b * n_t + t, g)),
        out_shape=jax.ShapeDtypeStruct((B * n_t * tq, D), BF16),
        scratch_shapes=[pltpu.VMEM((2 * hp, S, hw), BF16), pltpu.VMEM((hp, S, hw), BF16)],
        compiler_params=_params(("parallel", "parallel", "arbitrary")),
        name="diff_attention_ctx" if ctx_queries else "diff_attention",
    )(*args, lambdas, subln_g.reshape(1, hw))


def diff_mixer(h, u, mod, layer, p, geo, need_ctx, lambda_init, norm):
    B, C, L, rows_all, rows_out = geo
    D = h_width(h)
    cos, sin = diff_rope_tables(L, rows_all.bm)
    qkv = diff_qkv(u, p["w_qkv"], cos, sin, rows_all)
    o = diff_attention(qkv, p["lambdas"], p["subln_g"], lambda_init, B, C, L, D, False)
    if need_ctx:
        o = (o, diff_attention(qkv, p["lambdas"], p["subln_g"], lambda_init, B, C, L, D, True))
    return gated_matmul(o, p["w_o"].astype(BF16), h, mod, layer, 2, _out_rows(rows_out), bn=D, norm=norm,
                        name="diff_out")


def dft_matrices(Ls):
    n = 2 * Ls
    t0n = min(64, Ls)
    t1n = Ls // t0n
    f = jnp.arange(Ls, dtype=jnp.int32)[:, None]
    a1 = ((f * (jnp.arange(t1n, dtype=jnp.int32) * t0n)[None, :]) % n).astype(F32) * (2.0 * math.pi / n)
    a0 = ((f * jnp.arange(t0n, dtype=jnp.int32)[None, :]) % n).astype(F32) * (2.0 * math.pi / n)
    c1, s1 = jnp.cos(a1)[:, :, None], jnp.sin(a1)[:, :, None]
    c0, s0 = jnp.cos(a0)[:, None, :], jnp.sin(a0)[:, None, :]
    cosm = (c1 * c0 - s1 * s0).reshape(Ls, Ls)
    sinm = (s1 * c0 + c1 * s0).reshape(Ls, Ls)
    nyq = jnp.where(jnp.arange(Ls) % 2 == 0, 1.0, -1.0).astype(F32)[None, :]
    imag = jnp.where(f == 0, nyq, -sinm)
    fwd = jnp.concatenate([cosm, imag], axis=0)
    col = jnp.arange(n)
    cscale = jnp.where((col == 0) | (col == Ls), 1.0 / n, 2.0 / n).astype(F32)
    inv = fwd.T * cscale[None, :]
    return fwd.astype(BF16), inv.astype(BF16)


def hyena_filter_time(Ls, p):
    D = p["f_bias"].shape[0]
    order = p["f_w2"].shape[0]
    bands = (HY_EMB_DIM - 1) // 2
    t = jnp.linspace(0.0, 1.0, Ls, dtype=F32)[:, None]
    w = 2.0 * math.pi * jnp.arange(Ls, dtype=F32)[:, None] / Ls
    f = jnp.linspace(1e-4, bands - 1, bands, dtype=F32)
    feats = jnp.concatenate([t, jnp.cos(f * w), -jnp.sin(f * w)], axis=-1)
    feats = jnp.pad(feats, ((0, 0), (0, HY_EMB_PAD - HY_EMB_DIM)))
    w1 = jnp.pad(p["f_w1"], ((0, HY_EMB_PAD - HY_EMB_DIM), (0, 0)))
    deltas = jnp.abs(jnp.linspace(math.log(HY_TARGET) / HY_SLOW_PCT, math.log(HY_TARGET) / HY_FAST_PCT,
                                  D, dtype=F32))
    deltas2 = jnp.concatenate([deltas, deltas]).reshape(1, 2 * D)

    def ffn_kern(x_ref, w1_ref, b1_ref, w2_ref, b2_ref, w3_ref, b3_ref, fr_ref, o_ref):
        dot = functools.partial(jnp.dot, precision=HIGHEST, preferred_element_type=F32)
        fr = fr_ref[...]
        hcur = jnp.sin(fr[0:1] * (dot(x_ref[...], w1_ref[...]) + b1_ref[...]))
        hcur = jnp.sin(fr[1:2] * (dot(hcur, w2_ref[...]) + b2_ref[...]))
        o_ref[...] = jnp.sin(fr[2:3] * (dot(hcur, w3_ref[...]) + b3_ref[...]))

    full = lambda a: pl.BlockSpec(a.shape, lambda: (0,) * a.ndim)
    ffn_in = [feats, w1, p["f_b1"].reshape(1, order), p["f_w2"], p["f_b2"].reshape(1, order),
              p["f_w3"], p["f_b3"].reshape(1, order), p["f_freq"]]
    hff = pl.pallas_call(
        ffn_kern,
        in_specs=[full(a) for a in ffn_in],
        out_specs=pl.BlockSpec((Ls, order), lambda: (0, 0)),
        out_shape=jax.ShapeDtypeStruct((Ls, order), F32),
        name="hyena_filter_ffn",
    )(*ffn_in)

    bn = 512

    def out_kern(h_ref, w_ref, d_ref, o_ref):
        tt = lax.broadcasted_iota(jnp.int32, (Ls, 1), 0).astype(F32) * (1.0 / (Ls - 1))
        hw = jnp.dot(h_ref[...], w_ref[...], precision=HIGHEST, preferred_element_type=F32)
        o_ref[...] = hw * jnp.exp(-tt * d_ref[...])

    return pl.pallas_call(
        out_kern,
        grid=(2 * D // bn,),
        in_specs=[pl.BlockSpec((Ls, order), lambda j: (0, 0)),
                  pl.BlockSpec((order, bn), lambda j: (0, j)),
                  pl.BlockSpec((1, bn), lambda j: (0, j))],
        out_specs=pl.BlockSpec((Ls, bn), lambda j: (0, j)),
        out_shape=jax.ShapeDtypeStruct((Ls, 2 * D), F32),
        compiler_params=_params(("parallel",)),
        name="hyena_filter_out",
    )(hff, p["f_wout"], deltas2)


def hyena_filter_spectrum(hfb, f_bias, fwd, Ls, D):
    n = 2 * Ls
    cb = LANES
    ncb = D // cb

    def kern(w_ref, hf_ref, hb_ref, fb_ref, o_ref):
        row = lax.broadcasted_iota(jnp.int32, (Ls, 1), 0)
        hb = jnp.where(row == 0, 0.0, hb_ref[...])
        ab = jnp.dot(w_ref[...], jnp.concatenate([hf_ref[...], hb], axis=1).astype(BF16),
                     preferred_element_type=F32)
        a, b = ab[:, :cb], ab[:, cb:]
        frow = lax.broadcasted_iota(jnp.int32, (n, 1), 0)
        real = frow <= Ls
        o_ref[...] = a + jnp.where(real, b + fb_ref[...], -b)

    return pl.pallas_call(
        kern,
        grid=(ncb,),
        in_specs=[pl.BlockSpec((n, Ls), lambda j: (0, 0), pipeline_mode=pl.Buffered(1)),
                  pl.BlockSpec((Ls, cb), lambda j: (0, j)),
                  pl.BlockSpec((Ls, cb), lambda j: (0, ncb + j)),
                  pl.BlockSpec((1, cb), lambda j: (0, j))],
        out_specs=pl.BlockSpec((n, cb), lambda j: (0, j)),
        out_shape=jax.ShapeDtypeStruct((n, D), F32),
        compiler_params=_params(("parallel",)),
        name="hyena_filter_spectrum",
    )(fwd, hfb, hfb, f_bias.reshape(1, D))


def _short_conv(x, w, b):
    n = x.shape[0]
    row = lax.broadcasted_iota(jnp.int32, (n, 1), 0)
    prev = jnp.where(row == 0, 0.0, pltpu.roll(x, 1, 0))
    nxt = jnp.where(row == n - 1, 0.0, pltpu.roll(x, n - 1, 0))
    return prev * w[0:1] + x * w[1:2] + nxt * w[2:3] + b


def hyena_segment(z, p, Ls, row_blk0, B, D):
    n = 2 * Ls
    cb = MXU_COLS
    ncb = D // cb
    fwd, inv = dft_matrices(Ls)
    hfb = hyena_filter_time(Ls, p)
    kf = hyena_filter_spectrum(hfb, p["f_bias"], fwd, Ls, D)
    conv_w, conv_b = p["conv_w"], p["conv_b"].reshape(1, 3 * D)

    def fwd_kern(w_ref, x1_ref, v_ref, cw1_ref, cb1_ref, cwv_ref, cbv_ref, kf_ref, y_ref):
        g = (_short_conv(v_ref[...].astype(F32), cwv_ref[...], cbv_ref[...])
             * _short_conv(x1_ref[...].astype(F32), cw1_ref[...], cb1_ref[...])).astype(BF16)
        u = jnp.dot(w_ref[...], g, preferred_element_type=F32)
        ure, uim = u[:Ls], u[Ls:]
        kre, kim = kf_ref[:Ls, :], kf_ref[Ls:, :]
        first = lax.broadcasted_iota(jnp.int32, (Ls, 1), 0) == 0
        y_ref[:Ls, :] = (ure * kre - jnp.where(first, 0.0, uim * kim)).astype(BF16)
        y_ref[Ls:, :] = jnp.where(first, uim * kim, ure * kim + uim * kre).astype(BF16)

    y = pl.pallas_call(
        fwd_kern,
        grid=(ncb, B),
        in_specs=[pl.BlockSpec((n, Ls), lambda j, b: (0, 0), pipeline_mode=pl.Buffered(1)),
                  pl.BlockSpec((Ls, cb), lambda j, b: (row_blk0 + b, ncb + j)),
                  pl.BlockSpec((Ls, cb), lambda j, b: (row_blk0 + b, 2 * ncb + j)),
                  pl.BlockSpec((HY_SHORT, cb), lambda j, b: (0, ncb + j)),
                  pl.BlockSpec((1, cb), lambda j, b: (0, ncb + j)),
                  pl.BlockSpec((HY_SHORT, cb), lambda j, b: (0, 2 * ncb + j)),
                  pl.BlockSpec((1, cb), lambda j, b: (0, 2 * ncb + j)),
                  pl.BlockSpec((n, cb), lambda j, b: (0, j))],
        out_specs=pl.BlockSpec((None, n, cb), lambda j, b: (b, 0, j)),
        out_shape=jax.ShapeDtypeStruct((B, n, D), BF16),
        compiler_params=_params(("parallel", "arbitrary")),
        name="hyena_dft",
    )(fwd, z, z, conv_w, conv_b, conv_w, conv_b, kf)

    def inv_kern(w_ref, y_ref, x0_ref, cw0_ref, cb0_ref, o_ref):
        conv = jnp.dot(w_ref[...], y_ref[...], preferred_element_type=F32)
        o_ref[...] = (conv * _short_conv(x0_ref[...].astype(F32), cw0_ref[...], cb0_ref[...])).astype(BF16)

    return pl.pallas_call(
        inv_kern,
        grid=(ncb, B),
        in_specs=[pl.BlockSpec((Ls, n), lambda j, b: (0, 0), pipeline_mode=pl.Buffered(1)),
                  pl.BlockSpec((None, n, cb), lambda j, b: (b, 0, j)),
                  pl.BlockSpec((Ls, cb), lambda j, b: (row_blk0 + b, j)),
                  pl.BlockSpec((HY_SHORT, cb), lambda j, b: (0, j)),
                  pl.BlockSpec((1, cb), lambda j, b: (0, j))],
        out_specs=pl.BlockSpec((Ls, cb), lambda j, b: (b, j)),
        out_shape=jax.ShapeDtypeStruct((B * Ls, D), BF16),
        compiler_params=_params(("parallel", "arbitrary")),
        name="hyena_idft",
    )(inv, y, z, conv_w, conv_b)


def hyena_mixer(h, u, mod, layer, p, geo, need_ctx, norm):
    B, C, L, rows_all, rows_out = geo
    D = h_width(h)
    z = plain_matmul(u, p["w_in"].astype(BF16), rows_out, 1024, bias=p["b_in"], name="hyena_in")
    y = hyena_segment(z, p, L, 0, B, D)
    if need_ctx:
        y = (y, hyena_segment(z, p, C, B * L // C, B, D))
    return gated_matmul(y, p["w_out"].astype(BF16), h, mod, layer, 2, _out_rows(rows_out), bn=D, bias=p["b_out"],
                        norm=norm, name="hyena_out")


def kernel(x, c, ctx, c_ctx, ada_w, ada_b, norm_g, mlp_w1, mlp_w2, final_norm_g, mla_w_dq, mla_q_norm_g, mla_w_uq, mla_w_dkv, mla_kv_norm_g, mla_w_ukv, mla_w_o, hy_w_in, hy_b_in, hy_conv_w, hy_conv_b, hy_filt_w1, hy_filt_b1, hy_filt_w2, hy_filt_b2, hy_filt_w3, hy_filt_b3, hy_filt_freq, hy_filt_wout, hy_filt_bias, hy_w_out, hy_b_out, df_w_qkv, df_lambda, df_subln_g, df_w_o):
    B, L, D = x.shape
    C = ctx.shape[1]
    depth = ada_w.shape[0]
    bm = min(1024, B * C)
    rows_all = Rows(B, C, L, bm)
    rows_lat = Rows(B, C, L, bm, lat_only=True)

    mod = adaln_table(jnp.concatenate([c_ctx[None, :], c], axis=0), ada_w, ada_b)
    norm_g4 = norm_g.reshape(depth, 2, 1, D)
    w1s, w2s = mlp_w1.astype(BF16), mlp_w2.astype(BF16)
    h = (x.reshape(B * L, D), ctx.reshape(B * C, D))
    u = norm_modulate(h, norm_g4, mod, 0, 0, rows_all)

    for i in range(depth):
        need_ctx = i < depth - 1
        rows_out = rows_all if need_ctx else rows_lat
        geo = (B, C, L, rows_all, rows_out)
        kind, j = i % N_MIXERS, i // N_MIXERS
        mlp_norm = ("modulate", norm_g4, i, 1)
        if kind == 0:
            p = dict(w_dq=mla_w_dq[j], q_g=mla_q_norm_g[j], w_uq=mla_w_uq[j], w_dkv=mla_w_dkv[j],
                     kv_g=mla_kv_norm_g[j], w_ukv=mla_w_ukv[j], w_o=mla_w_o[j])
            h, u = mla_mixer(h, u, mod, i, p, geo, need_ctx, mlp_norm)
        elif kind == 1:
            p = dict(w_in=hy_w_in[j], b_in=hy_b_in[j], conv_w=hy_conv_w[j], conv_b=hy_conv_b[j],
                     f_w1=hy_filt_w1[j], f_b1=hy_filt_b1[j], f_w2=hy_filt_w2[j], f_b2=hy_filt_b2[j],
                     f_w3=hy_filt_w3[j], f_b3=hy_filt_b3[j], f_freq=hy_filt_freq[j], f_wout=hy_filt_wout[j],
                     f_bias=hy_filt_bias[j], w_out=hy_w_out[j], b_out=hy_b_out[j])
            h, u = hyena_mixer(h, u, mod, i, p, geo, need_ctx, mlp_norm)
        else:
            lambda_init = 0.8 - 0.6 * math.exp(-0.3 * i)
            p = dict(w_qkv=df_w_qkv[j], lambdas=df_lambda[j], subln_g=df_subln_g[j], w_o=df_w_o[j])
            h, u = diff_mixer(h, u, mod, i, p, geo, need_ctx, lambda_init, mlp_norm)
        next_norm = ("modulate", norm_g4, i + 1, 0) if need_ctx else ("final", final_norm_g)
        res = mlp(h, u, mod, i, (w1s, i), (w2s, i), rows_out, next_norm)
        if need_ctx:
            h, u = res

    return res.reshape(B, L, D)
```

```python
import functools
import math

import jax
import jax.numpy as jnp
import numpy as np
from jax import lax
from jax.experimental import pallas as pl
from jax.experimental.pallas import tpu as pltpu

F32 = jnp.float32
BF16 = jnp.bfloat16
HIGHEST = lax.Precision.HIGHEST
LOG2E = math.log2(math.e)

GRID_W = 64
ROPE_THETA = 10000.0
NORM_EPS = 1e-6
N_MIXERS = 3

MLA_NOPE = 128
MLA_ROPE = 64
MLA_V = 128
MLA_KV_RANK = 512
MLA_HEAD_PAD = 256
MLA_Q_TILE = 1024

HY_SHORT = 3
HY_EMB_DIM = 33
HY_EMB_PAD = 64
HY_TARGET = 1e-2
HY_FAST_PCT = 0.3
HY_SLOW_PCT = 1.5

DF_HEAD_DIM = 128
DF_SUBLN_EPS = 1e-5
DF_Q_TILE = 512

LANES = 128
MXU_COLS = 256
VMEM_LIMIT_MB = 56

ROW_TILE = 1024
COL_TILE = 1024
NORM_ROW_TILE = 512
K_TILE = 1024


def _params(semantics, vmem_mb=VMEM_LIMIT_MB):
    return pltpu.CompilerParams(dimension_semantics=semantics, vmem_limit_bytes=vmem_mb << 20)


class Rows:
    def __init__(self, B, C, L, bm, lat_only=False):
        assert L % bm == 0 and (B * C) % bm == 0
        self.geometry = (B, C, L, lat_only)
        self.bm = bm
        self.tiles_per_batch = L // bm
        self.n_lat = B * L // bm
        self.n = self.n_lat + (0 if lat_only else B * C // bm)

    def with_bm(self, bm):
        B, C, L, lat_only = self.geometry
        return Rows(B, C, L, bm, lat_only)

    def mod_row(self, i):
        return jnp.where(i < self.n_lat, 1 + i // self.tiles_per_batch, 0)

    def pos_block(self, i):
        return jnp.where(i < self.n_lat, i % self.tiles_per_batch, self.tiles_per_batch)


def _mod_spec(layer, chunk, rows, bn, col_of):
    return pl.BlockSpec((None, None, None, 1, bn),
                        lambda *g: (layer, chunk, rows.mod_row(g[0]), 0, col_of(*g)))


def _w_shape(w):
    return w[0].shape[1:] if isinstance(w, tuple) else w.shape


def _row_operand(arr, rows, width, col_of):
    bm = rows.bm
    n_lat = rows.n_lat
    if isinstance(arr, tuple) and rows.n == n_lat:
        arr = arr[0]
    if not isinstance(arr, tuple):
        return [(arr, (bm, width), lambda *g: (g[0], col_of(*g)))], lambda refs: refs[0][...]
    specs = [(arr[0], (bm, width), lambda *g: (jnp.minimum(g[0], n_lat - 1), col_of(*g))),
             (arr[1], (bm, width), lambda *g: (jnp.maximum(g[0] - n_lat, 0), col_of(*g)))]
    return specs, lambda refs: jnp.where(pl.program_id(0) < n_lat, refs[0][...], refs[1][...])


def fused_matmul(a, w, extras, outs, epilogue, *, rows, bn, bk=None, name):
    K, N = _w_shape(w)
    bk = bk or K
    nk = K // bk
    assert K % bk == 0 and N % bn == 0
    a_specs, read_a = _row_operand(a, rows, bk, lambda i, j, k: k)
    n_a = len(a_specs)
    in_specs = [pl.BlockSpec(bs, im) for _, bs, im in a_specs]
    if isinstance(w, tuple):
        w, w_layer = w
        in_specs += [pl.BlockSpec((None, bk, bn), lambda i, j, k: (w_layer, k, j))]
    else:
        in_specs += [pl.BlockSpec((bk, bn), lambda i, j, k: (k, j))]
    in_specs += [pl.BlockSpec(bs, im) for _, bs, im in extras]
    n_ex = len(extras)
    n_out = len(outs)

    def kern(*refs):
        w_ref = refs[n_a]
        ex = refs[n_a + 1:n_a + 1 + n_ex]
        out = refs[n_a + 1 + n_ex:n_a + 1 + n_ex + n_out]

        def dot():
            return jnp.dot(read_a(refs[:n_a]), w_ref[...], preferred_element_type=F32)

        if nk == 1:
            epilogue(dot(), ex, out)
        else:
            acc_ref = out[0]
            k = pl.program_id(2)

            @pl.when(k == 0)
            def _():
                acc_ref[...] = dot()

            @pl.when((k > 0) & (k < nk - 1))
            def _():
                acc_ref[...] += dot()

            @pl.when(k == nk - 1)
            def _():
                epilogue(acc_ref[...] + dot(), ex, out)

    res = pl.pallas_call(
        kern,
        grid=(rows.n, N // bn, nk),
        in_specs=in_specs,
        out_specs=[pl.BlockSpec(bs, im) for _, _, bs, im in outs],
        out_shape=[jax.ShapeDtypeStruct(s, d) for s, d, _, _ in outs],
        compiler_params=_params(("parallel", "arbitrary", "arbitrary")),
        name=name,
    )(*[e[0] for e in a_specs], w, *[e[0] for e in extras])
    return res


def _rot_half(x, quarter):
    n = x.shape[-1]
    lane = lax.broadcasted_iota(jnp.int32, x.shape, x.ndim - 1)
    first = (lane % (2 * quarter)) < quarter
    return jnp.where(first, -pltpu.roll(x, n - quarter, x.ndim - 1), pltpu.roll(x, quarter, x.ndim - 1))


def _rms(x, g, eps):
    return x * lax.rsqrt(jnp.mean(x * x, axis=-1, keepdims=True) + eps) * g


def gated_matmul(a, w, resid, mod, layer, chunk, rows, *, name, bn=COL_TILE, bk=None, bias=None, norm=None):
    T = rows.n * rows.bm
    bm = rows.bm
    N = _w_shape(w)[1]
    res_specs, read_res = _row_operand(resid, rows, bn, lambda i, j, k: j)
    n_res = len(res_specs)
    extras = res_specs + [(mod,) + _spec_parts(_mod_spec(layer, chunk, rows, bn, lambda i, j, k: j))]
    if bias is not None:
        extras.append((bias.reshape(1, N), (1, bn), lambda i, j, k: (0, j)))
    n_fix = len(extras)
    tile = lambda i, j, k: (i, j)
    outs = [((T, N), F32, (bm, bn), tile)]
    if norm is not None:
        assert bn == N
        if norm[0] == "modulate":
            _, g4, nl, which = norm
            extras += [(g4, (None, None, 1, N), lambda i, j, k: (nl, which, 0, 0)),
                       (mod,) + _spec_parts(_mod_spec(nl, 3 * which, rows, N, lambda i, j, k: 0)),
                       (mod,) + _spec_parts(_mod_spec(nl, 3 * which + 1, rows, N, lambda i, j, k: 0))]
            outs.append(((T, N), BF16, (bm, bn), tile))
        else:
            extras.append((norm[1].reshape(1, N), (1, N), lambda i, j, k: (0, 0)))

    def epi(acc, ex, out):
        y = acc if bias is None else acc + ex[n_res + 1][...]
        hn = read_res(ex[:n_res]) + ex[n_res][...] * y
        if norm is None:
            out[0][...] = hn
        elif norm[0] == "modulate":
            g_ref, sh_ref, sc_ref = ex[n_fix:]
            out[0][...] = hn
            out[1][...] = (_rms(hn, g_ref[...], NORM_EPS) * (1.0 + sc_ref[...]) + sh_ref[...]).astype(BF16)
        else:
            out[0][...] = _rms(hn, ex[n_fix][...], NORM_EPS)

    res = fused_matmul(a, w, extras, outs, epi, rows=rows, bn=bn, bk=bk, name=name)
    return res if len(res) > 1 else res[0]


def _spec_parts(spec):
    return spec.block_shape, spec.index_map


def h_width(h):
    return (h[0] if isinstance(h, tuple) else h).shape[1]


def _out_rows(rows):
    return rows.with_bm(min(NORM_ROW_TILE, rows.bm))


def adaln_table(cvec, ada_w, ada_b):
    depth, D, _ = ada_w.shape
    R = cvec.shape[0]

    def kern(c_ref, w_ref, b_ref, o_ref):
        c = c_ref[...]
        s = c / (1.0 + jnp.exp(-c))
        o_ref[...] = jnp.dot(s.astype(BF16), w_ref[...].astype(BF16), preferred_element_type=F32) + b_ref[...]

    out = pl.pallas_call(
        kern,
        grid=(depth, 6),
        in_specs=[pl.BlockSpec((R, D), lambda l, j: (0, 0)),
                  pl.BlockSpec((None, D, D), lambda l, j: (l, 0, j)),
                  pl.BlockSpec((None, 1, D), lambda l, j: (l, 0, j))],
        out_specs=pl.BlockSpec((None, None, R, D), lambda l, j: (l, j, 0, 0)),
        out_shape=jax.ShapeDtypeStruct((depth, 6, R, D), F32),
        compiler_params=_params(("parallel", "arbitrary")),
        name="adaln_table",
    )(cvec, ada_w, ada_b.reshape(depth, 1, 6 * D))
    return out.reshape(depth, 6, R, 1, D)


def norm_modulate(h, norm_g4, mod, layer, which, rows):
    D = norm_g4.shape[-1]
    bm = rows.bm
    h_specs, read_h = _row_operand(h, rows, D, lambda i: 0)
    n_h = len(h_specs)

    def kern(*refs):
        g_ref, sh_ref, sc_ref, o_ref = refs[n_h:]
        y = _rms(read_h(refs[:n_h]), g_ref[...], NORM_EPS)
        o_ref[...] = (y * (1.0 + sc_ref[...]) + sh_ref[...]).astype(BF16)

    return pl.pallas_call(
        kern,
        grid=(rows.n,),
        in_specs=[pl.BlockSpec(bs, im) for _, bs, im in h_specs]
        + [pl.BlockSpec((None, None, 1, D), lambda i: (layer, which, 0, 0)),
           _mod_spec(layer, 3 * which, rows, D, lambda i: 0),
           _mod_spec(layer, 3 * which + 1, rows, D, lambda i: 0)],
        out_specs=pl.BlockSpec((bm, D), lambda i: (i, 0)),
        out_shape=jax.ShapeDtypeStruct((rows.n * bm, D), BF16),
        compiler_params=_params(("parallel",)),
        name="norm_modulate",
    )(*[e[0] for e in h_specs], norm_g4, mod, mod)


def mlp(h, u, mod, layer, w1, w2, rows, norm):
    T = rows.n * rows.bm
    F, D = _w_shape(w2)

    def relu2(acc, ex, out):
        r = jnp.maximum(acc, 0.0)
        out[0][...] = (r * r).astype(BF16)

    hid = fused_matmul(u, w1, [], [((T, F), BF16, (rows.bm, COL_TILE), lambda i, j, k: (i, j))], relu2,
                       rows=rows, bn=COL_TILE, name="mlp_up")[0]
    return gated_matmul(hid, w2, h, mod, layer, 5, rows, bn=D, bk=K_TILE, norm=norm, name="mlp_down")


def axial_rope_tables(L, rot_dim):
    rows = L // GRID_W
    row = jnp.repeat(jnp.arange(rows, dtype=F32), GRID_W)
    col = jnp.tile(jnp.arange(GRID_W, dtype=F32), rows)
    pos = jnp.stack([row, col], axis=-1)
    n_freq = rot_dim // 4
    inv_freq = ROPE_THETA ** (-jnp.arange(n_freq, dtype=F32) / n_freq)
    ang = pos[:, :, None, None] * inv_freq
    ang = jnp.broadcast_to(ang, (L, 2, 2, n_freq)).reshape(L, rot_dim)
    return jnp.cos(ang), jnp.sin(ang)


def rope_tables_padded(L, rot_dim, bm):
    cos, sin = axial_rope_tables(L, rot_dim)
    cos = jnp.pad(cos, ((0, bm), (0, LANES - rot_dim)), constant_values=1.0)
    sin = jnp.pad(sin, ((0, bm), (0, LANES - rot_dim)))
    return cos, sin


def mla_down(u, w_dq, q_g, w_dkv, kv_g, cos, sin, rows):
    T = rows.n * rows.bm
    bm = rows.bm
    qr = w_dq.shape[1]
    wd = jnp.concatenate([w_dq, w_dkv, jnp.zeros((w_dq.shape[0], LANES - MLA_ROPE), w_dq.dtype)],
                         axis=1).astype(BF16)
    n_all = wd.shape[1]
    c0 = qr + MLA_KV_RANK

    def epi(acc, ex, out):
        qg_ref, kvg_ref, cos_ref, sin_ref = ex
        out[0][...] = _rms(acc[:, :qr], qg_ref[...], NORM_EPS).astype(BF16)
        out[1][...] = _rms(acc[:, qr:c0], kvg_ref[...], NORM_EPS).astype(BF16)
        kr = acc[:, c0:]
        out[2][...] = (kr * cos_ref[...] + _rot_half(kr, MLA_ROPE // 4) * sin_ref[...]).astype(BF16)

    extras = [(q_g.reshape(1, qr), (1, qr), lambda i, j, k: (0, 0)),
              (kv_g.reshape(1, MLA_KV_RANK), (1, MLA_KV_RANK), lambda i, j, k: (0, 0)),
              (cos, (bm, LANES), lambda i, j, k: (rows.pos_block(i), 0)),
              (sin, (bm, LANES), lambda i, j, k: (rows.pos_block(i), 0))]
    outs = [((T, qr), BF16, (bm, qr), lambda i, j, k: (i, 0)),
            ((T, MLA_KV_RANK), BF16, (bm, MLA_KV_RANK), lambda i, j, k: (i, 0)),
            ((T, LANES), BF16, (bm, LANES), lambda i, j, k: (i, 0))]
    return fused_matmul(u, wd, extras, outs, epi, rows=rows, bn=n_all, name="mla_down")


def mla_queries(cq, w_uq, cos, sin, rows, heads):
    T = rows.n * rows.bm
    bm = rows.bm
    qr = w_uq.shape[0]
    hd = MLA_NOPE + MLA_ROPE
    w = w_uq.reshape(qr, heads, hd)
    w = jnp.pad(w, ((0, 0), (0, 0), (0, MLA_HEAD_PAD - hd))).reshape(qr, heads * MLA_HEAD_PAD).astype(BF16)
    scale = hd ** -0.5 * LOG2E
    bn = 2 * COL_TILE

    def epi(acc, ex, out):
        cos_ref, sin_ref = ex
        for hh in range(bn // MLA_HEAD_PAD):
            c = hh * MLA_HEAD_PAD
            out[0][:, c:c + LANES] = (acc[:, c:c + LANES] * scale).astype(BF16)
            r = acc[:, c + LANES:c + 2 * LANES]
            r = r * cos_ref[...] + _rot_half(r, MLA_ROPE // 4) * sin_ref[...]
            out[0][:, c + LANES:c + 2 * LANES] = (r * scale).astype(BF16)

    extras = [(cos, (bm, LANES), lambda i, j, k: (rows.pos_block(i), 0)),
              (sin, (bm, LANES), lambda i, j, k: (rows.pos_block(i), 0))]
    outs = [((T, heads * MLA_HEAD_PAD), BF16, (bm, bn), lambda i, j, k: (i, j))]
    return fused_matmul(cq, w, extras, outs, epi, rows=rows, bn=bn, name="mla_queries")[0]


def plain_matmul(a, w, rows, bn, *, name, out_dtype=BF16, bias=None):
    T = rows.n * rows.bm
    N = w.shape[1]
    extras = [] if bias is None else [(bias.reshape(1, N), (1, bn), lambda i, j, k: (0, j))]

    def epi(acc, ex, out):
        y = acc if bias is None else acc + ex[0][...]
        out[0][...] = y.astype(out_dtype)

    return fused_matmul(a, w, extras, [((T, N), out_dtype, (rows.bm, bn), lambda i, j, k: (i, j))], epi,
                        rows=rows, bn=bn, name=name)[0]


def mla_attention(q, kv, kr, B, C, L, heads, ctx_queries):
    hp = 2
    tq = C if ctx_queries else MLA_Q_TILE
    n_t = 1 if ctx_queries else L // tq
    ctx_blk = B * L // C
    S = C if ctx_queries else C + L
    kvw = 2 * LANES
    q_row = (lambda b, g, t: ctx_blk + b) if ctx_queries else (lambda b, g, t: b * n_t + t)

    def kern(*refs):
        if ctx_queries:
            q_ref, kvc_ref, krc_ref, o_ref, k_s, v_s = refs
            parts = [(kvc_ref, krc_ref, 0, C)]
        else:
            q_ref, kvl_ref, kvc_ref, krl_ref, krc_ref, o_ref, k_s, v_s = refs
            parts = [(kvc_ref, krc_ref, 0, C), (kvl_ref, krl_ref, C, S)]

        @pl.when(pl.program_id(2) == 0)
        def _():
            for hh in range(hp):
                c0 = hh * kvw
                for kv_ref, kr_ref, lo, hi in parts:
                    k_s[hh, lo:hi, :LANES] = kv_ref[:, c0:c0 + LANES]
                    k_s[hh, lo:hi, LANES:] = kr_ref[...]
                    v_s[hh, lo:hi, :MLA_V] = kv_ref[:, c0 + LANES:c0 + kvw]
                v_s[hh, :, MLA_V:] = jnp.ones((S, MXU_COLS - MLA_V), BF16)

        ss = [lax.dot_general(q_ref[:, hh * MLA_HEAD_PAD:(hh + 1) * MLA_HEAD_PAD], k_s[hh],
                              (((1,), (1,)), ((), ())), preferred_element_type=F32) for hh in range(hp)]
        ps = [jnp.exp2(s - jnp.max(s, axis=-1, keepdims=True)).astype(BF16) for s in ss]
        for hh in range(hp):
            ov = jnp.dot(ps[hh], v_s[hh], preferred_element_type=F32)
            o_ref[:, hh * MLA_V:(hh + 1) * MLA_V] = (ov[:, :MLA_V] / ov[:, MLA_V:2 * MLA_V]).astype(BF16)

    lat_specs = [pl.BlockSpec((L, hp * kvw), lambda b, g, t: (b, g)),
                 pl.BlockSpec((L, LANES), lambda b, g, t: (b, 0))]
    ctx_specs = [pl.BlockSpec((C, hp * kvw), lambda b, g, t: (ctx_blk + b, g)),
                 pl.BlockSpec((C, LANES), lambda b, g, t: (ctx_blk + b, 0))]
    q_spec = pl.BlockSpec((tq, hp * MLA_HEAD_PAD), lambda b, g, t: (q_row(b, g, t), g))
    if ctx_queries:
        in_specs, args = [q_spec] + ctx_specs, (q, kv, kr)
    else:
        in_specs, args = [q_spec, lat_specs[0], ctx_specs[0], lat_specs[1], ctx_specs[1]], (q, kv, kv, kr, kr)
    return pl.pallas_call(
        kern,
        grid=(B, heads // hp, n_t),
        in_specs=in_specs,
        out_specs=pl.BlockSpec((tq, hp * MLA_V), lambda b, g, t: (b * n_t + t, g)),
        out_shape=jax.ShapeDtypeStruct((B * n_t * tq, heads * MLA_V), BF16),
        scratch_shapes=[pltpu.VMEM((hp, S, MLA_HEAD_PAD), BF16), pltpu.VMEM((hp, S, MXU_COLS), BF16)],
        compiler_params=_params(("parallel", "parallel", "arbitrary")),
        name="mla_attention_ctx" if ctx_queries else "mla_attention",
    )(*args)


def mla_mixer(h, u, mod, layer, p, geo, need_ctx, norm):
    B, C, L, rows_all, rows_out = geo
    heads = p["w_uq"].shape[1] // (MLA_NOPE + MLA_ROPE)
    cos, sin = rope_tables_padded(L, MLA_ROPE, rows_all.bm)
    cq, ckv, kr = mla_down(u, p["w_dq"], p["q_g"], p["w_dkv"], p["kv_g"], cos, sin, rows_all)
    kv = plain_matmul(ckv, p["w_ukv"].astype(BF16), rows_all, p["w_ukv"].shape[1], name="mla_kv")
    q = mla_queries(cq, p["w_uq"], cos, sin, rows_out, heads)
    o = mla_attention(q, kv, kr, B, C, L, heads, False)
    if need_ctx:
        o = (o, mla_attention(q, kv, kr, B, C, L, heads, True))
    return gated_matmul(o, p["w_o"].astype(BF16), h, mod, layer, 2, _out_rows(rows_out), bn=h_width(h), norm=norm,
                        name="mla_out")


def _diff_pair_layout():
    quarter = DF_HEAD_DIM // 4
    n = np.arange(2 * DF_HEAD_DIM)
    hf, c, a, r = n // DF_HEAD_DIM, (n % DF_HEAD_DIM) // (2 * quarter), (n % (2 * quarter)) // quarter, n % quarter
    head_perm = c * DF_HEAD_DIM + a * 2 * quarter + hf * quarter + r
    table_cols = (a * 2 * quarter + r)[:DF_HEAD_DIM]
    return head_perm, table_cols


def diff_rope_tables(L, bm):
    cos, sin = axial_rope_tables(L, DF_HEAD_DIM)
    _, cols = _diff_pair_layout()
    cos = jnp.pad(cos[:, cols], ((0, bm), (0, 0)), constant_values=1.0)
    sin = jnp.pad(sin[:, cols], ((0, bm), (0, 0)))
    return cos, sin


def diff_qkv(u, w_qkv, cos, sin, rows):
    T = rows.n * rows.bm
    bm = rows.bm
    D = w_qkv.shape[0]
    bn = COL_TILE
    hw = 2 * DF_HEAD_DIM
    scale = DF_HEAD_DIM ** -0.5 * LOG2E
    n_q = D // bn
    head_perm, _ = _diff_pair_layout()
    qk_cols = (np.arange(2 * D) // hw * hw)[:, None].reshape(-1, hw) + head_perm[None, :]
    cols = np.concatenate([qk_cols.reshape(-1), np.arange(2 * D, 3 * D)])
    w = w_qkv[:, cols].astype(BF16)

    def epi(acc, ex, out):
        cos_ref, sin_ref = ex
        mul = jnp.where(pl.program_id(1) < n_q, scale, 1.0).astype(F32)
        cs, sn = cos_ref[...] * mul, sin_ref[...] * mul
        for s in range(bn // hw):
            x1 = acc[:, s * hw:s * hw + LANES]
            x2 = acc[:, s * hw + LANES:(s + 1) * hw]
            out[0][:, s * hw:s * hw + LANES] = (x1 * cs - x2 * sn).astype(BF16)
            out[0][:, s * hw + LANES:(s + 1) * hw] = (x2 * cs + x1 * sn).astype(BF16)

    def table_block(i, j, k):
        return jnp.where(j < 2 * n_q, rows.pos_block(i), rows.tiles_per_batch), 0

    extras = [(cos, (bm, LANES), table_block), (sin, (bm, LANES), table_block)]
    outs = [((T, 3 * D), BF16, (bm, bn), lambda i, j, k: (i, j))]
    return fused_matmul(u, w, extras, outs, epi, rows=rows, bn=bn, name="diff_qkv")[0]


def diff_attention(qkv, lambdas, subln_g, lambda_init, B, C, L, D, ctx_queries):
    hd = DF_HEAD_DIM
    hw = 2 * hd
    heads = D // hw
    hp = 2
    tq = C if ctx_queries else DF_Q_TILE
    n_t = 1 if ctx_queries else L // tq
    ctx_blk = B * L // C
    S = C if ctx_queries else C + L
    kcol = D // (hp * hw)
    vcol = 2 * D // (hp * hw)
    q_row = (lambda b, g, t: ctx_blk + b) if ctx_queries else (lambda b, g, t: b * n_t + t)

    def kern(*refs):
        if ctx_queries:
            q_ref, kc_ref, vc_ref, lam_ref, g_ref, o_ref, k_s, v_s = refs
            parts = [(kc_ref, vc_ref, 0, C)]
        else:
            q_ref, kl_ref, kc_ref, vl_ref, vc_ref, lam_ref, g_ref, o_ref, k_s, v_s = refs
            parts = [(kc_ref, vc_ref, 0, C), (kl_ref, vl_ref, C, S)]

        @pl.when(pl.program_id(2) == 0)
        def _():
            map0 = (lax.broadcasted_iota(jnp.int32, (1, hw), 1) % hd) < hd // 2
            for k_ref, v_ref, lo, hi in parts:
                for hh in range(hp):
                    k = k_ref[:, hh * hw:(hh + 1) * hw]
                    k_s[2 * hh, lo:hi, :] = jnp.where(map0, k, jnp.zeros_like(k))
                    k_s[2 * hh + 1, lo:hi, :] = jnp.where(map0, jnp.zeros_like(k), k)
                    v_s[hh, lo:hi, :] = v_ref[:, hh * hw:(hh + 1) * hw]

        lf = lam_ref[...]
        lam = (jnp.exp(jnp.sum(lf[0:1] * lf[1:2], axis=-1, keepdims=True))
               - jnp.exp(jnp.sum(lf[2:3] * lf[3:4], axis=-1, keepdims=True)) + lambda_init)

        nc = 2 * hp
        ss = [lax.dot_general(q_ref[:, (c // 2) * hw:(c // 2 + 1) * hw], k_s[c],
                              (((1,), (1,)), ((), ())), preferred_element_type=F32) for c in range(nc)]
        ms = [jnp.max(s, axis=-1, keepdims=True) for s in ss]
        os_, ls = [0.0] * nc, [0.0] * nc
        for lo in range(0, S, MXU_COLS):
            for c in range(nc):
                p = jnp.exp2(ss[c][:, lo:lo + MXU_COLS] - ms[c])
                ls[c] = ls[c] + jnp.sum(p[:, :LANES] + p[:, LANES:], axis=-1, keepdims=True)
                os_[c] = os_[c] + jnp.dot(p.astype(BF16), v_s[c // 2, lo:lo + MXU_COLS, :],
                                          preferred_element_type=F32)
        for hh in range(hp):
            o = os_[2 * hh] / ls[2 * hh] - lam * (os_[2 * hh + 1] / ls[2 * hh + 1])
            o_ref[:, hh * hw:(hh + 1) * hw] = (_rms(o, g_ref[...], DF_SUBLN_EPS)
                                               * (1.0 - lambda_init)).astype(BF16)

    bw = hp * hw
    q_spec = pl.BlockSpec((tq, bw), lambda b, g, t: (q_row(b, g, t), g))
    lat_specs = [pl.BlockSpec((L, bw), lambda b, g, t: (b, kcol + g)),
                 pl.BlockSpec((L, bw), lambda b, g, t: (b, vcol + g))]
    ctx_specs = [pl.BlockSpec((C, bw), lambda b, g, t: (ctx_blk + b, kcol + g)),
                 pl.BlockSpec((C, bw), lambda b, g, t: (ctx_blk + b, vcol + g))]
    par_specs = [pl.BlockSpec((4, hd), lambda b, g, t: (0, 0)), pl.BlockSpec((1, hw), lambda b, g, t: (0, 0))]
    if ctx_queries:
        in_specs, args = [q_spec] + ctx_specs, (qkv, qkv, qkv)
    else:
        in_specs = [q_spec, lat_specs[0], ctx_specs[0], lat_specs[1], ctx_specs[1]]
        args = (qkv, qkv, qkv, qkv, qkv)
    return pl.pallas_call(
        kern,
        grid=(B, heads // hp, n_t),
        in_specs=in_specs + par_specs,
        out_specs=pl.BlockSpec((tq, bw), lambda b, g, t: (b * n_t + t, g)),
        out_shape=jax.ShapeDtypeStruct((B * n_t * tq, D), BF16),
        scratch_shapes=[pltpu.VMEM((2 * hp, S, hw), BF16), pltpu.VMEM((hp, S, hw), BF16)],
        compiler_params=_params(("parallel", "parallel", "arbitrary")),
        name="diff_attention_ctx" if ctx_queries else "diff_attention",
    )(*args, lambdas, subln_g.reshape(1, hw))


def diff_mixer(h, u, mod, layer, p, geo, need_ctx, lambda_init, norm):
    B, C, L, rows_all, rows_out = geo
    D = h_width(h)
    cos, sin = diff_rope_tables(L, rows_all.bm)
    qkv = diff_qkv(u, p["w_qkv"], cos, sin, rows_all)
    o = diff_attention(qkv, p["lambdas"], p["subln_g"], lambda_init, B, C, L, D, False)
    if need_ctx:
        o = (o, diff_attention(qkv, p["lambdas"], p["subln_g"], lambda_init, B, C, L, D, True))
    return gated_matmul(o, p["w_o"].astype(BF16), h, mod, layer, 2, _out_rows(rows_out), bn=D, norm=norm,
                        name="diff_out")


def dft_matrices(Ls):
    n = 2 * Ls
    t0n = min(64, Ls)
    t1n = Ls // t0n
    f = jnp.arange(Ls, dtype=jnp.int32)[:, None]
    a1 = ((f * (jnp.arange(t1n, dtype=jnp.int32) * t0n)[None, :]) % n).astype(F32) * (2.0 * math.pi / n)
    a0 = ((f * jnp.arange(t0n, dtype=jnp.int32)[None, :]) % n).astype(F32) * (2.0 * math.pi / n)
    c1, s1 = jnp.cos(a1)[:, :, None], jnp.sin(a1)[:, :, None]
    c0, s0 = jnp.cos(a0)[:, None, :], jnp.sin(a0)[:, None, :]
    cosm = (c1 * c0 - s1 * s0).reshape(Ls, Ls)
    sinm = (s1 * c0 + c1 * s0).reshape(Ls, Ls)
    nyq = jnp.where(jnp.arange(Ls) % 2 == 0, 1.0, -1.0).astype(F32)[None, :]
    imag = jnp.where(f == 0, nyq, -sinm)
    fwd = jnp.concatenate([cosm, imag], axis=0)
    col = jnp.arange(n)
    cscale = jnp.where((col == 0) | (col == Ls), 1.0 / n, 2.0 / n).astype(F32)
    inv = fwd.T * cscale[None, :]
    return fwd.astype(BF16), inv.astype(BF16)


def hyena_filter_time(Ls, p):
    D = p["f_bias"].shape[0]
    order = p["f_w2"].shape[0]
    bands = (HY_EMB_DIM - 1) // 2
    t = jnp.linspace(0.0, 1.0, Ls, dtype=F32)[:, None]
    w = 2.0 * math.pi * jnp.arange(Ls, dtype=F32)[:, None] / Ls
    f = jnp.linspace(1e-4, bands - 1, bands, dtype=F32)
    feats = jnp.concatenate([t, jnp.cos(f * w), -jnp.sin(f * w)], axis=-1)
    feats = jnp.pad(feats, ((0, 0), (0, HY_EMB_PAD - HY_EMB_DIM)))
    w1 = jnp.pad(p["f_w1"], ((0, HY_EMB_PAD - HY_EMB_DIM), (0, 0)))
    deltas = jnp.abs(jnp.linspace(math.log(HY_TARGET) / HY_SLOW_PCT, math.log(HY_TARGET) / HY_FAST_PCT,
                                  D, dtype=F32))
    deltas2 = jnp.concatenate([deltas, deltas]).reshape(1, 2 * D)

    def ffn_kern(x_ref, w1_ref, b1_ref, w2_ref, b2_ref, w3_ref, b3_ref, fr_ref, o_ref):
        dot = functools.partial(jnp.dot, precision=HIGHEST, preferred_element_type=F32)
        fr = fr_ref[...]
        hcur = jnp.sin(fr[0:1] * (dot(x_ref[...], w1_ref[...]) + b1_ref[...]))
        hcur = jnp.sin(fr[1:2] * (dot(hcur, w2_ref[...]) + b2_ref[...]))
        o_ref[...] = jnp.sin(fr[2:3] * (dot(hcur, w3_ref[...]) + b3_ref[...]))

    full = lambda a: pl.BlockSpec(a.shape, lambda: (0,) * a.ndim)
    ffn_in = [feats, w1, p["f_b1"].reshape(1, order), p["f_w2"], p["f_b2"].reshape(1, order),
              p["f_w3"], p["f_b3"].reshape(1, order), p["f_freq"]]
    hff = pl.pallas_call(
        ffn_kern,
        in_specs=[full(a) for a in ffn_in],
        out_specs=pl.BlockSpec((Ls, order), lambda: (0, 0)),
        out_shape=jax.ShapeDtypeStruct((Ls, order), F32),
        name="hyena_filter_ffn",
    )(*ffn_in)

    bn = COL_TILE // 2

    def out_kern(h_ref, w_ref, d_ref, o_ref):
        tt = lax.broadcasted_iota(jnp.int32, (Ls, 1), 0).astype(F32) * (1.0 / (Ls - 1))
        hw = jnp.dot(h_ref[...], w_ref[...], precision=HIGHEST, preferred_element_type=F32)
        o_ref[...] = hw * jnp.exp(-tt * d_ref[...])

    return pl.pallas_call(
        out_kern,
        grid=(2 * D // bn,),
        in_specs=[pl.BlockSpec((Ls, order), lambda j: (0, 0)),
                  pl.BlockSpec((order, bn), lambda j: (0, j)),
                  pl.BlockSpec((1, bn), lambda j: (0, j))],
        out_specs=pl.BlockSpec((Ls, bn), lambda j: (0, j)),
        out_shape=jax.ShapeDtypeStruct((Ls, 2 * D), F32),
        compiler_params=_params(("parallel",)),
        name="hyena_filter_out",
    )(hff, p["f_wout"], deltas2)


def hyena_filter_spectrum(hfb, f_bias, fwd, Ls, D):
    n = 2 * Ls
    cb = LANES
    ncb = D // cb

    def kern(w_ref, hf_ref, hb_ref, fb_ref, o_ref):
        row = lax.broadcasted_iota(jnp.int32, (Ls, 1), 0)
        hb = jnp.where(row == 0, 0.0, hb_ref[...])
        ab = jnp.dot(w_ref[...], jnp.concatenate([hf_ref[...], hb], axis=1).astype(BF16),
                     preferred_element_type=F32)
        a, b = ab[:, :cb], ab[:, cb:]
        frow = lax.broadcasted_iota(jnp.int32, (n, 1), 0)
        real = frow <= Ls
        o_ref[...] = a + jnp.where(real, b + fb_ref[...], -b)

    return pl.pallas_call(
        kern,
        grid=(ncb,),
        in_specs=[pl.BlockSpec((n, Ls), lambda j: (0, 0), pipeline_mode=pl.Buffered(1)),
                  pl.BlockSpec((Ls, cb), lambda j: (0, j)),
                  pl.BlockSpec((Ls, cb), lambda j: (0, ncb + j)),
                  pl.BlockSpec((1, cb), lambda j: (0, j))],
        out_specs=pl.BlockSpec((n, cb), lambda j: (0, j)),
        out_shape=jax.ShapeDtypeStruct((n, D), F32),
        compiler_params=_params(("parallel",)),
        name="hyena_filter_spectrum",
    )(fwd, hfb, hfb, f_bias.reshape(1, D))


def _short_conv(x, w, b):
    n = x.shape[0]
    row = lax.broadcasted_iota(jnp.int32, (n, 1), 0)
    prev = jnp.where(row == 0, 0.0, pltpu.roll(x, 1, 0))
    nxt = jnp.where(row == n - 1, 0.0, pltpu.roll(x, n - 1, 0))
    return prev * w[0:1] + x * w[1:2] + nxt * w[2:3] + b


def hyena_segment(z, p, Ls, row_blk0, B, D):
    n = 2 * Ls
    cb = MXU_COLS
    ncb = D // cb
    fwd, inv = dft_matrices(Ls)
    hfb = hyena_filter_time(Ls, p)
    kf = hyena_filter_spectrum(hfb, p["f_bias"], fwd, Ls, D)
    conv_w, conv_b = p["conv_w"], p["conv_b"].reshape(1, 3 * D)

    def fwd_kern(w_ref, x1_ref, v_ref, cw1_ref, cb1_ref, cwv_ref, cbv_ref, kf_ref, y_ref):
        g = (_short_conv(v_ref[...].astype(F32), cwv_ref[...], cbv_ref[...])
             * _short_conv(x1_ref[...].astype(F32), cw1_ref[...], cb1_ref[...])).astype(BF16)
        u = jnp.dot(w_ref[...], g, preferred_element_type=F32)
        ure, uim = u[:Ls], u[Ls:]
        kre, kim = kf_ref[:Ls, :], kf_ref[Ls:, :]
        first = lax.broadcasted_iota(jnp.int32, (Ls, 1), 0) == 0
        y_ref[:Ls, :] = (ure * kre - jnp.where(first, 0.0, uim * kim)).astype(BF16)
        y_ref[Ls:, :] = jnp.where(first, uim * kim, ure * kim + uim * kre).astype(BF16)

    y = pl.pallas_call(
        fwd_kern,
        grid=(ncb, B),
        in_specs=[pl.BlockSpec((n, Ls), lambda j, b: (0, 0), pipeline_mode=pl.Buffered(1)),
                  pl.BlockSpec((Ls, cb), lambda j, b: (row_blk0 + b, ncb + j)),
                  pl.BlockSpec((Ls, cb), lambda j, b: (row_blk0 + b, 2 * ncb + j)),
                  pl.BlockSpec((HY_SHORT, cb), lambda j, b: (0, ncb + j)),
                  pl.BlockSpec((1, cb), lambda j, b: (0, ncb + j)),
                  pl.BlockSpec((HY_SHORT, cb), lambda j, b: (0, 2 * ncb + j)),
                  pl.BlockSpec((1, cb), lambda j, b: (0, 2 * ncb + j)),
                  pl.BlockSpec((n, cb), lambda j, b: (0, j))],
        out_specs=pl.BlockSpec((None, n, cb), lambda j, b: (b, 0, j)),
        out_shape=jax.ShapeDtypeStruct((B, n, D), BF16),
        compiler_params=_params(("parallel", "arbitrary")),
        name="hyena_dft",
    )(fwd, z, z, conv_w, conv_b, conv_w, conv_b, kf)

    def inv_kern(w_ref, y_ref, x0_ref, cw0_ref, cb0_ref, o_ref):
        conv = jnp.dot(w_ref[...], y_ref[...], preferred_element_type=F32)
        o_ref[...] = (conv * _short_conv(x0_ref[...].astype(F32), cw0_ref[...], cb0_ref[...])).astype(BF16)

    return pl.pallas_call(
        inv_kern,
        grid=(ncb, B),
        in_specs=[pl.BlockSpec((Ls, n), lambda j, b: (0, 0), pipeline_mode=pl.Buffered(1)),
                  pl.BlockSpec((None, n, cb), lambda j, b: (b, 0, j)),
                  pl.BlockSpec((Ls, cb), lambda j, b: (row_blk0 + b, j)),
                  pl.BlockSpec((HY_SHORT, cb), lambda j, b: (0, j)),
                  pl.BlockSpec((1, cb), lambda j, b: (0, j))],
        out_specs=pl.BlockSpec((Ls, cb), lambda j, b: (b, j)),
        out_shape=jax.ShapeDtypeStruct((B * Ls, D), BF16),
        compiler_params=_params(("parallel", "arbitrary")),
        name="hyena_idft",
    )(inv, y, z, conv_w, conv_b)


def hyena_mixer(h, u, mod, layer, p, geo, need_ctx, norm):
    B, C, L, rows_all, rows_out = geo
    D = h_width(h)
    z = plain_matmul(u, p["w_in"].astype(BF16), rows_out, COL_TILE, bias=p["b_in"], name="hyena_in")
    y = hyena_segment(z, p, L, 0, B, D)
    if need_ctx:
        y = (y, hyena_segment(z, p, C, B * L // C, B, D))
    return gated_matmul(y, p["w_out"].astype(BF16), h, mod, layer, 2, _out_rows(rows_out), bn=D, bias=p["b_out"],
                        norm=norm, name="hyena_out")


def kernel(x, c, ctx, c_ctx, ada_w, ada_b, norm_g, mlp_w1, mlp_w2, final_norm_g, mla_w_dq, mla_q_norm_g, mla_w_uq, mla_w_dkv, mla_kv_norm_g, mla_w_ukv, mla_w_o, hy_w_in, hy_b_in, hy_conv_w, hy_conv_b, hy_filt_w1, hy_filt_b1, hy_filt_w2, hy_filt_b2, hy_filt_w3, hy_filt_b3, hy_filt_freq, hy_filt_wout, hy_filt_bias, hy_w_out, hy_b_out, df_w_qkv, df_lambda, df_subln_g, df_w_o):
    B, L, D = x.shape
    C = ctx.shape[1]
    depth = ada_w.shape[0]
    bm = min(ROW_TILE, B * C)
    rows_all = Rows(B, C, L, bm)
    rows_lat = Rows(B, C, L, bm, lat_only=True)

    mod = adaln_table(jnp.concatenate([c_ctx[None, :], c], axis=0), ada_w, ada_b)
    norm_g4 = norm_g.reshape(depth, 2, 1, D)
    w1s, w2s = mlp_w1.astype(BF16), mlp_w2.astype(BF16)
    h = (x.reshape(B * L, D), ctx.reshape(B * C, D))
    u = norm_modulate(h, norm_g4, mod, 0, 0, rows_all)

    for i in range(depth):
        need_ctx = i < depth - 1
        rows_out = rows_all if need_ctx else rows_lat
        geo = (B, C, L, rows_all, rows_out)
        kind, j = i % N_MIXERS, i // N_MIXERS
        mlp_norm = ("modulate", norm_g4, i, 1)
        if kind == 0:
            p = dict(w_dq=mla_w_dq[j], q_g=mla_q_norm_g[j], w_uq=mla_w_uq[j], w_dkv=mla_w_dkv[j],
                     kv_g=mla_kv_norm_g[j], w_ukv=mla_w_ukv[j], w_o=mla_w_o[j])
            h, u = mla_mixer(h, u, mod, i, p, geo, need_ctx, mlp_norm)
        elif kind == 1:
            p = dict(w_in=hy_w_in[j], b_in=hy_b_in[j], conv_w=hy_conv_w[j], conv_b=hy_conv_b[j],
                     f_w1=hy_filt_w1[j], f_b1=hy_filt_b1[j], f_w2=hy_filt_w2[j], f_b2=hy_filt_b2[j],
                     f_w3=hy_filt_w3[j], f_b3=hy_filt_b3[j], f_freq=hy_filt_freq[j], f_wout=hy_filt_wout[j],
                     f_bias=hy_filt_bias[j], w_out=hy_w_out[j], b_out=hy_b_out[j])
            h, u = hyena_mixer(h, u, mod, i, p, geo, need_ctx, mlp_norm)
        else:
            lambda_init = 0.8 - 0.6 * math.exp(-0.3 * i)
            p = dict(w_qkv=df_w_qkv[j], lambdas=df_lambda[j], subln_g=df_subln_g[j], w_o=df_w_o[j])
            h, u = diff_mixer(h, u, mod, i, p, geo, need_ctx, lambda_init, mlp_norm)
        next_norm = ("modulate", norm_g4, i + 1, 0) if need_ctx else ("final", final_norm_g)
        res = mlp(h, u, mod, i, (w1s, i), (w2s, i), rows_out, next_norm)
        if need_ctx:
            h, u = res

    return res.reshape(B, L, D)
```

```python
import functools
import math

import jax
import jax.numpy as jnp
import numpy as np
from jax import lax
from jax.experimental import pallas as pl
from jax.experimental.pallas import tpu as pltpu

F32 = jnp.float32
BF16 = jnp.bfloat16
HIGHEST = lax.Precision.HIGHEST
LOG2E = math.log2(math.e)

GRID_W = 64
ROPE_THETA = 10000.0
NORM_EPS = 1e-6
N_MIXERS = 3

MLA_NOPE = 128
MLA_ROPE = 64
MLA_V = 128
MLA_KV_RANK = 512
MLA_Q_TILE = 1024

HY_SHORT = 3
HY_EMB_DIM = 33
HY_EMB_PAD = 64
HY_TARGET = 1e-2
HY_FAST_PCT = 0.3
HY_SLOW_PCT = 1.5

DF_HEAD_DIM = 128
DF_SUBLN_EPS = 1e-5
DF_Q_TILE = 512

LANES = 128
MXU_COLS = 256
VMEM_LIMIT_MB = 56

ROW_TILE = 1024
COL_TILE = 1024
NORM_ROW_TILE = 512
K_TILE = 1024


def _params(semantics, vmem_mb=VMEM_LIMIT_MB):
    return pltpu.CompilerParams(dimension_semantics=semantics, vmem_limit_bytes=vmem_mb << 20)


class Rows:
    def __init__(self, B, C, L, bm, lat_only=False):
        assert L % bm == 0 and (B * C) % bm == 0
        self.geometry = (B, C, L, lat_only)
        self.bm = bm
        self.tiles_per_batch = L // bm
        self.n_lat = B * L // bm
        self.n = self.n_lat + (0 if lat_only else B * C // bm)

    def with_bm(self, bm):
        B, C, L, lat_only = self.geometry
        return Rows(B, C, L, bm, lat_only)

    def mod_row(self, i):
        return jnp.where(i < self.n_lat, 1 + i // self.tiles_per_batch, 0)

    def pos_block(self, i):
        return jnp.where(i < self.n_lat, i % self.tiles_per_batch, self.tiles_per_batch)


def _mod_spec(layer, chunk, rows, bn, col_of):
    return pl.BlockSpec((None, None, None, 1, bn),
                        lambda *g: (layer, chunk, rows.mod_row(g[0]), 0, col_of(*g)))


def _w_shape(w):
    return w[0].shape[1:] if isinstance(w, tuple) else w.shape


def _row_operand(arr, rows, width, col_of):
    bm = rows.bm
    n_lat = rows.n_lat
    if isinstance(arr, tuple) and rows.n == n_lat:
        arr = arr[0]
    if not isinstance(arr, tuple):
        return [(arr, (bm, width), lambda *g: (g[0], col_of(*g)))], lambda refs: refs[0][...]
    specs = [(arr[0], (bm, width), lambda *g: (jnp.minimum(g[0], n_lat - 1), col_of(*g))),
             (arr[1], (bm, width), lambda *g: (jnp.maximum(g[0] - n_lat, 0), col_of(*g)))]
    return specs, lambda refs: jnp.where(pl.program_id(0) < n_lat, refs[0][...], refs[1][...])


def fused_matmul(a, w, extras, outs, epilogue, *, rows, bn, bk=None, name):
    K, N = _w_shape(w)
    bk = bk or K
    nk = K // bk
    assert K % bk == 0 and N % bn == 0
    a_specs, read_a = _row_operand(a, rows, bk, lambda i, j, k: k)
    n_a = len(a_specs)
    in_specs = [pl.BlockSpec(bs, im) for _, bs, im in a_specs]
    if isinstance(w, tuple):
        w, w_layer = w
        in_specs += [pl.BlockSpec((None, bk, bn), lambda i, j, k: (w_layer, k, j))]
    else:
        in_specs += [pl.BlockSpec((bk, bn), lambda i, j, k: (k, j))]
    in_specs += [pl.BlockSpec(bs, im) for _, bs, im in extras]
    n_ex = len(extras)
    n_out = len(outs)

    def kern(*refs):
        w_ref = refs[n_a]
        ex = refs[n_a + 1:n_a + 1 + n_ex]
        out = refs[n_a + 1 + n_ex:n_a + 1 + n_ex + n_out]

        def dot():
            return jnp.dot(read_a(refs[:n_a]), w_ref[...], preferred_element_type=F32)

        if nk == 1:
            epilogue(dot(), ex, out)
        else:
            acc_ref = out[0]
            k = pl.program_id(2)

            @pl.when(k == 0)
            def _():
                acc_ref[...] = dot()

            @pl.when((k > 0) & (k < nk - 1))
            def _():
                acc_ref[...] += dot()

            @pl.when(k == nk - 1)
            def _():
                epilogue(acc_ref[...] + dot(), ex, out)

    res = pl.pallas_call(
        kern,
        grid=(rows.n, N // bn, nk),
        in_specs=in_specs,
        out_specs=[pl.BlockSpec(bs, im) for _, _, bs, im in outs],
        out_shape=[jax.ShapeDtypeStruct(s, d) for s, d, _, _ in outs],
        compiler_params=_params(("parallel", "arbitrary", "arbitrary")),
        name=name,
    )(*[e[0] for e in a_specs], w, *[e[0] for e in extras])
    return res


def _rot_half(x, quarter):
    n = x.shape[-1]
    lane = lax.broadcasted_iota(jnp.int32, x.shape, x.ndim - 1)
    first = (lane % (2 * quarter)) < quarter
    return jnp.where(first, -pltpu.roll(x, n - quarter, x.ndim - 1), pltpu.roll(x, quarter, x.ndim - 1))


def _rms(x, g, eps):
    return x * lax.rsqrt(jnp.mean(x * x, axis=-1, keepdims=True) + eps) * g


def gated_matmul(a, w, resid, mod, layer, chunk, rows, *, name, bn=COL_TILE, bk=None, bias=None, norm=None):
    T = rows.n * rows.bm
    bm = rows.bm
    N = _w_shape(w)[1]
    res_specs, read_res = _row_operand(resid, rows, bn, lambda i, j, k: j)
    n_res = len(res_specs)
    extras = res_specs + [(mod,) + _spec_parts(_mod_spec(layer, chunk, rows, bn, lambda i, j, k: j))]
    if bias is not None:
        extras.append((bias.reshape(1, N), (1, bn), lambda i, j, k: (0, j)))
    n_fix = len(extras)
    tile = lambda i, j, k: (i, j)
    outs = [((T, N), F32, (bm, bn), tile)]
    if norm is not None:
        assert bn == N
        if norm[0] == "modulate":
            _, g4, nl, which = norm
            extras += [(g4, (None, None, 1, N), lambda i, j, k: (nl, which, 0, 0)),
                       (mod,) + _spec_parts(_mod_spec(nl, 3 * which, rows, N, lambda i, j, k: 0)),
                       (mod,) + _spec_parts(_mod_spec(nl, 3 * which + 1, rows, N, lambda i, j, k: 0))]
            outs.append(((T, N), BF16, (bm, bn), tile))
        else:
            extras.append((norm[1].reshape(1, N), (1, N), lambda i, j, k: (0, 0)))

    def epi(acc, ex, out):
        y = acc if bias is None else acc + ex[n_res + 1][...]
        hn = read_res(ex[:n_res]) + ex[n_res][...] * y
        if norm is None:
            out[0][...] = hn
        elif norm[0] == "modulate":
            g_ref, sh_ref, sc_ref = ex[n_fix:]
            out[0][...] = hn
            out[1][...] = (_rms(hn, g_ref[...], NORM_EPS) * (1.0 + sc_ref[...]) + sh_ref[...]).astype(BF16)
        else:
            out[0][...] = _rms(hn, ex[n_fix][...], NORM_EPS)

    res = fused_matmul(a, w, extras, outs, epi, rows=rows, bn=bn, bk=bk, name=name)
    return res if len(res) > 1 else res[0]


def _spec_parts(spec):
    return spec.block_shape, spec.index_map


def h_width(h):
    return (h[0] if isinstance(h, tuple) else h).shape[1]


def _out_rows(rows):
    return rows.with_bm(min(NORM_ROW_TILE, rows.bm))


def adaln_table(cvec, ada_w, ada_b):
    depth, D, _ = ada_w.shape
    R = cvec.shape[0]

    def kern(c_ref, w_ref, b_ref, o_ref):
        c = c_ref[...]
        s = c / (1.0 + jnp.exp(-c))
        o_ref[...] = jnp.dot(s.astype(BF16), w_ref[...].astype(BF16), preferred_element_type=F32) + b_ref[...]

    out = pl.pallas_call(
        kern,
        grid=(depth, 6),
        in_specs=[pl.BlockSpec((R, D), lambda l, j: (0, 0)),
                  pl.BlockSpec((None, D, D), lambda l, j: (l, 0, j)),
                  pl.BlockSpec((None, 1, D), lambda l, j: (l, 0, j))],
        out_specs=pl.BlockSpec((None, None, R, D), lambda l, j: (l, j, 0, 0)),
        out_shape=jax.ShapeDtypeStruct((depth, 6, R, D), F32),
        compiler_params=_params(("parallel", "arbitrary")),
        name="adaln_table",
    )(cvec, ada_w, ada_b.reshape(depth, 1, 6 * D))
    return out.reshape(depth, 6, R, 1, D)


def norm_modulate(h, norm_g4, mod, layer, which, rows):
    D = norm_g4.shape[-1]
    bm = rows.bm
    h_specs, read_h = _row_operand(h, rows, D, lambda i: 0)
    n_h = len(h_specs)

    def kern(*refs):
        g_ref, sh_ref, sc_ref, o_ref = refs[n_h:]
        y = _rms(read_h(refs[:n_h]), g_ref[...], NORM_EPS)
        o_ref[...] = (y * (1.0 + sc_ref[...]) + sh_ref[...]).astype(BF16)

    return pl.pallas_call(
        kern,
        grid=(rows.n,),
        in_specs=[pl.BlockSpec(bs, im) for _, bs, im in h_specs]
        + [pl.BlockSpec((None, None, 1, D), lambda i: (layer, which, 0, 0)),
           _mod_spec(layer, 3 * which, rows, D, lambda i: 0),
           _mod_spec(layer, 3 * which + 1, rows, D, lambda i: 0)],
        out_specs=pl.BlockSpec((bm, D), lambda i: (i, 0)),
        out_shape=jax.ShapeDtypeStruct((rows.n * bm, D), BF16),
        compiler_params=_params(("parallel",)),
        name="norm_modulate",
    )(*[e[0] for e in h_specs], norm_g4, mod, mod)


def mlp(h, u, mod, layer, w1, w2, rows, norm):
    T = rows.n * rows.bm
    F, D = _w_shape(w2)

    def relu2(acc, ex, out):
        r = jnp.maximum(acc, 0.0)
        out[0][...] = (r * r).astype(BF16)

    hid = fused_matmul(u, w1, [], [((T, F), BF16, (rows.bm, COL_TILE), lambda i, j, k: (i, j))], relu2,
                       rows=rows, bn=COL_TILE, name="mlp_up")[0]
    return gated_matmul(hid, w2, h, mod, layer, 5, rows, bn=D, bk=K_TILE, norm=norm, name="mlp_down")


def axial_rope_tables(L, rot_dim):
    rows = L // GRID_W
    row = jnp.repeat(jnp.arange(rows, dtype=F32), GRID_W)
    col = jnp.tile(jnp.arange(GRID_W, dtype=F32), rows)
    pos = jnp.stack([row, col], axis=-1)
    n_freq = rot_dim // 4
    inv_freq = ROPE_THETA ** (-jnp.arange(n_freq, dtype=F32) / n_freq)
    ang = pos[:, :, None, None] * inv_freq
    ang = jnp.broadcast_to(ang, (L, 2, 2, n_freq)).reshape(L, rot_dim)
    return jnp.cos(ang), jnp.sin(ang)


def mla_rope_tables(L, bm):
    cos, sin = axial_rope_tables(L, MLA_ROPE)
    reps = LANES // MLA_ROPE
    cos = jnp.pad(jnp.tile(cos, (1, reps)), ((0, bm), (0, 0)), constant_values=1.0)
    sin = jnp.pad(jnp.tile(sin, (1, reps)), ((0, bm), (0, 0)))
    return cos, sin


def mla_down(u, w_dq, q_g, w_dkv, kv_g, cos, sin, rows):
    T = rows.n * rows.bm
    bm = rows.bm
    qr = w_dq.shape[1]
    wd = jnp.concatenate([w_dq, w_dkv] + [w_dkv[:, MLA_KV_RANK:]] * (LANES // MLA_ROPE - 1), axis=1).astype(BF16)
    n_all = wd.shape[1]
    c0 = qr + MLA_KV_RANK

    def epi(acc, ex, out):
        qg_ref, kvg_ref, cos_ref, sin_ref = ex
        out[0][...] = _rms(acc[:, :qr], qg_ref[...], NORM_EPS).astype(BF16)
        out[1][...] = _rms(acc[:, qr:c0], kvg_ref[...], NORM_EPS).astype(BF16)
        kr = acc[:, c0:]
        out[2][...] = (kr * cos_ref[...] + _rot_half(kr, MLA_ROPE // 4) * sin_ref[...]).astype(BF16)

    extras = [(q_g.reshape(1, qr), (1, qr), lambda i, j, k: (0, 0)),
              (kv_g.reshape(1, MLA_KV_RANK), (1, MLA_KV_RANK), lambda i, j, k: (0, 0)),
              (cos, (bm, LANES), lambda i, j, k: (rows.pos_block(i), 0)),
              (sin, (bm, LANES), lambda i, j, k: (rows.pos_block(i), 0))]
    outs = [((T, qr), BF16, (bm, qr), lambda i, j, k: (i, 0)),
            ((T, MLA_KV_RANK), BF16, (bm, MLA_KV_RANK), lambda i, j, k: (i, 0)),
            ((T, LANES), BF16, (bm, LANES), lambda i, j, k: (i, 0))]
    return fused_matmul(u, wd, extras, outs, epi, rows=rows, bn=n_all, name="mla_down")


def mla_queries(cq, w_uq, cos, sin, rows, heads):
    T = rows.n * rows.bm
    bm = rows.bm
    qr = w_uq.shape[0]
    hd = MLA_NOPE + MLA_ROPE
    w = w_uq.reshape(qr, heads, hd)
    w_nope = w[:, :, :MLA_NOPE].reshape(qr, heads * MLA_NOPE).astype(BF16)
    w_rope = w[:, :, MLA_NOPE:].reshape(qr, heads * MLA_ROPE).astype(BF16)
    scale = hd ** -0.5 * LOG2E
    tile = lambda i, j, k: (i, j)

    def epi_nope(acc, ex, out):
        out[0][...] = (acc * scale).astype(BF16)

    q_nope = fused_matmul(cq, w_nope, [], [((T, heads * MLA_NOPE), BF16, (bm, heads * MLA_NOPE), tile)], epi_nope,
                          rows=rows, bn=heads * MLA_NOPE, name="mla_q_nope")[0]

    def epi_rope(acc, ex, out):
        cos_ref, sin_ref = ex
        cs, sn = cos_ref[...] * scale, sin_ref[...] * scale
        for c in range(0, heads * MLA_ROPE, LANES):
            r = acc[:, c:c + LANES]
            out[0][:, c:c + LANES] = (r * cs + _rot_half(r, MLA_ROPE // 4) * sn).astype(BF16)

    extras = [(cos, (bm, LANES), lambda i, j, k: (rows.pos_block(i), 0)),
              (sin, (bm, LANES), lambda i, j, k: (rows.pos_block(i), 0))]
    q_rope = fused_matmul(cq, w_rope, extras, [((T, heads * MLA_ROPE), BF16, (bm, heads * MLA_ROPE), tile)],
                          epi_rope, rows=rows, bn=heads * MLA_ROPE, name="mla_q_rope")[0]
    return q_nope, q_rope


def plain_matmul(a, w, rows, bn, *, name, out_dtype=BF16, bias=None):
    T = rows.n * rows.bm
    N = w.shape[1]
    extras = [] if bias is None else [(bias.reshape(1, N), (1, bn), lambda i, j, k: (0, j))]

    def epi(acc, ex, out):
        y = acc if bias is None else acc + ex[0][...]
        out[0][...] = y.astype(out_dtype)

    return fused_matmul(a, w, extras, [((T, N), out_dtype, (rows.bm, bn), lambda i, j, k: (i, j))], epi,
                        rows=rows, bn=bn, name=name)[0]


def mla_attention(q_nope, q_rope, kv, kr, B, C, L, heads, ctx_queries):
    hp = LANES // MLA_ROPE
    tq = C if ctx_queries else MLA_Q_TILE
    n_t = 1 if ctx_queries else L // tq
    ctx_blk = B * L // C
    S = C if ctx_queries else C + L
    kvw = 2 * LANES
    q_row = (lambda b, g, t: ctx_blk + b) if ctx_queries else (lambda b, g, t: b * n_t + t)

    def kern(*refs):
        if ctx_queries:
            qn_ref, qr_ref, kvc_ref, krc_ref, o_ref, k_s, v_s = refs
            parts = [(kvc_ref, krc_ref, 0, C)]
        else:
            qn_ref, qr_ref, kvl_ref, kvc_ref, krl_ref, krc_ref, o_ref, k_s, v_s = refs
            parts = [(kvc_ref, krc_ref, 0, C), (kvl_ref, krl_ref, C, S)]

        @pl.when(pl.program_id(2) == 0)
        def _():
            half = lax.broadcasted_iota(jnp.int32, (1, LANES), 1) // MLA_ROPE
            for hh in range(hp):
                c0 = hh * kvw
                for kv_ref, kr_ref, lo, hi in parts:
                    k_s[hh, lo:hi, :LANES] = kv_ref[:, c0:c0 + LANES]
                    kr = kr_ref[...]
                    k_s[hh, lo:hi, LANES:] = jnp.where(half == hh, kr, jnp.zeros_like(kr))
                    v_s[hh, lo:hi, :MLA_V] = kv_ref[:, c0 + LANES:c0 + kvw]
                v_s[hh, :, MLA_V:] = jnp.ones((S, MXU_COLS - MLA_V), BF16)

        ss = [lax.dot_general(jnp.concatenate([qn_ref[:, hh * MLA_NOPE:(hh + 1) * MLA_NOPE], qr_ref[...]], axis=1),
                              k_s[hh], (((1,), (1,)), ((), ())), preferred_element_type=F32) for hh in range(hp)]
        ps = [jnp.exp2(s - jnp.max(s, axis=-1, keepdims=True)).astype(BF16) for s in ss]
        for hh in range(hp):
            ov = jnp.dot(ps[hh], v_s[hh], preferred_element_type=F32)
            o_ref[:, hh * MLA_V:(hh + 1) * MLA_V] = (ov[:, :MLA_V] / ov[:, MLA_V:2 * MLA_V]).astype(BF16)

    lat_specs = [pl.BlockSpec((L, hp * kvw), lambda b, g, t: (b, g)),
                 pl.BlockSpec((L, LANES), lambda b, g, t: (b, 0))]
    ctx_specs = [pl.BlockSpec((C, hp * kvw), lambda b, g, t: (ctx_blk + b, g)),
                 pl.BlockSpec((C, LANES), lambda b, g, t: (ctx_blk + b, 0))]
    q_specs = [pl.BlockSpec((tq, hp * MLA_NOPE), lambda b, g, t: (q_row(b, g, t), g)),
               pl.BlockSpec((tq, hp * MLA_ROPE), lambda b, g, t: (q_row(b, g, t), g))]
    if ctx_queries:
        in_specs, args = q_specs + ctx_specs, (q_nope, q_rope, kv, kr)
    else:
        in_specs = q_specs + [lat_specs[0], ctx_specs[0], lat_specs[1], ctx_specs[1]]
        args = (q_nope, q_rope, kv, kv, kr, kr)
    return pl.pallas_call(
        kern,
        grid=(B, heads // hp, n_t),
        in_specs=in_specs,
        out_specs=pl.BlockSpec((tq, hp * MLA_V), lambda b, g, t: (b * n_t + t, g)),
        out_shape=jax.ShapeDtypeStruct((B * n_t * tq, heads * MLA_V), BF16),
        scratch_shapes=[pltpu.VMEM((hp, S, MXU_COLS), BF16), pltpu.VMEM((hp, S, MXU_COLS), BF16)],
        compiler_params=_params(("parallel", "parallel", "arbitrary")),
        name="mla_attention_ctx" if ctx_queries else "mla_attention",
    )(*args)


def mla_mixer(h, u, mod, layer, p, geo, need_ctx, norm):
    B, C, L, rows_all, rows_out = geo
    heads = p["w_uq"].shape[1] // (MLA_NOPE + MLA_ROPE)
    cos, sin = mla_rope_tables(L, rows_all.bm)
    cq, ckv, kr = mla_down(u, p["w_dq"], p["q_g"], p["w_dkv"], p["kv_g"], cos, sin, rows_all)
    kv = plain_matmul(ckv, p["w_ukv"].astype(BF16), rows_all, p["w_ukv"].shape[1], name="mla_kv")
    q_nope, q_rope = mla_queries(cq, p["w_uq"], cos, sin, rows_out, heads)
    o = mla_attention(q_nope, q_rope, kv, kr, B, C, L, heads, False)
    if need_ctx:
        o = (o, mla_attention(q_nope, q_rope, kv, kr, B, C, L, heads, True))
    return gated_matmul(o, p["w_o"].astype(BF16), h, mod, layer, 2, _out_rows(rows_out), bn=h_width(h), norm=norm,
                        name="mla_out")


def _diff_pair_layout():
    quarter = DF_HEAD_DIM // 4
    n = np.arange(2 * DF_HEAD_DIM)
    hf, c, a, r = n // DF_HEAD_DIM, (n % DF_HEAD_DIM) // (2 * quarter), (n % (2 * quarter)) // quarter, n % quarter
    head_perm = c * DF_HEAD_DIM + a * 2 * quarter + hf * quarter + r
    table_cols = (a * 2 * quarter + r)[:DF_HEAD_DIM]
    return head_perm, table_cols


def diff_rope_tables(L, bm):
    cos, sin = axial_rope_tables(L, DF_HEAD_DIM)
    _, cols = _diff_pair_layout()
    cos = jnp.pad(cos[:, cols], ((0, bm), (0, 0)), constant_values=1.0)
    sin = jnp.pad(sin[:, cols], ((0, bm), (0, 0)))
    return cos, sin


def diff_qkv(u, w_qkv, cos, sin, rows):
    T = rows.n * rows.bm
    bm = rows.bm
    D = w_qkv.shape[0]
    bn = COL_TILE
    hw = 2 * DF_HEAD_DIM
    scale = DF_HEAD_DIM ** -0.5 * LOG2E
    n_q = D // bn
    head_perm, _ = _diff_pair_layout()
    qk_cols = (np.arange(2 * D) // hw * hw)[:, None].reshape(-1, hw) + head_perm[None, :]
    cols = np.concatenate([qk_cols.reshape(-1), np.arange(2 * D, 3 * D)])
    w = w_qkv[:, cols].astype(BF16)

    def epi(acc, ex, out):
        cos_ref, sin_ref = ex
        mul = jnp.where(pl.program_id(1) < n_q, scale, 1.0).astype(F32)
        cs, sn = cos_ref[...] * mul, sin_ref[...] * mul
        for s in range(bn // hw):
            x1 = acc[:, s * hw:s * hw + LANES]
            x2 = acc[:, s * hw + LANES:(s + 1) * hw]
            out[0][:, s * hw:s * hw + LANES] = (x1 * cs - x2 * sn).astype(BF16)
            out[0][:, s * hw + LANES:(s + 1) * hw] = (x2 * cs + x1 * sn).astype(BF16)

    def table_block(i, j, k):
        return jnp.where(j < 2 * n_q, rows.pos_block(i), rows.tiles_per_batch), 0

    extras = [(cos, (bm, LANES), table_block), (sin, (bm, LANES), table_block)]
    outs = [((T, 3 * D), BF16, (bm, bn), lambda i, j, k: (i, j))]
    return fused_matmul(u, w, extras, outs, epi, rows=rows, bn=bn, name="diff_qkv")[0]


def diff_attention(qkv, lambdas, subln_g, lambda_init, B, C, L, D, ctx_queries):
    hd = DF_HEAD_DIM
    hw = 2 * hd
    heads = D // hw
    hp = 2
    tq = C if ctx_queries else DF_Q_TILE
    n_t = 1 if ctx_queries else L // tq
    ctx_blk = B * L // C
    S = C if ctx_queries else C + L
    kcol = D // (hp * hw)
    vcol = 2 * D // (hp * hw)
    q_row = (lambda b, g, t: ctx_blk + b) if ctx_queries else (lambda b, g, t: b * n_t + t)

    def kern(*refs):
        if ctx_queries:
            q_ref, kc_ref, vc_ref, lam_ref, g_ref, o_ref, k_s, v_s = refs
            parts = [(kc_ref, vc_ref, 0, C)]
        else:
            q_ref, kl_ref, kc_ref, vl_ref, vc_ref, lam_ref, g_ref, o_ref, k_s, v_s = refs
            parts = [(kc_ref, vc_ref, 0, C), (kl_ref, vl_ref, C, S)]

        @pl.when(pl.program_id(2) == 0)
        def _():
            map0 = (lax.broadcasted_iota(jnp.int32, (1, hw), 1) % hd) < hd // 2
            for k_ref, v_ref, lo, hi in parts:
                for hh in range(hp):
                    k = k_ref[:, hh * hw:(hh + 1) * hw]
                    k_s[2 * hh, lo:hi, :] = jnp.where(map0, k, jnp.zeros_like(k))
                    k_s[2 * hh + 1, lo:hi, :] = jnp.where(map0, jnp.zeros_like(k), k)
                    v_s[hh, lo:hi, :] = v_ref[:, hh * hw:(hh + 1) * hw]

        lf = lam_ref[...]
        lam = (jnp.exp(jnp.sum(lf[0:1] * lf[1:2], axis=-1, keepdims=True))
               - jnp.exp(jnp.sum(lf[2:3] * lf[3:4], axis=-1, keepdims=True)) + lambda_init)

        nc = 2 * hp
        ss = [lax.dot_general(q_ref[:, (c // 2) * hw:(c // 2 + 1) * hw], k_s[c],
                              (((1,), (1,)), ((), ())), preferred_element_type=F32) for c in range(nc)]
        ms = [jnp.max(s, axis=-1, keepdims=True) for s in ss]
        os_, ls = [0.0] * nc, [0.0] * nc
        for lo in range(0, S, MXU_COLS):
            for c in range(nc):
                p = jnp.exp2(ss[c][:, lo:lo + MXU_COLS] - ms[c])
                ls[c] = ls[c] + jnp.sum(p[:, :LANES] + p[:, LANES:], axis=-1, keepdims=True)
                os_[c] = os_[c] + jnp.dot(p.astype(BF16), v_s[c // 2, lo:lo + MXU_COLS, :],
                                          preferred_element_type=F32)
        for hh in range(hp):
            o = os_[2 * hh] / ls[2 * hh] - lam * (os_[2 * hh + 1] / ls[2 * hh + 1])
            o_ref[:, hh * hw:(hh + 1) * hw] = (_rms(o, g_ref[...], DF_SUBLN_EPS)
                                               * (1.0 - lambda_init)).astype(BF16)

    bw = hp * hw
    q_spec = pl.BlockSpec((tq, bw), lambda b, g, t: (q_row(b, g, t), g))
    lat_specs = [pl.BlockSpec((L, bw), lambda b, g, t: (b, kcol + g)),
                 pl.BlockSpec((L, bw), lambda b, g, t: (b, vcol + g))]
    ctx_specs = [pl.BlockSpec((C, bw), lambda b, g, t: (ctx_blk + b, kcol + g)),
                 pl.BlockSpec((C, bw), lambda b, g, t: (ctx_blk + b, vcol + g))]
    par_specs = [pl.BlockSpec((4, hd), lambda b, g, t: (0, 0)), pl.BlockSpec((1, hw), lambda b, g, t: (0, 0))]
    if ctx_queries:
        in_specs, args = [q_spec] + ctx_specs, (qkv, qkv, qkv)
    else:
        in_specs = [q_spec, lat_specs[0], ctx_specs[0], lat_specs[1], ctx_specs[1]]
        args = (qkv, qkv, qkv, qkv, qkv)
    return pl.pallas_call(
        kern,
        grid=(B, heads // hp, n_t),
        in_specs=in_specs + par_specs,
        out_specs=pl.BlockSpec((tq, bw), lambda b, g, t: (b * n_t + t, g)),
        out_shape=jax.ShapeDtypeStruct((B * n_t * tq, D), BF16),
        scratch_shapes=[pltpu.VMEM((2 * hp, S, hw), BF16), pltpu.VMEM((hp, S, hw), BF16)],
        compiler_params=_params(("parallel", "parallel", "arbitrary")),
        name="diff_attention_ctx" if ctx_queries else "diff_attention",
    )(*args, lambdas, subln_g.reshape(1, hw))


def diff_mixer(h, u, mod, layer, p, geo, need_ctx, lambda_init, norm):
    B, C, L, rows_all, rows_out = geo
    D = h_width(h)
    cos, sin = diff_rope_tables(L, rows_all.bm)
    qkv = diff_qkv(u, p["w_qkv"], cos, sin, rows_all)
    o = diff_attention(qkv, p["lambdas"], p["subln_g"], lambda_init, B, C, L, D, False)
    if need_ctx:
        o = (o, diff_attention(qkv, p["lambdas"], p["subln_g"], lambda_init, B, C, L, D, True))
    return gated_matmul(o, p["w_o"].astype(BF16), h, mod, layer, 2, _out_rows(rows_out), bn=D, norm=norm,
                        name="diff_out")


def dft_matrices(Ls):
    n = 2 * Ls
    t0n = min(64, Ls)
    t1n = Ls // t0n
    f = jnp.arange(Ls, dtype=jnp.int32)[:, None]
    a1 = ((f * (jnp.arange(t1n, dtype=jnp.int32) * t0n)[None, :]) % n).astype(F32) * (2.0 * math.pi / n)
    a0 = ((f * jnp.arange(t0n, dtype=jnp.int32)[None, :]) % n).astype(F32) * (2.0 * math.pi / n)
    c1, s1 = jnp.cos(a1)[:, :, None], jnp.sin(a1)[:, :, None]
    c0, s0 = jnp.cos(a0)[:, None, :], jnp.sin(a0)[:, None, :]
    cosm = (c1 * c0 - s1 * s0).reshape(Ls, Ls)
    sinm = (s1 * c0 + c1 * s0).reshape(Ls, Ls)
    nyq = jnp.where(jnp.arange(Ls) % 2 == 0, 1.0, -1.0).astype(F32)[None, :]
    imag = jnp.where(f == 0, nyq, -sinm)
    fwd = jnp.concatenate([cosm, imag], axis=0)
    col = jnp.arange(n)
    cscale = jnp.where((col == 0) | (col == Ls), 1.0 / n, 2.0 / n).astype(F32)
    inv = fwd.T * cscale[None, :]
    return fwd.astype(BF16), inv.astype(BF16)


def hyena_filter_time(Ls, p):
    D = p["f_bias"].shape[0]
    order = p["f_w2"].shape[0]
    bands = (HY_EMB_DIM - 1) // 2
    t = jnp.linspace(0.0, 1.0, Ls, dtype=F32)[:, None]
    w = 2.0 * math.pi * jnp.arange(Ls, dtype=F32)[:, None] / Ls
    f = jnp.linspace(1e-4, bands - 1, bands, dtype=F32)
    feats = jnp.concatenate([t, jnp.cos(f * w), -jnp.sin(f * w)], axis=-1)
    feats = jnp.pad(feats, ((0, 0), (0, HY_EMB_PAD - HY_EMB_DIM)))
    w1 = jnp.pad(p["f_w1"], ((0, HY_EMB_PAD - HY_EMB_DIM), (0, 0)))
    deltas = jnp.abs(jnp.linspace(math.log(HY_TARGET) / HY_SLOW_PCT, math.log(HY_TARGET) / HY_FAST_PCT,
                                  D, dtype=F32))
    deltas2 = jnp.concatenate([deltas, deltas]).reshape(1, 2 * D)

    def ffn_kern(x_ref, w1_ref, b1_ref, w2_ref, b2_ref, w3_ref, b3_ref, fr_ref, o_ref):
        dot = functools.partial(jnp.dot, precision=HIGHEST, preferred_element_type=F32)
        fr = fr_ref[...]
        hcur = jnp.sin(fr[0:1] * (dot(x_ref[...], w1_ref[...]) + b1_ref[...]))
        hcur = jnp.sin(fr[1:2] * (dot(hcur, w2_ref[...]) + b2_ref[...]))
        o_ref[...] = jnp.sin(fr[2:3] * (dot(hcur, w3_ref[...]) + b3_ref[...]))

    full = lambda a: pl.BlockSpec(a.shape, lambda: (0,) * a.ndim)
    ffn_in = [feats, w1, p["f_b1"].reshape(1, order), p["f_w2"], p["f_b2"].reshape(1, order),
              p["f_w3"], p["f_b3"].reshape(1, order), p["f_freq"]]
    hff = pl.pallas_call(
        ffn_kern,
        in_specs=[full(a) for a in ffn_in],
        out_specs=pl.BlockSpec((Ls, order), lambda: (0, 0)),
        out_shape=jax.ShapeDtypeStruct((Ls, order), F32),
        name="hyena_filter_ffn",
    )(*ffn_in)

    bn = COL_TILE // 2

    def out_kern(h_ref, w_ref, d_ref, o_ref):
        tt = lax.broadcasted_iota(jnp.int32, (Ls, 1), 0).astype(F32) * (1.0 / (Ls - 1))
        hw = jnp.dot(h_ref[...], w_ref[...], precision=HIGHEST, preferred_element_type=F32)
        o_ref[...] = hw * jnp.exp(-tt * d_ref[...])

    return pl.pallas_call(
        out_kern,
        grid=(2 * D // bn,),
        in_specs=[pl.BlockSpec((Ls, order), lambda j: (0, 0)),
                  pl.BlockSpec((order, bn), lambda j: (0, j)),
                  pl.BlockSpec((1, bn), lambda j: (0, j))],
        out_specs=pl.BlockSpec((Ls, bn), lambda j: (0, j)),
        out_shape=jax.ShapeDtypeStruct((Ls, 2 * D), F32),
        compiler_params=_params(("parallel",)),
        name="hyena_filter_out",
    )(hff, p["f_wout"], deltas2)


def hyena_filter_spectrum(hfb, f_bias, fwd, Ls, D):
    n = 2 * Ls
    cb = LANES
    ncb = D // cb

    def kern(w_ref, hf_ref, hb_ref, fb_ref, o_ref):
        row = lax.broadcasted_iota(jnp.int32, (Ls, 1), 0)
        hb = jnp.where(row == 0, 0.0, hb_ref[...])
        ab = jnp.dot(w_ref[...], jnp.concatenate([hf_ref[...], hb], axis=1).astype(BF16),
                     preferred_element_type=F32)
        a, b = ab[:, :cb], ab[:, cb:]
        frow = lax.broadcasted_iota(jnp.int32, (n, 1), 0)
        real = frow <= Ls
        o_ref[...] = a + jnp.where(real, b + fb_ref[...], -b)

    return pl.pallas_call(
        kern,
        grid=(ncb,),
        in_specs=[pl.BlockSpec((n, Ls), lambda j: (0, 0), pipeline_mode=pl.Buffered(1)),
                  pl.BlockSpec((Ls, cb), lambda j: (0, j)),
                  pl.BlockSpec((Ls, cb), lambda j: (0, ncb + j)),
                  pl.BlockSpec((1, cb), lambda j: (0, j))],
        out_specs=pl.BlockSpec((n, cb), lambda j: (0, j)),
        out_shape=jax.ShapeDtypeStruct((n, D), F32),
        compiler_params=_params(("parallel",)),
        name="hyena_filter_spectrum",
    )(fwd, hfb, hfb, f_bias.reshape(1, D))


def _short_conv(x, w, b):
    n = x.shape[0]
    row = lax.broadcasted_iota(jnp.int32, (n, 1), 0)
    prev = jnp.where(row == 0, 0.0, pltpu.roll(x, 1, 0))
    nxt = jnp.where(row == n - 1, 0.0, pltpu.roll(x, n - 1, 0))
    return prev * w[0:1] + x * w[1:2] + nxt * w[2:3] + b


def hyena_segment(z, p, Ls, row_blk0, B, D):
    n = 2 * Ls
    cb = MXU_COLS
    ncb = D // cb
    fwd, inv = dft_matrices(Ls)
    hfb = hyena_filter_time(Ls, p)
    kf = hyena_filter_spectrum(hfb, p["f_bias"], fwd, Ls, D)
    conv_w, conv_b = p["conv_w"], p["conv_b"].reshape(1, 3 * D)

    def fwd_kern(w_ref, x1_ref, v_ref, cw1_ref, cb1_ref, cwv_ref, cbv_ref, kf_ref, y_ref):
        g = (_short_conv(v_ref[...].astype(F32), cwv_ref[...], cbv_ref[...])
             * _short_conv(x1_ref[...].astype(F32), cw1_ref[...], cb1_ref[...])).astype(BF16)
        u = jnp.dot(w_ref[...], g, preferred_element_type=F32)
        ure, uim = u[:Ls], u[Ls:]
        kre, kim = kf_ref[:Ls, :], kf_ref[Ls:, :]
        first = lax.broadcasted_iota(jnp.int32, (Ls, 1), 0) == 0
        y_ref[:Ls, :] = (ure * kre - jnp.where(first, 0.0, uim * kim)).astype(BF16)
        y_ref[Ls:, :] = jnp.where(first, uim * kim, ure * kim + uim * kre).astype(BF16)

    y = pl.pallas_call(
        fwd_kern,
        grid=(ncb, B),
        in_specs=[pl.BlockSpec((n, Ls), lambda j, b: (0, 0), pipeline_mode=pl.Buffered(1)),
                  pl.BlockSpec((Ls, cb), lambda j, b: (row_blk0 + b, ncb + j)),
                  pl.BlockSpec((Ls, cb), lambda j, b: (row_blk0 + b, 2 * ncb + j)),
                  pl.BlockSpec((HY_SHORT, cb), lambda j, b: (0, ncb + j)),
                  pl.BlockSpec((1, cb), lambda j, b: (0, ncb + j)),
                  pl.BlockSpec((HY_SHORT, cb), lambda j, b: (0, 2 * ncb + j)),
                  pl.BlockSpec((1, cb), lambda j, b: (0, 2 * ncb + j)),
                  pl.BlockSpec((n, cb), lambda j, b: (0, j))],
        out_specs=pl.BlockSpec((None, n, cb), lambda j, b: (b, 0, j)),
        out_shape=jax.ShapeDtypeStruct((B, n, D), BF16),
        compiler_params=_params(("parallel", "arbitrary")),
        name="hyena_dft",
    )(fwd, z, z, conv_w, conv_b, conv_w, conv_b, kf)

    def inv_kern(w_ref, y_ref, x0_ref, cw0_ref, cb0_ref, o_ref):
        conv = jnp.dot(w_ref[...], y_ref[...], preferred_element_type=F32)
        o_ref[...] = (conv * _short_conv(x0_ref[...].astype(F32), cw0_ref[...], cb0_ref[...])).astype(BF16)

    return pl.pallas_call(
        inv_kern,
        grid=(ncb, B),
        in_specs=[pl.BlockSpec((Ls, n), lambda j, b: (0, 0), pipeline_mode=pl.Buffered(1)),
                  pl.BlockSpec((None, n, cb), lambda j, b: (b, 0, j)),
                  pl.BlockSpec((Ls, cb), lambda j, b: (row_blk0 + b, j)),
                  pl.BlockSpec((HY_SHORT, cb), lambda j, b: (0, j)),
                  pl.BlockSpec((1, cb), lambda j, b: (0, j))],
        out_specs=pl.BlockSpec((Ls, cb), lambda j, b: (b, j)),
        out_shape=jax.ShapeDtypeStruct((B * Ls, D), BF16),
        compiler_params=_params(("parallel", "arbitrary")),
        name="hyena_idft",
    )(inv, y, z, conv_w, conv_b)


def hyena_mixer(h, u, mod, layer, p, geo, need_ctx, norm):
    B, C, L, rows_all, rows_out = geo
    D = h_width(h)
    z = plain_matmul(u, p["w_in"].astype(BF16), rows_out, COL_TILE, bias=p["b_in"], name="hyena_in")
    y = hyena_segment(z, p, L, 0, B, D)
    if need_ctx:
        y = (y, hyena_segment(z, p, C, B * L // C, B, D))
    return gated_matmul(y, p["w_out"].astype(BF16), h, mod, layer, 2, _out_rows(rows_out), bn=D, bias=p["b_out"],
                        norm=norm, name="hyena_out")


def kernel(x, c, ctx, c_ctx, ada_w, ada_b, norm_g, mlp_w1, mlp_w2, final_norm_g, mla_w_dq, mla_q_norm_g, mla_w_uq, mla_w_dkv, mla_kv_norm_g, mla_w_ukv, mla_w_o, hy_w_in, hy_b_in, hy_conv_w, hy_conv_b, hy_filt_w1, hy_filt_b1, hy_filt_w2, hy_filt_b2, hy_filt_w3, hy_filt_b3, hy_filt_freq, hy_filt_wout, hy_filt_bias, hy_w_out, hy_b_out, df_w_qkv, df_lambda, df_subln_g, df_w_o):
    B, L, D = x.shape
    C = ctx.shape[1]
    depth = ada_w.shape[0]
    bm = min(ROW_TILE, B * C)
    rows_all = Rows(B, C, L, bm)
    rows_lat = Rows(B, C, L, bm, lat_only=True)

    mod = adaln_table(jnp.concatenate([c_ctx[None, :], c], axis=0), ada_w, ada_b)
    norm_g4 = norm_g.reshape(depth, 2, 1, D)
    w1s, w2s = mlp_w1.astype(BF16), mlp_w2.astype(BF16)
    h = (x.reshape(B * L, D), ctx.reshape(B * C, D))
    u = norm_modulate(h, norm_g4, mod, 0, 0, rows_all)

    for i in range(depth):
        need_ctx = i < depth - 1
        rows_out = rows_all if need_ctx else rows_lat
        geo = (B, C, L, rows_all, rows_out)
        kind, j = i % N_MIXERS, i // N_MIXERS
        mlp_norm = ("modulate", norm_g4, i, 1)
        if kind == 0:
            p = dict(w_dq=mla_w_dq[j], q_g=mla_q_norm_g[j], w_uq=mla_w_uq[j], w_dkv=mla_w_dkv[j],
                     kv_g=mla_kv_norm_g[j], w_ukv=mla_w_ukv[j], w_o=mla_w_o[j])
            h, u = mla_mixer(h, u, mod, i, p, geo, need_ctx, mlp_norm)
        elif kind == 1:
            p = dict(w_in=hy_w_in[j], b_in=hy_b_in[j], conv_w=hy_conv_w[j], conv_b=hy_conv_b[j],
                     f_w1=hy_filt_w1[j], f_b1=hy_filt_b1[j], f_w2=hy_filt_w2[j], f_b2=hy_filt_b2[j],
                     f_w3=hy_filt_w3[j], f_b3=hy_filt_b3[j], f_freq=hy_filt_freq[j], f_wout=hy_filt_wout[j],
                     f_bias=hy_filt_bias[j], w_out=hy_w_out[j], b_out=hy_b_out[j])
            h, u = hyena_mixer(h, u, mod, i, p, geo, need_ctx, mlp_norm)
        else:
            lambda_init = 0.8 - 0.6 * math.exp(-0.3 * i)
            p = dict(w_qkv=df_w_qkv[j], lambdas=df_lambda[j], subln_g=df_subln_g[j], w_o=df_w_o[j])
            h, u = diff_mixer(h, u, mod, i, p, geo, need_ctx, lambda_init, mlp_norm)
        next_norm = ("modulate", norm_g4, i + 1, 0) if need_ctx else ("final", final_norm_g)
        res = mlp(h, u, mod, i, (w1s, i), (w2s, i), rows_out, next_norm)
        if need_ctx:
            h, u = res

    return res.reshape(B, L, D)
```

```python
import functools
import math

import jax
import jax.numpy as jnp
import numpy as np
from jax import lax
from jax.experimental import pallas as pl
from jax.experimental.pallas import tpu as pltpu

F32 = jnp.float32
BF16 = jnp.bfloat16
HIGHEST = lax.Precision.HIGHEST
LOG2E = math.log2(math.e)

GRID_W = 64
ROPE_THETA = 10000.0
NORM_EPS = 1e-6
N_MIXERS = 3

MLA_NOPE = 128
MLA_ROPE = 64
MLA_V = 128
MLA_KV_RANK = 512
MLA_HEAD_PAD = 256
MLA_Q_TILE = 1024

HY_SHORT = 3
HY_EMB_DIM = 33
HY_EMB_PAD = 64
HY_TARGET = 1e-2
HY_FAST_PCT = 0.3
HY_SLOW_PCT = 1.5

DF_HEAD_DIM = 128
DF_SUBLN_EPS = 1e-5
DF_Q_TILE = 512

LANES = 128
MXU_COLS = 256
VMEM_LIMIT_MB = 56

ROW_TILE = 1024
COL_TILE = 2048
NORM_ROW_TILE = 512
K_TILE = 1024


def _params(semantics, vmem_mb=VMEM_LIMIT_MB):
    return pltpu.CompilerParams(dimension_semantics=semantics, vmem_limit_bytes=vmem_mb << 20)


class Rows:
    def __init__(self, B, C, L, bm, lat_only=False):
        assert L % bm == 0 and (B * C) % bm == 0
        self.geometry = (B, C, L, lat_only)
        self.bm = bm
        self.tiles_per_batch = L // bm
        self.n_lat = B * L // bm
        self.n = self.n_lat + (0 if lat_only else B * C // bm)

    def with_bm(self, bm):
        B, C, L, lat_only = self.geometry
        return Rows(B, C, L, bm, lat_only)

    def mod_row(self, i):
        return jnp.where(i < self.n_lat, 1 + i // self.tiles_per_batch, 0)

    def pos_block(self, i):
        return jnp.where(i < self.n_lat, i % self.tiles_per_batch, self.tiles_per_batch)


def _mod_spec(layer, chunk, rows, bn, col_of):
    return pl.BlockSpec((None, None, None, 1, bn),
                        lambda *g: (layer, chunk, rows.mod_row(g[0]), 0, col_of(*g)))


def _w_shape(w):
    return w[0].shape[1:] if isinstance(w, tuple) else w.shape


def _row_operand(arr, rows, width, col_of):
    bm = rows.bm
    n_lat = rows.n_lat
    if isinstance(arr, tuple) and rows.n == n_lat:
        arr = arr[0]
    if not isinstance(arr, tuple):
        return [(arr, (bm, width), lambda *g: (g[0], col_of(*g)))], lambda refs: refs[0][...]
    specs = [(arr[0], (bm, width), lambda *g: (jnp.minimum(g[0], n_lat - 1), col_of(*g))),
             (arr[1], (bm, width), lambda *g: (jnp.maximum(g[0] - n_lat, 0), col_of(*g)))]
    return specs, lambda refs: jnp.where(pl.program_id(0) < n_lat, refs[0][...], refs[1][...])


def fused_matmul(a, w, extras, outs, epilogue, *, rows, bn, bk=None, name):
    K, N = _w_shape(w)
    bk = bk or K
    nk = K // bk
    assert K % bk == 0 and N % bn == 0
    a_specs, read_a = _row_operand(a, rows, bk, lambda i, j, k: k)
    n_a = len(a_specs)
    in_specs = [pl.BlockSpec(bs, im) for _, bs, im in a_specs]
    if isinstance(w, tuple):
        w, w_layer = w
        in_specs += [pl.BlockSpec((None, bk, bn), lambda i, j, k: (w_layer, k, j))]
    else:
        in_specs += [pl.BlockSpec((bk, bn), lambda i, j, k: (k, j))]
    in_specs += [pl.BlockSpec(bs, im) for _, bs, im in extras]
    n_ex = len(extras)
    n_out = len(outs)

    def kern(*refs):
        w_ref = refs[n_a]
        ex = refs[n_a + 1:n_a + 1 + n_ex]
        out = refs[n_a + 1 + n_ex:n_a + 1 + n_ex + n_out]

        def dot():
            return jnp.dot(read_a(refs[:n_a]), w_ref[...], preferred_element_type=F32)

        if nk == 1:
            epilogue(dot(), ex, out)
        else:
            acc_ref = out[0]
            k = pl.program_id(2)

            @pl.when(k == 0)
            def _():
                acc_ref[...] = dot()

            @pl.when((k > 0) & (k < nk - 1))
            def _():
                acc_ref[...] += dot()

            @pl.when(k == nk - 1)
            def _():
                epilogue(acc_ref[...] + dot(), ex, out)

    res = pl.pallas_call(
        kern,
        grid=(rows.n, N // bn, nk),
        in_specs=in_specs,
        out_specs=[pl.BlockSpec(bs, im) for _, _, bs, im in outs],
        out_shape=[jax.ShapeDtypeStruct(s, d) for s, d, _, _ in outs],
        compiler_params=_params(("parallel", "arbitrary", "arbitrary")),
        name=name,
    )(*[e[0] for e in a_specs], w, *[e[0] for e in extras])
    return res


def _rot_half(x, quarter):
    n = x.shape[-1]
    lane = lax.broadcasted_iota(jnp.int32, x.shape, x.ndim - 1)
    first = (lane % (2 * quarter)) < quarter
    return jnp.where(first, -pltpu.roll(x, n - quarter, x.ndim - 1), pltpu.roll(x, quarter, x.ndim - 1))


def _rms(x, g, eps):
    return x * lax.rsqrt(jnp.mean(x * x, axis=-1, keepdims=True) + eps) * g


def gated_matmul(a, w, resid, mod, layer, chunk, rows, *, name, bn=COL_TILE, bk=None, bias=None, norm=None):
    T = rows.n * rows.bm
    bm = rows.bm
    N = _w_shape(w)[1]
    res_specs, read_res = _row_operand(resid, rows, bn, lambda i, j, k: j)
    n_res = len(res_specs)
    extras = res_specs + [(mod,) + _spec_parts(_mod_spec(layer, chunk, rows, bn, lambda i, j, k: j))]
    if bias is not None:
        extras.append((bias.reshape(1, N), (1, bn), lambda i, j, k: (0, j)))
    n_fix = len(extras)
    tile = lambda i, j, k: (i, j)
    outs = [((T, N), F32, (bm, bn), tile)]
    if norm is not None:
        assert bn == N
        if norm[0] == "modulate":
            _, g4, nl, which = norm
            extras += [(g4, (None, None, 1, N), lambda i, j, k: (nl, which, 0, 0)),
                       (mod,) + _spec_parts(_mod_spec(nl, 3 * which, rows, N, lambda i, j, k: 0)),
                       (mod,) + _spec_parts(_mod_spec(nl, 3 * which + 1, rows, N, lambda i, j, k: 0))]
            outs.append(((T, N), BF16, (bm, bn), tile))
        else:
            extras.append((norm[1].reshape(1, N), (1, N), lambda i, j, k: (0, 0)))

    def epi(acc, ex, out):
        y = acc if bias is None else acc + ex[n_res + 1][...]
        hn = read_res(ex[:n_res]) + ex[n_res][...] * y
        if norm is None:
            out[0][...] = hn
        elif norm[0] == "modulate":
            g_ref, sh_ref, sc_ref = ex[n_fix:]
            out[0][...] = hn
            out[1][...] = (_rms(hn, g_ref[...], NORM_EPS) * (1.0 + sc_ref[...]) + sh_ref[...]).astype(BF16)
        else:
            out[0][...] = _rms(hn, ex[n_fix][...], NORM_EPS)

    res = fused_matmul(a, w, extras, outs, epi, rows=rows, bn=bn, bk=bk, name=name)
    return res if len(res) > 1 else res[0]


def _spec_parts(spec):
    return spec.block_shape, spec.index_map


def h_width(h):
    return (h[0] if isinstance(h, tuple) else h).shape[1]


def _out_rows(rows):
    return rows.with_bm(min(NORM_ROW_TILE, rows.bm))


def adaln_table(cvec, ada_w, ada_b):
    depth, D, _ = ada_w.shape
    R = cvec.shape[0]

    def kern(c_ref, w_ref, b_ref, o_ref):
        c = c_ref[...]
        s = c / (1.0 + jnp.exp(-c))
        o_ref[...] = jnp.dot(s.astype(BF16), w_ref[...].astype(BF16), preferred_element_type=F32) + b_ref[...]

    out = pl.pallas_call(
        kern,
        grid=(depth, 6),
        in_specs=[pl.BlockSpec((R, D), lambda l, j: (0, 0)),
                  pl.BlockSpec((None, D, D), lambda l, j: (l, 0, j)),
                  pl.BlockSpec((None, 1, D), lambda l, j: (l, 0, j))],
        out_specs=pl.BlockSpec((None, None, R, D), lambda l, j: (l, j, 0, 0)),
        out_shape=jax.ShapeDtypeStruct((depth, 6, R, D), F32),
        compiler_params=_params(("parallel", "arbitrary")),
        name="adaln_table",
    )(cvec, ada_w, ada_b.reshape(depth, 1, 6 * D))
    return out.reshape(depth, 6, R, 1, D)


def norm_modulate(h, norm_g4, mod, layer, which, rows):
    D = norm_g4.shape[-1]
    bm = rows.bm
    h_specs, read_h = _row_operand(h, rows, D, lambda i: 0)
    n_h = len(h_specs)

    def kern(*refs):
        g_ref, sh_ref, sc_ref, o_ref = refs[n_h:]
        y = _rms(read_h(refs[:n_h]), g_ref[...], NORM_EPS)
        o_ref[...] = (y * (1.0 + sc_ref[...]) + sh_ref[...]).astype(BF16)

    return pl.pallas_call(
        kern,
        grid=(rows.n,),
        in_specs=[pl.BlockSpec(bs, im) for _, bs, im in h_specs]
        + [pl.BlockSpec((None, None, 1, D), lambda i: (layer, which, 0, 0)),
           _mod_spec(layer, 3 * which, rows, D, lambda i: 0),
           _mod_spec(layer, 3 * which + 1, rows, D, lambda i: 0)],
        out_specs=pl.BlockSpec((bm, D), lambda i: (i, 0)),
        out_shape=jax.ShapeDtypeStruct((rows.n * bm, D), BF16),
        compiler_params=_params(("parallel",)),
        name="norm_modulate",
    )(*[e[0] for e in h_specs], norm_g4, mod, mod)


def mlp(h, u, mod, layer, w1, w2, rows, norm):
    T = rows.n * rows.bm
    F, D = _w_shape(w2)

    def relu2(acc, ex, out):
        r = jnp.maximum(acc, 0.0)
        out[0][...] = (r * r).astype(BF16)

    hid = fused_matmul(u, w1, [], [((T, F), BF16, (rows.bm, COL_TILE), lambda i, j, k: (i, j))], relu2,
                       rows=rows, bn=COL_TILE, name="mlp_up")[0]
    return gated_matmul(hid, w2, h, mod, layer, 5, rows, bn=D, bk=K_TILE, norm=norm, name="mlp_down")


def axial_rope_tables(L, rot_dim):
    rows = L // GRID_W
    row = jnp.repeat(jnp.arange(rows, dtype=F32), GRID_W)
    col = jnp.tile(jnp.arange(GRID_W, dtype=F32), rows)
    pos = jnp.stack([row, col], axis=-1)
    n_freq = rot_dim // 4
    inv_freq = ROPE_THETA ** (-jnp.arange(n_freq, dtype=F32) / n_freq)
    ang = pos[:, :, None, None] * inv_freq
    ang = jnp.broadcast_to(ang, (L, 2, 2, n_freq)).reshape(L, rot_dim)
    return jnp.cos(ang), jnp.sin(ang)


def rope_tables_padded(L, rot_dim, bm):
    cos, sin = axial_rope_tables(L, rot_dim)
    cos = jnp.pad(cos, ((0, bm), (0, LANES - rot_dim)), constant_values=1.0)
    sin = jnp.pad(sin, ((0, bm), (0, LANES - rot_dim)))
    return cos, sin


def mla_down(u, w_dq, q_g, w_dkv, kv_g, cos, sin, rows):
    T = rows.n * rows.bm
    bm = rows.bm
    qr = w_dq.shape[1]
    wd = jnp.concatenate([w_dq, w_dkv, jnp.zeros((w_dq.shape[0], LANES - MLA_ROPE), w_dq.dtype)],
                         axis=1).astype(BF16)
    n_all = wd.shape[1]
    c0 = qr + MLA_KV_RANK

    def epi(acc, ex, out):
        qg_ref, kvg_ref, cos_ref, sin_ref = ex
        out[0][...] = _rms(acc[:, :qr], qg_ref[...], NORM_EPS).astype(BF16)
        out[1][...] = _rms(acc[:, qr:c0], kvg_ref[...], NORM_EPS).astype(BF16)
        kr = acc[:, c0:]
        out[2][...] = (kr * cos_ref[...] + _rot_half(kr, MLA_ROPE // 4) * sin_ref[...]).astype(BF16)

    extras = [(q_g.reshape(1, qr), (1, qr), lambda i, j, k: (0, 0)),
              (kv_g.reshape(1, MLA_KV_RANK), (1, MLA_KV_RANK), lambda i, j, k: (0, 0)),
              (cos, (bm, LANES), lambda i, j, k: (rows.pos_block(i), 0)),
              (sin, (bm, LANES), lambda i, j, k: (rows.pos_block(i), 0))]
    outs = [((T, qr), BF16, (bm, qr), lambda i, j, k: (i, 0)),
            ((T, MLA_KV_RANK), BF16, (bm, MLA_KV_RANK), lambda i, j, k: (i, 0)),
            ((T, LANES), BF16, (bm, LANES), lambda i, j, k: (i, 0))]
    return fused_matmul(u, wd, extras, outs, epi, rows=rows, bn=n_all, name="mla_down")


def mla_queries(cq, w_uq, cos, sin, rows, heads):
    T = rows.n * rows.bm
    bm = rows.bm
    qr = w_uq.shape[0]
    hd = MLA_NOPE + MLA_ROPE
    w = w_uq.reshape(qr, heads, hd)
    w = jnp.pad(w, ((0, 0), (0, 0), (0, MLA_HEAD_PAD - hd))).reshape(qr, heads * MLA_HEAD_PAD).astype(BF16)
    scale = hd ** -0.5 * LOG2E
    bn = COL_TILE

    def epi(acc, ex, out):
        cos_ref, sin_ref = ex
        for hh in range(bn // MLA_HEAD_PAD):
            c = hh * MLA_HEAD_PAD
            out[0][:, c:c + LANES] = (acc[:, c:c + LANES] * scale).astype(BF16)
            r = acc[:, c + LANES:c + 2 * LANES]
            r = r * cos_ref[...] + _rot_half(r, MLA_ROPE // 4) * sin_ref[...]
            out[0][:, c + LANES:c + 2 * LANES] = (r * scale).astype(BF16)

    extras = [(cos, (bm, LANES), lambda i, j, k: (rows.pos_block(i), 0)),
              (sin, (bm, LANES), lambda i, j, k: (rows.pos_block(i), 0))]
    outs = [((T, heads * MLA_HEAD_PAD), BF16, (bm, bn), lambda i, j, k: (i, j))]
    return fused_matmul(cq, w, extras, outs, epi, rows=rows, bn=bn, name="mla_queries")[0]


def plain_matmul(a, w, rows, bn, *, name, out_dtype=BF16, bias=None):
    T = rows.n * rows.bm
    N = w.shape[1]
    extras = [] if bias is None else [(bias.reshape(1, N), (1, bn), lambda i, j, k: (0, j))]

    def epi(acc, ex, out):
        y = acc if bias is None else acc + ex[0][...]
        out[0][...] = y.astype(out_dtype)

    return fused_matmul(a, w, extras, [((T, N), out_dtype, (rows.bm, bn), lambda i, j, k: (i, j))], epi,
                        rows=rows, bn=bn, name=name)[0]


def mla_attention(q, kv, kr, B, C, L, heads, ctx_queries):
    hp = 2
    tq = C if ctx_queries else MLA_Q_TILE
    n_t = 1 if ctx_queries else L // tq
    ctx_blk = B * L // C
    S = C if ctx_queries else C + L
    kvw = 2 * LANES
    q_row = (lambda b, g, t: ctx_blk + b) if ctx_queries else (lambda b, g, t: b * n_t + t)

    def kern(*refs):
        if ctx_queries:
            q_ref, kvc_ref, krc_ref, o_ref, k_s, v_s = refs
            parts = [(kvc_ref, krc_ref, 0, C)]
        else:
            q_ref, kvl_ref, kvc_ref, krl_ref, krc_ref, o_ref, k_s, v_s = refs
            parts = [(kvc_ref, krc_ref, 0, C), (kvl_ref, krl_ref, C, S)]

        @pl.when(pl.program_id(2) == 0)
        def _():
            for hh in range(hp):
                c0 = hh * kvw
                for kv_ref, kr_ref, lo, hi in parts:
                    k_s[hh, lo:hi, :LANES] = kv_ref[:, c0:c0 + LANES]
                    k_s[hh, lo:hi, LANES:] = kr_ref[...]
                    v_s[hh, lo:hi, :MLA_V] = kv_ref[:, c0 + LANES:c0 + kvw]
                v_s[hh, :, MLA_V:] = jnp.ones((S, MXU_COLS - MLA_V), BF16)

        ss = [lax.dot_general(q_ref[:, hh * MLA_HEAD_PAD:(hh + 1) * MLA_HEAD_PAD], k_s[hh],
                              (((1,), (1,)), ((), ())), preferred_element_type=F32) for hh in range(hp)]
        ps = [jnp.exp2(s - jnp.max(s, axis=-1, keepdims=True)).astype(BF16) for s in ss]
        for hh in range(hp):
            ov = jnp.dot(ps[hh], v_s[hh], preferred_element_type=F32)
            o_ref[:, hh * MLA_V:(hh + 1) * MLA_V] = (ov[:, :MLA_V] / ov[:, MLA_V:2 * MLA_V]).astype(BF16)

    lat_specs = [pl.BlockSpec((L, hp * kvw), lambda b, g, t: (b, g)),
                 pl.BlockSpec((L, LANES), lambda b, g, t: (b, 0))]
    ctx_specs = [pl.BlockSpec((C, hp * kvw), lambda b, g, t: (ctx_blk + b, g)),
                 pl.BlockSpec((C, LANES), lambda b, g, t: (ctx_blk + b, 0))]
    q_spec = pl.BlockSpec((tq, hp * MLA_HEAD_PAD), lambda b, g, t: (q_row(b, g, t), g))
    if ctx_queries:
        in_specs, args = [q_spec] + ctx_specs, (q, kv, kr)
    else:
        in_specs, args = [q_spec, lat_specs[0], ctx_specs[0], lat_specs[1], ctx_specs[1]], (q, kv, kv, kr, kr)
    return pl.pallas_call(
        kern,
        grid=(B, heads // hp, n_t),
        in_specs=in_specs,
        out_specs=pl.BlockSpec((tq, hp * MLA_V), lambda b, g, t: (b * n_t + t, g)),
        out_shape=jax.ShapeDtypeStruct((B * n_t * tq, heads * MLA_V), BF16),
        scratch_shapes=[pltpu.VMEM((hp, S, MLA_HEAD_PAD), BF16), pltpu.VMEM((hp, S, MXU_COLS), BF16)],
        compiler_params=_params(("parallel", "parallel", "arbitrary")),
        name="mla_attention_ctx" if ctx_queries else "mla_attention",
    )(*args)


def mla_mixer(h, u, mod, layer, p, geo, need_ctx, norm):
    B, C, L, rows_all, rows_out = geo
    heads = p["w_uq"].shape[1] // (MLA_NOPE + MLA_ROPE)
    cos, sin = rope_tables_padded(L, MLA_ROPE, rows_all.bm)
    cq, ckv, kr = mla_down(u, p["w_dq"], p["q_g"], p["w_dkv"], p["kv_g"], cos, sin, rows_all)
    kv = plain_matmul(ckv, p["w_ukv"].astype(BF16), rows_all, p["w_ukv"].shape[1], name="mla_kv")
    q = mla_queries(cq, p["w_uq"], cos, sin, rows_out, heads)
    o = mla_attention(q, kv, kr, B, C, L, heads, False)
    if need_ctx:
        o = (o, mla_attention(q, kv, kr, B, C, L, heads, True))
    return gated_matmul(o, p["w_o"].astype(BF16), h, mod, layer, 2, _out_rows(rows_out), bn=h_width(h), norm=norm,
                        name="mla_out")


def _diff_pair_layout():
    quarter = DF_HEAD_DIM // 4
    n = np.arange(2 * DF_HEAD_DIM)
    hf, c, a, r = n // DF_HEAD_DIM, (n % DF_HEAD_DIM) // (2 * quarter), (n % (2 * quarter)) // quarter, n % quarter
    head_perm = c * DF_HEAD_DIM + a * 2 * quarter + hf * quarter + r
    table_cols = (a * 2 * quarter + r)[:DF_HEAD_DIM]
    return head_perm, table_cols


def diff_rope_tables(L, bm):
    cos, sin = axial_rope_tables(L, DF_HEAD_DIM)
    _, cols = _diff_pair_layout()
    cos = jnp.pad(cos[:, cols], ((0, bm), (0, 0)), constant_values=1.0)
    sin = jnp.pad(sin[:, cols], ((0, bm), (0, 0)))
    return cos, sin


def diff_qkv(u, w_qkv, cos, sin, rows):
    T = rows.n * rows.bm
    bm = rows.bm
    D = w_qkv.shape[0]
    bn = COL_TILE
    hw = 2 * DF_HEAD_DIM
    scale = DF_HEAD_DIM ** -0.5 * LOG2E
    n_q = D // bn
    head_perm, _ = _diff_pair_layout()
    qk_cols = (np.arange(2 * D) // hw * hw)[:, None].reshape(-1, hw) + head_perm[None, :]
    cols = np.concatenate([qk_cols.reshape(-1), np.arange(2 * D, 3 * D)])
    w = w_qkv[:, cols].astype(BF16)

    def epi(acc, ex, out):
        cos_ref, sin_ref = ex
        mul = jnp.where(pl.program_id(1) < n_q, scale, 1.0).astype(F32)
        cs, sn = cos_ref[...] * mul, sin_ref[...] * mul
        for s in range(bn // hw):
            x1 = acc[:, s * hw:s * hw + LANES]
            x2 = acc[:, s * hw + LANES:(s + 1) * hw]
            out[0][:, s * hw:s * hw + LANES] = (x1 * cs - x2 * sn).astype(BF16)
            out[0][:, s * hw + LANES:(s + 1) * hw] = (x2 * cs + x1 * sn).astype(BF16)

    def table_block(i, j, k):
        return jnp.where(j < 2 * n_q, rows.pos_block(i), rows.tiles_per_batch), 0

    extras = [(cos, (bm, LANES), table_block), (sin, (bm, LANES), table_block)]
    outs = [((T, 3 * D), BF16, (bm, bn), lambda i, j, k: (i, j))]
    return fused_matmul(u, w, extras, outs, epi, rows=rows, bn=bn, name="diff_qkv")[0]


def diff_attention(qkv, lambdas, subln_g, lambda_init, B, C, L, D, ctx_queries):
    hd = DF_HEAD_DIM
    hw = 2 * hd
    heads = D // hw
    hp = 2
    tq = C if ctx_queries else DF_Q_TILE
    n_t = 1 if ctx_queries else L // tq
    ctx_blk = B * L // C
    S = C if ctx_queries else C + L
    kcol = D // (hp * hw)
    vcol = 2 * D // (hp * hw)
    q_row = (lambda b, g, t: ctx_blk + b) if ctx_queries else (lambda b, g, t: b * n_t + t)

    def kern(*refs):
        if ctx_queries:
            q_ref, kc_ref, vc_ref, lam_ref, g_ref, o_ref, k_s, v_s = refs
            parts = [(kc_ref, vc_ref, 0, C)]
        else:
            q_ref, kl_ref, kc_ref, vl_ref, vc_ref, lam_ref, g_ref, o_ref, k_s, v_s = refs
            parts = [(kc_ref, vc_ref, 0, C), (kl_ref, vl_ref, C, S)]

        @pl.when(pl.program_id(2) == 0)
        def _():
            map0 = (lax.broadcasted_iota(jnp.int32, (1, hw), 1) % hd) < hd // 2
            for k_ref, v_ref, lo, hi in parts:
                for hh in range(hp):
                    k = k_ref[:, hh * hw:(hh + 1) * hw]
                    k_s[2 * hh, lo:hi, :] = jnp.where(map0, k, jnp.zeros_like(k))
                    k_s[2 * hh + 1, lo:hi, :] = jnp.where(map0, jnp.zeros_like(k), k)
                    v_s[hh, lo:hi, :] = v_ref[:, hh * hw:(hh + 1) * hw]

        lf = lam_ref[...]
        lam = (jnp.exp(jnp.sum(lf[0:1] * lf[1:2], axis=-1, keepdims=True))
               - jnp.exp(jnp.sum(lf[2:3] * lf[3:4], axis=-1, keepdims=True)) + lambda_init)

        nc = 2 * hp
        ss = [lax.dot_general(q_ref[:, (c // 2) * hw:(c // 2 + 1) * hw], k_s[c],
                              (((1,), (1,)), ((), ())), preferred_element_type=F32) for c in range(nc)]
        ms = [jnp.max(s, axis=-1, keepdims=True) for s in ss]
        os_, ls = [0.0] * nc, [0.0] * nc
        for lo in range(0, S, MXU_COLS):
            for c in range(nc):
                p = jnp.exp2(ss[c][:, lo:lo + MXU_COLS] - ms[c])
                ls[c] = ls[c] + jnp.sum(p[:, :LANES] + p[:, LANES:], axis=-1, keepdims=True)
                os_[c] = os_[c] + jnp.dot(p.astype(BF16), v_s[c // 2, lo:lo + MXU_COLS, :],
                                          preferred_element_type=F32)
        for hh in range(hp):
            o = os_[2 * hh] / ls[2 * hh] - lam * (os_[2 * hh + 1] / ls[2 * hh + 1])
            o_ref[:, hh * hw:(hh + 1) * hw] = (_rms(o, g_ref[...], DF_SUBLN_EPS)
                                               * (1.0 - lambda_init)).astype(BF16)

    bw = hp * hw
    q_spec = pl.BlockSpec((tq, bw), lambda b, g, t: (q_row(b, g, t), g))
    lat_specs = [pl.BlockSpec((L, bw), lambda b, g, t: (b, kcol + g)),
                 pl.BlockSpec((L, bw), lambda b, g, t: (b, vcol + g))]
    ctx_specs = [pl.BlockSpec((C, bw), lambda b, g, t: (ctx_blk + b, kcol + g)),
                 pl.BlockSpec((C, bw), lambda b, g, t: (ctx_blk + b, vcol + g))]
    par_specs = [pl.BlockSpec((4, hd), lambda b, g, t: (0, 0)), pl.BlockSpec((1, hw), lambda b, g, t: (0, 0))]
    if ctx_queries:
        in_specs, args = [q_spec] + ctx_specs, (qkv, qkv, qkv)
    else:
        in_specs = [q_spec, lat_specs[0], ctx_specs[0], lat_specs[1], ctx_specs[1]]
        args = (qkv, qkv, qkv, qkv, qkv)
    return pl.pallas_call(
        kern,
        grid=(B, heads // hp, n_t),
        in_specs=in_specs + par_specs,
        out_specs=pl.BlockSpec((tq, bw), lambda b, g, t: (b * n_t + t, g)),
        out_shape=jax.ShapeDtypeStruct((B * n_t * tq, D), BF16),
        scratch_shapes=[pltpu.VMEM((2 * hp, S, hw), BF16), pltpu.VMEM((hp, S, hw), BF16)],
        compiler_params=_params(("parallel", "parallel", "arbitrary")),
        name="diff_attention_ctx" if ctx_queries else "diff_attention",
    )(*args, lambdas, subln_g.reshape(1, hw))


def diff_mixer(h, u, mod, layer, p, geo, need_ctx, lambda_init, norm):
    B, C, L, rows_all, rows_out = geo
    D = h_width(h)
    cos, sin = diff_rope_tables(L, rows_all.bm)
    qkv = diff_qkv(u, p["w_qkv"], cos, sin, rows_all)
    o = diff_attention(qkv, p["lambdas"], p["subln_g"], lambda_init, B, C, L, D, False)
    if need_ctx:
        o = (o, diff_attention(qkv, p["lambdas"], p["subln_g"], lambda_init, B, C, L, D, True))
    return gated_matmul(o, p["w_o"].astype(BF16), h, mod, layer, 2, _out_rows(rows_out), bn=D, norm=norm,
                        name="diff_out")


def dft_matrices(Ls):
    n = 2 * Ls
    t0n = min(64, Ls)
    t1n = Ls // t0n
    f = jnp.arange(Ls, dtype=jnp.int32)[:, None]
    a1 = ((f * (jnp.arange(t1n, dtype=jnp.int32) * t0n)[None, :]) % n).astype(F32) * (2.0 * math.pi / n)
    a0 = ((f * jnp.arange(t0n, dtype=jnp.int32)[None, :]) % n).astype(F32) * (2.0 * math.pi / n)
    c1, s1 = jnp.cos(a1)[:, :, None], jnp.sin(a1)[:, :, None]
    c0, s0 = jnp.cos(a0)[:, None, :], jnp.sin(a0)[:, None, :]
    cosm = (c1 * c0 - s1 * s0).reshape(Ls, Ls)
    sinm = (s1 * c0 + c1 * s0).reshape(Ls, Ls)
    nyq = jnp.where(jnp.arange(Ls) % 2 == 0, 1.0, -1.0).astype(F32)[None, :]
    imag = jnp.where(f == 0, nyq, -sinm)
    fwd = jnp.concatenate([cosm, imag], axis=0)
    col = jnp.arange(n)
    cscale = jnp.where((col == 0) | (col == Ls), 1.0 / n, 2.0 / n).astype(F32)
    inv = fwd.T * cscale[None, :]
    return fwd.astype(BF16), inv.astype(BF16)


def hyena_filter_time(Ls, p):
    D = p["f_bias"].shape[0]
    order = p["f_w2"].shape[0]
    bands = (HY_EMB_DIM - 1) // 2
    t = jnp.linspace(0.0, 1.0, Ls, dtype=F32)[:, None]
    w = 2.0 * math.pi * jnp.arange(Ls, dtype=F32)[:, None] / Ls
    f = jnp.linspace(1e-4, bands - 1, bands, dtype=F32)
    feats = jnp.concatenate([t, jnp.cos(f * w), -jnp.sin(f * w)], axis=-1)
    feats = jnp.pad(feats, ((0, 0), (0, HY_EMB_PAD - HY_EMB_DIM)))
    w1 = jnp.pad(p["f_w1"], ((0, HY_EMB_PAD - HY_EMB_DIM), (0, 0)))
    deltas = jnp.abs(jnp.linspace(math.log(HY_TARGET) / HY_SLOW_PCT, math.log(HY_TARGET) / HY_FAST_PCT,
                                  D, dtype=F32))
    deltas2 = jnp.concatenate([deltas, deltas]).reshape(1, 2 * D)

    def ffn_kern(x_ref, w1_ref, b1_ref, w2_ref, b2_ref, w3_ref, b3_ref, fr_ref, o_ref):
        dot = functools.partial(jnp.dot, precision=HIGHEST, preferred_element_type=F32)
        fr = fr_ref[...]
        hcur = jnp.sin(fr[0:1] * (dot(x_ref[...], w1_ref[...]) + b1_ref[...]))
        hcur = jnp.sin(fr[1:2] * (dot(hcur, w2_ref[...]) + b2_ref[...]))
        o_ref[...] = jnp.sin(fr[2:3] * (dot(hcur, w3_ref[...]) + b3_ref[...]))

    full = lambda a: pl.BlockSpec(a.shape, lambda: (0,) * a.ndim)
    ffn_in = [feats, w1, p["f_b1"].reshape(1, order), p["f_w2"], p["f_b2"].reshape(1, order),
              p["f_w3"], p["f_b3"].reshape(1, order), p["f_freq"]]
    hff = pl.pallas_call(
        ffn_kern,
        in_specs=[full(a) for a in ffn_in],
        out_specs=pl.BlockSpec((Ls, order), lambda: (0, 0)),
        out_shape=jax.ShapeDtypeStruct((Ls, order), F32),
        name="hyena_filter_ffn",
    )(*ffn_in)

    bn = COL_TILE // 4

    def out_kern(h_ref, w_ref, d_ref, o_ref):
        tt = lax.broadcasted_iota(jnp.int32, (Ls, 1), 0).astype(F32) * (1.0 / (Ls - 1))
        hw = jnp.dot(h_ref[...], w_ref[...], precision=HIGHEST, preferred_element_type=F32)
        o_ref[...] = hw * jnp.exp(-tt * d_ref[...])

    return pl.pallas_call(
        out_kern,
        grid=(2 * D // bn,),
        in_specs=[pl.BlockSpec((Ls, order), lambda j: (0, 0)),
                  pl.BlockSpec((order, bn), lambda j: (0, j)),
                  pl.BlockSpec((1, bn), lambda j: (0, j))],
        out_specs=pl.BlockSpec((Ls, bn), lambda j: (0, j)),
        out_shape=jax.ShapeDtypeStruct((Ls, 2 * D), F32),
        compiler_params=_params(("parallel",)),
        name="hyena_filter_out",
    )(hff, p["f_wout"], deltas2)


def hyena_filter_spectrum(hfb, f_bias, fwd, Ls, D):
    n = 2 * Ls
    cb = LANES
    ncb = D // cb

    def kern(w_ref, hf_ref, hb_ref, fb_ref, o_ref):
        row = lax.broadcasted_iota(jnp.int32, (Ls, 1), 0)
        hb = jnp.where(row == 0, 0.0, hb_ref[...])
        ab = jnp.dot(w_ref[...], jnp.concatenate([hf_ref[...], hb], axis=1).astype(BF16),
                     preferred_element_type=F32)
        a, b = ab[:, :cb], ab[:, cb:]
        frow = lax.broadcasted_iota(jnp.int32, (n, 1), 0)
        real = frow <= Ls
        o_ref[...] = a + jnp.where(real, b + fb_ref[...], -b)

    return pl.pallas_call(
        kern,
        grid=(ncb,),
        in_specs=[pl.BlockSpec((n, Ls), lambda j: (0, 0), pipeline_mode=pl.Buffered(1)),
                  pl.BlockSpec((Ls, cb), lambda j: (0, j)),
                  pl.BlockSpec((Ls, cb), lambda j: (0, ncb + j)),
                  pl.BlockSpec((1, cb), lambda j: (0, j))],
        out_specs=pl.BlockSpec((n, cb), lambda j: (0, j)),
        out_shape=jax.ShapeDtypeStruct((n, D), F32),
        compiler_params=_params(("parallel",)),
        name="hyena_filter_spectrum",
    )(fwd, hfb, hfb, f_bias.reshape(1, D))


def _short_conv(x, w, b):
    n = x.shape[0]
    row = lax.broadcasted_iota(jnp.int32, (n, 1), 0)
    prev = jnp.where(row == 0, 0.0, pltpu.roll(x, 1, 0))
    nxt = jnp.where(row == n - 1, 0.0, pltpu.roll(x, n - 1, 0))
    return prev * w[0:1] + x * w[1:2] + nxt * w[2:3] + b


def hyena_segment(z, p, Ls, row_blk0, B, D):
    n = 2 * Ls
    cb = MXU_COLS
    ncb = D // cb
    fwd, inv = dft_matrices(Ls)
    hfb = hyena_filter_time(Ls, p)
    kf = hyena_filter_spectrum(hfb, p["f_bias"], fwd, Ls, D)
    conv_w, conv_b = p["conv_w"], p["conv_b"].reshape(1, 3 * D)

    def fwd_kern(w_ref, x1_ref, v_ref, cw1_ref, cb1_ref, cwv_ref, cbv_ref, kf_ref, y_ref):
        g = (_short_conv(v_ref[...].astype(F32), cwv_ref[...], cbv_ref[...])
             * _short_conv(x1_ref[...].astype(F32), cw1_ref[...], cb1_ref[...])).astype(BF16)
        u = jnp.dot(w_ref[...], g, preferred_element_type=F32)
        ure, uim = u[:Ls], u[Ls:]
        kre, kim = kf_ref[:Ls, :], kf_ref[Ls:, :]
        first = lax.broadcasted_iota(jnp.int32, (Ls, 1), 0) == 0
        y_ref[:Ls, :] = (ure * kre - jnp.where(first, 0.0, uim * kim)).astype(BF16)
        y_ref[Ls:, :] = jnp.where(first, uim * kim, ure * kim + uim * kre).astype(BF16)

    y = pl.pallas_call(
        fwd_kern,
        grid=(ncb, B),
        in_specs=[pl.BlockSpec((n, Ls), lambda j, b: (0, 0), pipeline_mode=pl.Buffered(1)),
                  pl.BlockSpec((Ls, cb), lambda j, b: (row_blk0 + b, ncb + j)),
                  pl.BlockSpec((Ls, cb), lambda j, b: (row_blk0 + b, 2 * ncb + j)),
                  pl.BlockSpec((HY_SHORT, cb), lambda j, b: (0, ncb + j)),
                  pl.BlockSpec((1, cb), lambda j, b: (0, ncb + j)),
                  pl.BlockSpec((HY_SHORT, cb), lambda j, b: (0, 2 * ncb + j)),
                  pl.BlockSpec((1, cb), lambda j, b: (0, 2 * ncb + j)),
                  pl.BlockSpec((n, cb), lambda j, b: (0, j))],
        out_specs=pl.BlockSpec((None, n, cb), lambda j, b: (b, 0, j)),
        out_shape=jax.ShapeDtypeStruct((B, n, D), BF16),
        compiler_params=_params(("parallel", "arbitrary")),
        name="hyena_dft",
    )(fwd, z, z, conv_w, conv_b, conv_w, conv_b, kf)

    def inv_kern(w_ref, y_ref, x0_ref, cw0_ref, cb0_ref, o_ref):
        conv = jnp.dot(w_ref[...], y_ref[...], preferred_element_type=F32)
        o_ref[...] = (conv * _short_conv(x0_ref[...].astype(F32), cw0_ref[...], cb0_ref[...])).astype(BF16)

    return pl.pallas_call(
        inv_kern,
        grid=(ncb, B),
        in_specs=[pl.BlockSpec((Ls, n), lambda j, b: (0, 0), pipeline_mode=pl.Buffered(1)),
                  pl.BlockSpec((None, n, cb), lambda j, b: (b, 0, j)),
                  pl.BlockSpec((Ls, cb), lambda j, b: (row_blk0 + b, j)),
                  pl.BlockSpec((HY_SHORT, cb), lambda j, b: (0, j)),
                  pl.BlockSpec((1, cb), lambda j, b: (0, j))],
        out_specs=pl.BlockSpec((Ls, cb), lambda j, b: (b, j)),
        out_shape=jax.ShapeDtypeStruct((B * Ls, D), BF16),
        compiler_params=_params(("parallel", "arbitrary")),
        name="hyena_idft",
    )(inv, y, z, conv_w, conv_b)


def hyena_mixer(h, u, mod, layer, p, geo, need_ctx, norm):
    B, C, L, rows_all, rows_out = geo
    D = h_width(h)
    z = plain_matmul(u, p["w_in"].astype(BF16), rows_out, COL_TILE, bias=p["b_in"], name="hyena_in")
    y = hyena_segment(z, p, L, 0, B, D)
    if need_ctx:
        y = (y, hyena_segment(z, p, C, B * L // C, B, D))
    return gated_matmul(y, p["w_out"].astype(BF16), h, mod, layer, 2, _out_rows(rows_out), bn=D, bias=p["b_out"],
                        norm=norm, name="hyena_out")


def kernel(x, c, ctx, c_ctx, ada_w, ada_b, norm_g, mlp_w1, mlp_w2, final_norm_g, mla_w_dq, mla_q_norm_g, mla_w_uq, mla_w_dkv, mla_kv_norm_g, mla_w_ukv, mla_w_o, hy_w_in, hy_b_in, hy_conv_w, hy_conv_b, hy_filt_w1, hy_filt_b1, hy_filt_w2, hy_filt_b2, hy_filt_w3, hy_filt_b3, hy_filt_freq, hy_filt_wout, hy_filt_bias, hy_w_out, hy_b_out, df_w_qkv, df_lambda, df_subln_g, df_w_o):
    B, L, D = x.shape
    C = ctx.shape[1]
    depth = ada_w.shape[0]
    bm = min(ROW_TILE, B * C)
    rows_all = Rows(B, C, L, bm)
    rows_lat = Rows(B, C, L, bm, lat_only=True)

    mod = adaln_table(jnp.concatenate([c_ctx[None, :], c], axis=0), ada_w, ada_b)
    norm_g4 = norm_g.reshape(depth, 2, 1, D)
    w1s, w2s = mlp_w1.astype(BF16), mlp_w2.astype(BF16)
    h = (x.reshape(B * L, D), ctx.reshape(B * C, D))
    u = norm_modulate(h, norm_g4, mod, 0, 0, rows_all)

    for i in range(depth):
        need_ctx = i < depth - 1
        rows_out = rows_all if need_ctx else rows_lat
        geo = (B, C, L, rows_all, rows_out)
        kind, j = i % N_MIXERS, i // N_MIXERS
        mlp_norm = ("modulate", norm_g4, i, 1)
        if kind == 0:
            p = dict(w_dq=mla_w_dq[j], q_g=mla_q_norm_g[j], w_uq=mla_w_uq[j], w_dkv=mla_w_dkv[j],
                     kv_g=mla_kv_norm_g[j], w_ukv=mla_w_ukv[j], w_o=mla_w_o[j])
            h, u = mla_mixer(h, u, mod, i, p, geo, need_ctx, mlp_norm)
        elif kind == 1:
            p = dict(w_in=hy_w_in[j], b_in=hy_b_in[j], conv_w=hy_conv_w[j], conv_b=hy_conv_b[j],
                     f_w1=hy_filt_w1[j], f_b1=hy_filt_b1[j], f_w2=hy_filt_w2[j], f_b2=hy_filt_b2[j],
                     f_w3=hy_filt_w3[j], f_b3=hy_filt_b3[j], f_freq=hy_filt_freq[j], f_wout=hy_filt_wout[j],
                     f_bias=hy_filt_bias[j], w_out=hy_w_out[j], b_out=hy_b_out[j])
            h, u = hyena_mixer(h, u, mod, i, p, geo, need_ctx, mlp_norm)
        else:
            lambda_init = 0.8 - 0.6 * math.exp(-0.3 * i)
            p = dict(w_qkv=df_w_qkv[j], lambdas=df_lambda[j], subln_g=df_subln_g[j], w_o=df_w_o[j])
            h, u = diff_mixer(h, u, mod, i, p, geo, need_ctx, lambda_init, mlp_norm)
        next_norm = ("modulate", norm_g4, i + 1, 0) if need_ctx else ("final", final_norm_g)
        res = mlp(h, u, mod, i, (w1s, i), (w2s, i), rows_out, next_norm)
        if need_ctx:
            h, u = res

    return res.reshape(B, L, D)
```

```python
import functools
import math

import jax
import jax.numpy as jnp
import numpy as np
from jax import lax
from jax.experimental import pallas as pl
from jax.experimental.pallas import tpu as pltpu

F32 = jnp.float32
BF16 = jnp.bfloat16
HIGHEST = lax.Precision.HIGHEST
LOG2E = math.log2(math.e)

GRID_W = 64
ROPE_THETA = 10000.0
NORM_EPS = 1e-6
N_MIXERS = 3

MLA_NOPE = 128
MLA_ROPE = 64
MLA_V = 128
MLA_KV_RANK = 512
MLA_HEAD_PAD = 256
MLA_Q_TILE = 1024

HY_SHORT = 3
HY_EMB_DIM = 33
HY_EMB_PAD = 64
HY_TARGET = 1e-2
HY_FAST_PCT = 0.3
HY_SLOW_PCT = 1.5

DF_HEAD_DIM = 128
DF_SUBLN_EPS = 1e-5
DF_Q_TILE = 512

LANES = 128
MXU_COLS = 256
VMEM_LIMIT_MB = 56

ROW_TILE = 1024
COL_TILE = 2048
NORM_ROW_TILE = 512
K_TILE = 1024


def _params(semantics, vmem_mb=VMEM_LIMIT_MB):
    return pltpu.CompilerParams(dimension_semantics=semantics, vmem_limit_bytes=vmem_mb << 20)


class Rows:
    def __init__(self, B, C, L, bm, lat_only=False):
        assert L % bm == 0 and (B * C) % bm == 0
        self.geometry = (B, C, L, lat_only)
        self.bm = bm
        self.tiles_per_batch = L // bm
        self.n_lat = B * L // bm
        self.n = self.n_lat + (0 if lat_only else B * C // bm)

    def with_bm(self, bm):
        B, C, L, lat_only = self.geometry
        return Rows(B, C, L, bm, lat_only)

    def mod_row(self, i):
        return jnp.where(i < self.n_lat, 1 + i // self.tiles_per_batch, 0)

    def pos_block(self, i):
        return jnp.where(i < self.n_lat, i % self.tiles_per_batch, self.tiles_per_batch)


def _mod_spec(layer, chunk, rows, bn, col_of):
    return pl.BlockSpec((None, None, None, 1, bn),
                        lambda *g: (layer, chunk, rows.mod_row(g[0]), 0, col_of(*g)))


def _w_shape(w):
    return w[0].shape[1:] if isinstance(w, tuple) else w.shape


def _row_operand(arr, rows, width, col_of):
    bm = rows.bm
    n_lat = rows.n_lat
    if isinstance(arr, tuple) and rows.n == n_lat:
        arr = arr[0]
    if not isinstance(arr, tuple):
        return [(arr, (bm, width), lambda *g: (g[0], col_of(*g)))], lambda refs: refs[0][...]
    specs = [(arr[0], (bm, width), lambda *g: (jnp.minimum(g[0], n_lat - 1), col_of(*g))),
             (arr[1], (bm, width), lambda *g: (jnp.maximum(g[0] - n_lat, 0), col_of(*g)))]
    return specs, lambda refs: jnp.where(pl.program_id(0) < n_lat, refs[0][...], refs[1][...])


def fused_matmul(a, w, extras, outs, epilogue, *, rows, bn, bk=None, name):
    K, N = _w_shape(w)
    bk = bk or K
    nk = K // bk
    assert K % bk == 0 and N % bn == 0
    a_specs, read_a = _row_operand(a, rows, bk, lambda i, j, k: k)
    n_a = len(a_specs)
    in_specs = [pl.BlockSpec(bs, im) for _, bs, im in a_specs]
    if isinstance(w, tuple):
        w, w_layer = w
        in_specs += [pl.BlockSpec((None, bk, bn), lambda i, j, k: (w_layer, k, j))]
    else:
        in_specs += [pl.BlockSpec((bk, bn), lambda i, j, k: (k, j))]
    in_specs += [pl.BlockSpec(bs, im) for _, bs, im in extras]
    n_ex = len(extras)
    n_out = len(outs)

    def kern(*refs):
        w_ref = refs[n_a]
        ex = refs[n_a + 1:n_a + 1 + n_ex]
        out = refs[n_a + 1 + n_ex:n_a + 1 + n_ex + n_out]

        def dot():
            return jnp.dot(read_a(refs[:n_a]), w_ref[...], preferred_element_type=F32)

        if nk == 1:
            epilogue(dot(), ex, out)
        else:
            acc_ref = out[0]
            k = pl.program_id(2)

            @pl.when(k == 0)
            def _():
                acc_ref[...] = dot()

            @pl.when((k > 0) & (k < nk - 1))
            def _():
                acc_ref[...] += dot()

            @pl.when(k == nk - 1)
            def _():
                epilogue(acc_ref[...] + dot(), ex, out)

    res = pl.pallas_call(
        kern,
        grid=(rows.n, N // bn, nk),
        in_specs=in_specs,
        out_specs=[pl.BlockSpec(bs, im) for _, _, bs, im in outs],
        out_shape=[jax.ShapeDtypeStruct(s, d) for s, d, _, _ in outs],
        compiler_params=_params(("parallel", "arbitrary", "arbitrary")),
        name=name,
    )(*[e[0] for e in a_specs], w, *[e[0] for e in extras])
    return res


def _rot_half(x, quarter):
    n = x.shape[-1]
    lane = lax.broadcasted_iota(jnp.int32, x.shape, x.ndim - 1)
    first = (lane % (2 * quarter)) < quarter
    return jnp.where(first, -pltpu.roll(x, n - quarter, x.ndim - 1), pltpu.roll(x, quarter, x.ndim - 1))


def _rms(x, g, eps):
    return x * lax.rsqrt(jnp.mean(x * x, axis=-1, keepdims=True) + eps) * g


def gated_matmul(a, w, resid, mod, layer, chunk, rows, *, name, bn=COL_TILE, bk=None, bias=None, norm=None):
    T = rows.n * rows.bm
    bm = rows.bm
    N = _w_shape(w)[1]
    res_specs, read_res = _row_operand(resid, rows, bn, lambda i, j, k: j)
    n_res = len(res_specs)
    extras = res_specs + [(mod,) + _spec_parts(_mod_spec(layer, chunk, rows, bn, lambda i, j, k: j))]
    if bias is not None:
        extras.append((bias.reshape(1, N), (1, bn), lambda i, j, k: (0, j)))
    n_fix = len(extras)
    tile = lambda i, j, k: (i, j)
    outs = [((T, N), F32, (bm, bn), tile)]
    if norm is not None:
        assert bn == N
        if norm[0] == "modulate":
            _, g4, nl, which = norm
            extras += [(g4, (None, None, 1, N), lambda i, j, k: (nl, which, 0, 0)),
                       (mod,) + _spec_parts(_mod_spec(nl, 3 * which, rows, N, lambda i, j, k: 0)),
                       (mod,) + _spec_parts(_mod_spec(nl, 3 * which + 1, rows, N, lambda i, j, k: 0))]
            outs.append(((T, N), BF16, (bm, bn), tile))
        else:
            extras.append((norm[1].reshape(1, N), (1, N), lambda i, j, k: (0, 0)))

    def epi(acc, ex, out):
        y = acc if bias is None else acc + ex[n_res + 1][...]
        hn = read_res(ex[:n_res]) + ex[n_res][...] * y
        if norm is None:
            out[0][...] = hn
        elif norm[0] == "modulate":
            g_ref, sh_ref, sc_ref = ex[n_fix:]
            out[0][...] = hn
            out[1][...] = (_rms(hn, g_ref[...], NORM_EPS) * (1.0 + sc_ref[...]) + sh_ref[...]).astype(BF16)
        else:
            out[0][...] = _rms(hn, ex[n_fix][...], NORM_EPS)

    res = fused_matmul(a, w, extras, outs, epi, rows=rows, bn=bn, bk=bk, name=name)
    return res if len(res) > 1 else res[0]


def _spec_parts(spec):
    return spec.block_shape, spec.index_map


def h_width(h):
    return (h[0] if isinstance(h, tuple) else h).shape[1]


def _out_rows(rows):
    return rows.with_bm(min(NORM_ROW_TILE, rows.bm))


def adaln_table(cvec, ada_w, ada_b):
    depth, D, _ = ada_w.shape
    R = cvec.shape[0]

    def kern(c_ref, w_ref, b_ref, o_ref):
        c = c_ref[...]
        s = c / (1.0 + jnp.exp(-c))
        o_ref[...] = jnp.dot(s.astype(BF16), w_ref[...].astype(BF16), preferred_element_type=F32) + b_ref[...]

    out = pl.pallas_call(
        kern,
        grid=(depth, 6),
        in_specs=[pl.BlockSpec((R, D), lambda l, j: (0, 0)),
                  pl.BlockSpec((None, D, D), lambda l, j: (l, 0, j)),
                  pl.BlockSpec((None, 1, D), lambda l, j: (l, 0, j))],
        out_specs=pl.BlockSpec((None, None, R, D), lambda l, j: (l, j, 0, 0)),
        out_shape=jax.ShapeDtypeStruct((depth, 6, R, D), F32),
        compiler_params=_params(("parallel", "arbitrary")),
        name="adaln_table",
    )(cvec, ada_w, ada_b.reshape(depth, 1, 6 * D))
    return out.reshape(depth, 6, R, 1, D)


def norm_modulate(h, norm_g4, mod, layer, which, rows):
    D = norm_g4.shape[-1]
    bm = rows.bm
    h_specs, read_h = _row_operand(h, rows, D, lambda i: 0)
    n_h = len(h_specs)

    def kern(*refs):
        g_ref, sh_ref, sc_ref, o_ref = refs[n_h:]
        y = _rms(read_h(refs[:n_h]), g_ref[...], NORM_EPS)
        o_ref[...] = (y * (1.0 + sc_ref[...]) + sh_ref[...]).astype(BF16)

    return pl.pallas_call(
        kern,
        grid=(rows.n,),
        in_specs=[pl.BlockSpec(bs, im) for _, bs, im in h_specs]
        + [pl.BlockSpec((None, None, 1, D), lambda i: (layer, which, 0, 0)),
           _mod_spec(layer, 3 * which, rows, D, lambda i: 0),
           _mod_spec(layer, 3 * which + 1, rows, D, lambda i: 0)],
        out_specs=pl.BlockSpec((bm, D), lambda i: (i, 0)),
        out_shape=jax.ShapeDtypeStruct((rows.n * bm, D), BF16),
        compiler_params=_params(("parallel",)),
        name="norm_modulate",
    )(*[e[0] for e in h_specs], norm_g4, mod, mod)


def mlp(h, u, mod, layer, w1, w2, rows, norm):
    rows = _out_rows(rows)
    bm = rows.bm
    T = rows.n * bm
    (w1s, _), (w2s, _) = w1, w2
    _, D, F = w1s.shape
    fc = K_TILE
    nf = F // fc
    modulate = norm[0] == "modulate"

    def kern(*refs):
        u_ref, w1_ref, w2_ref, res_ref, gate_ref = refs[:5]
        norm_refs, outs, hid_s = refs[5:-3 if modulate else -2], refs[-3 if modulate else -2:-1], refs[-1]
        acc_ref = outs[0]
        f = pl.program_id(1)

        def up(slot):
            r = jnp.maximum(jnp.dot(u_ref[...], w1_ref[...], preferred_element_type=F32), 0.0)
            hid_s[slot] = (r * r).astype(BF16)

        def down(slot):
            return jnp.dot(hid_s[slot], w2_ref[...], preferred_element_type=F32)

        @pl.when(f == 0)
        def _():
            up(0)

        @pl.when(f == 1)
        def _():
            up(1)
            acc_ref[...] = down(0)

        @pl.when((f > 1) & (f < nf))
        def _():
            up(f % 2)
            acc_ref[...] += down((f - 1) % 2)

        @pl.when(f == nf)
        def _():
            hn = res_ref[...] + gate_ref[...] * (acc_ref[...] + down((f - 1) % 2))
            if modulate:
                g_ref, sh_ref, sc_ref = norm_refs
                outs[0][...] = hn
                outs[1][...] = (_rms(hn, g_ref[...], NORM_EPS) * (1.0 + sc_ref[...]) + sh_ref[...]).astype(BF16)
            else:
                outs[0][...] = _rms(hn, norm_refs[0][...], NORM_EPS)

    tile = lambda i, f: (i, 0)
    in_specs = [pl.BlockSpec((bm, D), tile),
                pl.BlockSpec((None, D, fc), lambda i, f: (layer, 0, jnp.minimum(f, nf - 1))),
                pl.BlockSpec((None, fc, D), lambda i, f: (layer, jnp.maximum(f - 1, 0), 0)),
                pl.BlockSpec((bm, D), tile),
                _mod_spec(layer, 5, rows, D, lambda i, f: 0)]
    args = [u, w1s, w2s, h, mod]
    out_specs = [pl.BlockSpec((bm, D), tile)]
    out_shape = [jax.ShapeDtypeStruct((T, D), F32)]
    if modulate:
        _, g4, nl, which = norm
        in_specs += [pl.BlockSpec((None, None, 1, D), lambda i, f: (nl, which, 0, 0)),
                     _mod_spec(nl, 3 * which, rows, D, lambda i, f: 0),
                     _mod_spec(nl, 3 * which + 1, rows, D, lambda i, f: 0)]
        args += [g4, mod, mod]
        out_specs.append(pl.BlockSpec((bm, D), tile))
        out_shape.append(jax.ShapeDtypeStruct((T, D), BF16))
    else:
        in_specs.append(pl.BlockSpec((1, D), lambda i, f: (0, 0)))
        args.append(norm[1].reshape(1, D))
    res = pl.pallas_call(
        kern,
        grid=(rows.n, nf + 1),
        in_specs=in_specs,
        out_specs=out_specs,
        out_shape=out_shape,
        scratch_shapes=[pltpu.VMEM((2, bm, fc), BF16)],
        compiler_params=_params(("parallel", "arbitrary")),
        name="mlp",
    )(*args)
    return res if modulate else res[0]


def axial_rope_tables(L, rot_dim):
    rows = L // GRID_W
    row = jnp.repeat(jnp.arange(rows, dtype=F32), GRID_W)
    col = jnp.tile(jnp.arange(GRID_W, dtype=F32), rows)
    pos = jnp.stack([row, col], axis=-1)
    n_freq = rot_dim // 4
    inv_freq = ROPE_THETA ** (-jnp.arange(n_freq, dtype=F32) / n_freq)
    ang = pos[:, :, None, None] * inv_freq
    ang = jnp.broadcast_to(ang, (L, 2, 2, n_freq)).reshape(L, rot_dim)
    return jnp.cos(ang), jnp.sin(ang)


def rope_tables_padded(L, rot_dim, bm):
    cos, sin = axial_rope_tables(L, rot_dim)
    cos = jnp.pad(cos, ((0, bm), (0, LANES - rot_dim)), constant_values=1.0)
    sin = jnp.pad(sin, ((0, bm), (0, LANES - rot_dim)))
    return cos, sin


def mla_down(u, w_dq, q_g, w_dkv, kv_g, cos, sin, rows):
    T = rows.n * rows.bm
    bm = rows.bm
    qr = w_dq.shape[1]
    wd = jnp.concatenate([w_dq, w_dkv, jnp.zeros((w_dq.shape[0], LANES - MLA_ROPE), w_dq.dtype)],
                         axis=1).astype(BF16)
    n_all = wd.shape[1]
    c0 = qr + MLA_KV_RANK

    def epi(acc, ex, out):
        qg_ref, kvg_ref, cos_ref, sin_ref = ex
        out[0][...] = _rms(acc[:, :qr], qg_ref[...], NORM_EPS).astype(BF16)
        out[1][...] = _rms(acc[:, qr:c0], kvg_ref[...], NORM_EPS).astype(BF16)
        kr = acc[:, c0:]
        out[2][...] = (kr * cos_ref[...] + _rot_half(kr, MLA_ROPE // 4) * sin_ref[...]).astype(BF16)

    extras = [(q_g.reshape(1, qr), (1, qr), lambda i, j, k: (0, 0)),
              (kv_g.reshape(1, MLA_KV_RANK), (1, MLA_KV_RANK), lambda i, j, k: (0, 0)),
              (cos, (bm, LANES), lambda i, j, k: (rows.pos_block(i), 0)),
              (sin, (bm, LANES), lambda i, j, k: (rows.pos_block(i), 0))]
    outs = [((T, qr), BF16, (bm, qr), lambda i, j, k: (i, 0)),
            ((T, MLA_KV_RANK), BF16, (bm, MLA_KV_RANK), lambda i, j, k: (i, 0)),
            ((T, LANES), BF16, (bm, LANES), lambda i, j, k: (i, 0))]
    return fused_matmul(u, wd, extras, outs, epi, rows=rows, bn=n_all, name="mla_down")


def mla_queries(cq, w_uq, cos, sin, rows, heads):
    T = rows.n * rows.bm
    bm = rows.bm
    qr = w_uq.shape[0]
    hd = MLA_NOPE + MLA_ROPE
    w = w_uq.reshape(qr, heads, hd)
    w = jnp.pad(w, ((0, 0), (0, 0), (0, MLA_HEAD_PAD - hd))).reshape(qr, heads * MLA_HEAD_PAD).astype(BF16)
    scale = hd ** -0.5 * LOG2E
    bn = COL_TILE

    def epi(acc, ex, out):
        cos_ref, sin_ref = ex
        for hh in range(bn // MLA_HEAD_PAD):
            c = hh * MLA_HEAD_PAD
            out[0][:, c:c + LANES] = (acc[:, c:c + LANES] * scale).astype(BF16)
            r = acc[:, c + LANES:c + 2 * LANES]
            r = r * cos_ref[...] + _rot_half(r, MLA_ROPE // 4) * sin_ref[...]
            out[0][:, c + LANES:c + 2 * LANES] = (r * scale).astype(BF16)

    extras = [(cos, (bm, LANES), lambda i, j, k: (rows.pos_block(i), 0)),
              (sin, (bm, LANES), lambda i, j, k: (rows.pos_block(i), 0))]
    outs = [((T, heads * MLA_HEAD_PAD), BF16, (bm, bn), lambda i, j, k: (i, j))]
    return fused_matmul(cq, w, extras, outs, epi, rows=rows, bn=bn, name="mla_queries")[0]


def plain_matmul(a, w, rows, bn, *, name, out_dtype=BF16, bias=None):
    T = rows.n * rows.bm
    N = w.shape[1]
    extras = [] if bias is None else [(bias.reshape(1, N), (1, bn), lambda i, j, k: (0, j))]

    def epi(acc, ex, out):
        y = acc if bias is None else acc + ex[0][...]
        out[0][...] = y.astype(out_dtype)

    return fused_matmul(a, w, extras, [((T, N), out_dtype, (rows.bm, bn), lambda i, j, k: (i, j))], epi,
                        rows=rows, bn=bn, name=name)[0]


def mla_attention(q, kv, kr, B, C, L, heads, ctx_queries):
    hp = 2
    tq = C if ctx_queries else MLA_Q_TILE
    n_t = 1 if ctx_queries else L // tq
    ctx_blk = B * L // C
    S = C if ctx_queries else C + L
    kvw = 2 * LANES
    q_row = (lambda b, g, t: ctx_blk + b) if ctx_queries else (lambda b, g, t: b * n_t + t)

    def kern(*refs):
        if ctx_queries:
            q_ref, kvc_ref, krc_ref, o_ref, k_s, v_s = refs
            parts = [(kvc_ref, krc_ref, 0, C)]
        else:
            q_ref, kvl_ref, kvc_ref, krl_ref, krc_ref, o_ref, k_s, v_s = refs
            parts = [(kvc_ref, krc_ref, 0, C), (kvl_ref, krl_ref, C, S)]

        @pl.when(pl.program_id(2) == 0)
        def _():
            for hh in range(hp):
                c0 = hh * kvw
                for kv_ref, kr_ref, lo, hi in parts:
                    k_s[hh, lo:hi, :LANES] = kv_ref[:, c0:c0 + LANES]
                    k_s[hh, lo:hi, LANES:] = kr_ref[...]
                    v_s[hh, lo:hi, :MLA_V] = kv_ref[:, c0 + LANES:c0 + kvw]
                v_s[hh, :, MLA_V:] = jnp.ones((S, MXU_COLS - MLA_V), BF16)

        ss = [lax.dot_general(q_ref[:, hh * MLA_HEAD_PAD:(hh + 1) * MLA_HEAD_PAD], k_s[hh],
                              (((1,), (1,)), ((), ())), preferred_element_type=F32) for hh in range(hp)]
        ps = [jnp.exp2(s - jnp.max(s, axis=-1, keepdims=True)).astype(BF16) for s in ss]
        for hh in range(hp):
            ov = jnp.dot(ps[hh], v_s[hh], preferred_element_type=F32)
            o_ref[:, hh * MLA_V:(hh + 1) * MLA_V] = (ov[:, :MLA_V] / ov[:, MLA_V:2 * MLA_V]).astype(BF16)

    lat_specs = [pl.BlockSpec((L, hp * kvw), lambda b, g, t: (b, g)),
                 pl.BlockSpec((L, LANES), lambda b, g, t: (b, 0))]
    ctx_specs = [pl.BlockSpec((C, hp * kvw), lambda b, g, t: (ctx_blk + b, g)),
                 pl.BlockSpec((C, LANES), lambda b, g, t: (ctx_blk + b, 0))]
    q_spec = pl.BlockSpec((tq, hp * MLA_HEAD_PAD), lambda b, g, t: (q_row(b, g, t), g))
    if ctx_queries:
        in_specs, args = [q_spec] + ctx_specs, (q, kv, kr)
    else:
        in_specs, args = [q_spec, lat_specs[0], ctx_specs[0], lat_specs[1], ctx_specs[1]], (q, kv, kv, kr, kr)
    return pl.pallas_call(
        kern,
        grid=(B, heads // hp, n_t),
        in_specs=in_specs,
        out_specs=pl.BlockSpec((tq, hp * MLA_V), lambda b, g, t: (b * n_t + t, g)),
        out_shape=jax.ShapeDtypeStruct((B * n_t * tq, heads * MLA_V), BF16),
        scratch_shapes=[pltpu.VMEM((hp, S, MLA_HEAD_PAD), BF16), pltpu.VMEM((hp, S, MXU_COLS), BF16)],
        compiler_params=_params(("parallel", "parallel", "arbitrary")),
        name="mla_attention_ctx" if ctx_queries else "mla_attention",
    )(*args)


def mla_mixer(h, u, mod, layer, p, geo, need_ctx, norm):
    B, C, L, rows_all, rows_out = geo
    heads = p["w_uq"].shape[1] // (MLA_NOPE + MLA_ROPE)
    cos, sin = rope_tables_padded(L, MLA_ROPE, rows_all.bm)
    cq, ckv, kr = mla_down(u, p["w_dq"], p["q_g"], p["w_dkv"], p["kv_g"], cos, sin, rows_all)
    kv = plain_matmul(ckv, p["w_ukv"].astype(BF16), rows_all, p["w_ukv"].shape[1], name="mla_kv")
    q = mla_queries(cq, p["w_uq"], cos, sin, rows_out, heads)
    o = mla_attention(q, kv, kr, B, C, L, heads, False)
    if need_ctx:
        o = (o, mla_attention(q, kv, kr, B, C, L, heads, True))
    return gated_matmul(o, p["w_o"].astype(BF16), h, mod, layer, 2, _out_rows(rows_out), bn=h_width(h), norm=norm,
                        name="mla_out")


def _diff_pair_layout():
    quarter = DF_HEAD_DIM // 4
    n = np.arange(2 * DF_HEAD_DIM)
    hf, c, a, r = n // DF_HEAD_DIM, (n % DF_HEAD_DIM) // (2 * quarter), (n % (2 * quarter)) // quarter, n % quarter
    head_perm = c * DF_HEAD_DIM + a * 2 * quarter + hf * quarter + r
    table_cols = (a * 2 * quarter + r)[:DF_HEAD_DIM]
    return head_perm, table_cols


def diff_rope_tables(L, bm):
    cos, sin = axial_rope_tables(L, DF_HEAD_DIM)
    _, cols = _diff_pair_layout()
    cos = jnp.pad(cos[:, cols], ((0, bm), (0, 0)), constant_values=1.0)
    sin = jnp.pad(sin[:, cols], ((0, bm), (0, 0)))
    return cos, sin


def diff_qkv(u, w_qkv, cos, sin, rows):
    T = rows.n * rows.bm
    bm = rows.bm
    D = w_qkv.shape[0]
    bn = COL_TILE
    hw = 2 * DF_HEAD_DIM
    scale = DF_HEAD_DIM ** -0.5 * LOG2E
    n_q = D // bn
    head_perm, _ = _diff_pair_layout()
    qk_cols = (np.arange(2 * D) // hw * hw)[:, None].reshape(-1, hw) + head_perm[None, :]
    cols = np.concatenate([qk_cols.reshape(-1), np.arange(2 * D, 3 * D)])
    w = w_qkv[:, cols].astype(BF16)

    def epi(acc, ex, out):
        cos_ref, sin_ref = ex
        mul = jnp.where(pl.program_id(1) < n_q, scale, 1.0).astype(F32)
        cs, sn = cos_ref[...] * mul, sin_ref[...] * mul
        for s in range(bn // hw):
            x1 = acc[:, s * hw:s * hw + LANES]
            x2 = acc[:, s * hw + LANES:(s + 1) * hw]
            out[0][:, s * hw:s * hw + LANES] = (x1 * cs - x2 * sn).astype(BF16)
            out[0][:, s * hw + LANES:(s + 1) * hw] = (x2 * cs + x1 * sn).astype(BF16)

    def table_block(i, j, k):
        return jnp.where(j < 2 * n_q, rows.pos_block(i), rows.tiles_per_batch), 0

    extras = [(cos, (bm, LANES), table_block), (sin, (bm, LANES), table_block)]
    outs = [((T, 3 * D), BF16, (bm, bn), lambda i, j, k: (i, j))]
    return fused_matmul(u, w, extras, outs, epi, rows=rows, bn=bn, name="diff_qkv")[0]


def diff_attention(qkv, lambdas, subln_g, lambda_init, B, C, L, D, ctx_queries):
    hd = DF_HEAD_DIM
    hw = 2 * hd
    heads = D // hw
    hp = 2
    tq = C if ctx_queries else DF_Q_TILE
    n_t = 1 if ctx_queries else L // tq
    ctx_blk = B * L // C
    S = C if ctx_queries else C + L
    kcol = D // (hp * hw)
    vcol = 2 * D // (hp * hw)
    q_row = (lambda b, g, t: ctx_blk + b) if ctx_queries else (lambda b, g, t: b * n_t + t)

    def kern(*refs):
        if ctx_queries:
            q_ref, kc_ref, vc_ref, lam_ref, g_ref, o_ref, k_s, v_s = refs
            parts = [(kc_ref, vc_ref, 0, C)]
        else:
            q_ref, kl_ref, kc_ref, vl_ref, vc_ref, lam_ref, g_ref, o_ref, k_s, v_s = refs
            parts = [(kc_ref, vc_ref, 0, C), (kl_ref, vl_ref, C, S)]

        @pl.when(pl.program_id(2) == 0)
        def _():
            map0 = (lax.broadcasted_iota(jnp.int32, (1, hw), 1) % hd) < hd // 2
            for k_ref, v_ref, lo, hi in parts:
                for hh in range(hp):
                    k = k_ref[:, hh * hw:(hh + 1) * hw]
                    k_s[2 * hh, lo:hi, :] = jnp.where(map0, k, jnp.zeros_like(k))
                    k_s[2 * hh + 1, lo:hi, :] = jnp.where(map0, jnp.zeros_like(k), k)
                    v_s[hh, lo:hi, :] = v_ref[:, hh * hw:(hh + 1) * hw]

        lf = lam_ref[...]
        lam = (jnp.exp(jnp.sum(lf[0:1] * lf[1:2], axis=-1, keepdims=True))
               - jnp.exp(jnp.sum(lf[2:3] * lf[3:4], axis=-1, keepdims=True)) + lambda_init)

        nc = 2 * hp
        ss = [lax.dot_general(q_ref[:, (c // 2) * hw:(c // 2 + 1) * hw], k_s[c],
                              (((1,), (1,)), ((), ())), preferred_element_type=F32) for c in range(nc)]
        ms = [jnp.max(s, axis=-1, keepdims=True) for s in ss]
        os_, ls = [0.0] * nc, [0.0] * nc
        for lo in range(0, S, MXU_COLS):
            for c in range(nc):
                p = jnp.exp2(ss[c][:, lo:lo + MXU_COLS] - ms[c])
                ls[c] = ls[c] + jnp.sum(p[:, :LANES] + p[:, LANES:], axis=-1, keepdims=True)
                os_[c] = os_[c] + jnp.dot(p.astype(BF16), v_s[c // 2, lo:lo + MXU_COLS, :],
                                          preferred_element_type=F32)
        for hh in range(hp):
            o = os_[2 * hh] / ls[2 * hh] - lam * (os_[2 * hh + 1] / ls[2 * hh + 1])
            o_ref[:, hh * hw:(hh + 1) * hw] = (_rms(o, g_ref[...], DF_SUBLN_EPS)
                                               * (1.0 - lambda_init)).astype(BF16)

    bw = hp * hw
    q_spec = pl.BlockSpec((tq, bw), lambda b, g, t: (q_row(b, g, t), g))
    lat_specs = [pl.BlockSpec((L, bw), lambda b, g, t: (b, kcol + g)),
                 pl.BlockSpec((L, bw), lambda b, g, t: (b, vcol + g))]
    ctx_specs = [pl.BlockSpec((C, bw), lambda b, g, t: (ctx_blk + b, kcol + g)),
                 pl.BlockSpec((C, bw), lambda b, g, t: (ctx_blk + b, vcol + g))]
    par_specs = [pl.BlockSpec((4, hd), lambda b, g, t: (0, 0)), pl.BlockSpec((1, hw), lambda b, g, t: (0, 0))]
    if ctx_queries:
        in_specs, args = [q_spec] + ctx_specs, (qkv, qkv, qkv)
    else:
        in_specs = [q_spec, lat_specs[0], ctx_specs[0], lat_specs[1], ctx_specs[1]]
        args = (qkv, qkv, qkv, qkv, qkv)
    return pl.pallas_call(
        kern,
        grid=(B, heads // hp, n_t),
        in_specs=in_specs + par_specs,
        out_specs=pl.BlockSpec((tq, bw), lambda b, g, t: (b * n_t + t, g)),
        out_shape=jax.ShapeDtypeStruct((B * n_t * tq, D), BF16),
        scratch_shapes=[pltpu.VMEM((2 * hp, S, hw), BF16), pltpu.VMEM((hp, S, hw), BF16)],
        compiler_params=_params(("parallel", "parallel", "arbitrary")),
        name="diff_attention_ctx" if ctx_queries else "diff_attention",
    )(*args, lambdas, subln_g.reshape(1, hw))


def diff_mixer(h, u, mod, layer, p, geo, need_ctx, lambda_init, norm):
    B, C, L, rows_all, rows_out = geo
    D = h_width(h)
    cos, sin = diff_rope_tables(L, rows_all.bm)
    qkv = diff_qkv(u, p["w_qkv"], cos, sin, rows_all)
    o = diff_attention(qkv, p["lambdas"], p["subln_g"], lambda_init, B, C, L, D, False)
    if need_ctx:
        o = (o, diff_attention(qkv, p["lambdas"], p["subln_g"], lambda_init, B, C, L, D, True))
    return gated_matmul(o, p["w_o"].astype(BF16), h, mod, layer, 2, _out_rows(rows_out), bn=D, norm=norm,
                        name="diff_out")


def dft_matrices(Ls):
    n = 2 * Ls
    t0n = min(64, Ls)
    t1n = Ls // t0n
    f = jnp.arange(Ls, dtype=jnp.int32)[:, None]
    a1 = ((f * (jnp.arange(t1n, dtype=jnp.int32) * t0n)[None, :]) % n).astype(F32) * (2.0 * math.pi / n)
    a0 = ((f * jnp.arange(t0n, dtype=jnp.int32)[None, :]) % n).astype(F32) * (2.0 * math.pi / n)
    c1, s1 = jnp.cos(a1)[:, :, None], jnp.sin(a1)[:, :, None]
    c0, s0 = jnp.cos(a0)[:, None, :], jnp.sin(a0)[:, None, :]
    cosm = (c1 * c0 - s1 * s0).reshape(Ls, Ls)
    sinm = (s1 * c0 + c1 * s0).reshape(Ls, Ls)
    nyq = jnp.where(jnp.arange(Ls) % 2 == 0, 1.0, -1.0).astype(F32)[None, :]
    imag = jnp.where(f == 0, nyq, -sinm)
    fwd = jnp.concatenate([cosm, imag], axis=0)
    col = jnp.arange(n)
    cscale = jnp.where((col == 0) | (col == Ls), 1.0 / n, 2.0 / n).astype(F32)
    inv = fwd.T * cscale[None, :]
    return fwd.astype(BF16), inv.astype(BF16)


def hyena_filter_time(Ls, p):
    D = p["f_bias"].shape[0]
    order = p["f_w2"].shape[0]
    bands = (HY_EMB_DIM - 1) // 2
    t = jnp.linspace(0.0, 1.0, Ls, dtype=F32)[:, None]
    w = 2.0 * math.pi * jnp.arange(Ls, dtype=F32)[:, None] / Ls
    f = jnp.linspace(1e-4, bands - 1, bands, dtype=F32)
    feats = jnp.concatenate([t, jnp.cos(f * w), -jnp.sin(f * w)], axis=-1)
    feats = jnp.pad(feats, ((0, 0), (0, HY_EMB_PAD - HY_EMB_DIM)))
    w1 = jnp.pad(p["f_w1"], ((0, HY_EMB_PAD - HY_EMB_DIM), (0, 0)))
    deltas = jnp.abs(jnp.linspace(math.log(HY_TARGET) / HY_SLOW_PCT, math.log(HY_TARGET) / HY_FAST_PCT,
                                  D, dtype=F32))
    deltas2 = jnp.concatenate([deltas, deltas]).reshape(1, 2 * D)

    def ffn_kern(x_ref, w1_ref, b1_ref, w2_ref, b2_ref, w3_ref, b3_ref, fr_ref, o_ref):
        dot = functools.partial(jnp.dot, precision=HIGHEST, preferred_element_type=F32)
        fr = fr_ref[...]
        hcur = jnp.sin(fr[0:1] * (dot(x_ref[...], w1_ref[...]) + b1_ref[...]))
        hcur = jnp.sin(fr[1:2] * (dot(hcur, w2_ref[...]) + b2_ref[...]))
        o_ref[...] = jnp.sin(fr[2:3] * (dot(hcur, w3_ref[...]) + b3_ref[...]))

    full = lambda a: pl.BlockSpec(a.shape, lambda: (0,) * a.ndim)
    ffn_in = [feats, w1, p["f_b1"].reshape(1, order), p["f_w2"], p["f_b2"].reshape(1, order),
              p["f_w3"], p["f_b3"].reshape(1, order), p["f_freq"]]
    hff = pl.pallas_call(
        ffn_kern,
        in_specs=[full(a) for a in ffn_in],
        out_specs=pl.BlockSpec((Ls, order), lambda: (0, 0)),
        out_shape=jax.ShapeDtypeStruct((Ls, order), F32),
        name="hyena_filter_ffn",
    )(*ffn_in)

    bn = COL_TILE // 4

    def out_kern(h_ref, w_ref, d_ref, o_ref):
        tt = lax.broadcasted_iota(jnp.int32, (Ls, 1), 0).astype(F32) * (1.0 / (Ls - 1))
        hw = jnp.dot(h_ref[...], w_ref[...], precision=HIGHEST, preferred_element_type=F32)
        o_ref[...] = hw * jnp.exp(-tt * d_ref[...])

    return pl.pallas_call(
        out_kern,
        grid=(2 * D // bn,),
        in_specs=[pl.BlockSpec((Ls, order), lambda j: (0, 0)),
                  pl.BlockSpec((order, bn), lambda j: (0, j)),
                  pl.BlockSpec((1, bn), lambda j: (0, j))],
        out_specs=pl.BlockSpec((Ls, bn), lambda j: (0, j)),
        out_shape=jax.ShapeDtypeStruct((Ls, 2 * D), F32),
        compiler_params=_params(("parallel",)),
        name="hyena_filter_out",
    )(hff, p["f_wout"], deltas2)


def hyena_filter_spectrum(hfb, f_bias, fwd, Ls, D):
    n = 2 * Ls
    cb = LANES
    ncb = D // cb

    def kern(w_ref, hf_ref, hb_ref, fb_ref, o_ref):
        row = lax.broadcasted_iota(jnp.int32, (Ls, 1), 0)
        hb = jnp.where(row == 0, 0.0, hb_ref[...])
        ab = jnp.dot(w_ref[...], jnp.concatenate([hf_ref[...], hb], axis=1).astype(BF16),
                     preferred_element_type=F32)
        a, b = ab[:, :cb], ab[:, cb:]
        frow = lax.broadcasted_iota(jnp.int32, (n, 1), 0)
        real = frow <= Ls
        o_ref[...] = a + jnp.where(real, b + fb_ref[...], -b)

    return pl.pallas_call(
        kern,
        grid=(ncb,),
        in_specs=[pl.BlockSpec((n, Ls), lambda j: (0, 0), pipeline_mode=pl.Buffered(1)),
                  pl.BlockSpec((Ls, cb), lambda j: (0, j)),
                  pl.BlockSpec((Ls, cb), lambda j: (0, ncb + j)),
                  pl.BlockSpec((1, cb), lambda j: (0, j))],
        out_specs=pl.BlockSpec((n, cb), lambda j: (0, j)),
        out_shape=jax.ShapeDtypeStruct((n, D), F32),
        compiler_params=_params(("parallel",)),
        name="hyena_filter_spectrum",
    )(fwd, hfb, hfb, f_bias.reshape(1, D))


def _short_conv(x, w, b):
    n = x.shape[0]
    row = lax.broadcasted_iota(jnp.int32, (n, 1), 0)
    prev = jnp.where(row == 0, 0.0, pltpu.roll(x, 1, 0))
    nxt = jnp.where(row == n - 1, 0.0, pltpu.roll(x, n - 1, 0))
    return prev * w[0:1] + x * w[1:2] + nxt * w[2:3] + b


def hyena_segment(z, p, Ls, row_blk0, B, D):
    n = 2 * Ls
    cb = MXU_COLS
    ncb = D // cb
    fwd, inv = dft_matrices(Ls)
    hfb = hyena_filter_time(Ls, p)
    kf = hyena_filter_spectrum(hfb, p["f_bias"], fwd, Ls, D)
    conv_w, conv_b = p["conv_w"], p["conv_b"].reshape(1, 3 * D)

    def fwd_kern(w_ref, x1_ref, v_ref, cw1_ref, cb1_ref, cwv_ref, cbv_ref, kf_ref, y_ref):
        g = (_short_conv(v_ref[...].astype(F32), cwv_ref[...], cbv_ref[...])
             * _short_conv(x1_ref[...].astype(F32), cw1_ref[...], cb1_ref[...])).astype(BF16)
        u = jnp.dot(w_ref[...], g, preferred_element_type=F32)
        ure, uim = u[:Ls], u[Ls:]
        kre, kim = kf_ref[:Ls, :], kf_ref[Ls:, :]
        first = lax.broadcasted_iota(jnp.int32, (Ls, 1), 0) == 0
        y_ref[:Ls, :] = (ure * kre - jnp.where(first, 0.0, uim * kim)).astype(BF16)
        y_ref[Ls:, :] = jnp.where(first, uim * kim, ure * kim + uim * kre).astype(BF16)

    y = pl.pallas_call(
        fwd_kern,
        grid=(ncb, B),
        in_specs=[pl.BlockSpec((n, Ls), lambda j, b: (0, 0), pipeline_mode=pl.Buffered(1)),
                  pl.BlockSpec((Ls, cb), lambda j, b: (row_blk0 + b, ncb + j)),
                  pl.BlockSpec((Ls, cb), lambda j, b: (row_blk0 + b, 2 * ncb + j)),
                  pl.BlockSpec((HY_SHORT, cb), lambda j, b: (0, ncb + j)),
                  pl.BlockSpec((1, cb), lambda j, b: (0, ncb + j)),
                  pl.BlockSpec((HY_SHORT, cb), lambda j, b: (0, 2 * ncb + j)),
                  pl.BlockSpec((1, cb), lambda j, b: (0, 2 * ncb + j)),
                  pl.BlockSpec((n, cb), lambda j, b: (0, j))],
        out_specs=pl.BlockSpec((None, n, cb), lambda j, b: (b, 0, j)),
        out_shape=jax.ShapeDtypeStruct((B, n, D), BF16),
        compiler_params=_params(("parallel", "arbitrary")),
        name="hyena_dft",
    )(fwd, z, z, conv_w, conv_b, conv_w, conv_b, kf)

    def inv_kern(w_ref, y_ref, x0_ref, cw0_ref, cb0_ref, o_ref):
        conv = jnp.dot(w_ref[...], y_ref[...], preferred_element_type=F32)
        o_ref[...] = (conv * _short_conv(x0_ref[...].astype(F32), cw0_ref[...], cb0_ref[...])).astype(BF16)

    return pl.pallas_call(
        inv_kern,
        grid=(ncb, B),
        in_specs=[pl.BlockSpec((Ls, n), lambda j, b: (0, 0), pipeline_mode=pl.Buffered(1)),
                  pl.BlockSpec((None, n, cb), lambda j, b: (b, 0, j)),
                  pl.BlockSpec((Ls, cb), lambda j, b: (row_blk0 + b, j)),
                  pl.BlockSpec((HY_SHORT, cb), lambda j, b: (0, j)),
                  pl.BlockSpec((1, cb), lambda j, b: (0, j))],
        out_specs=pl.BlockSpec((Ls, cb), lambda j, b: (b, j)),
        out_shape=jax.ShapeDtypeStruct((B * Ls, D), BF16),
        compiler_params=_params(("parallel", "arbitrary")),
        name="hyena_idft",
    )(inv, y, z, conv_w, conv_b)


def hyena_mixer(h, u, mod, layer, p, geo, need_ctx, norm):
    B, C, L, rows_all, rows_out = geo
    D = h_width(h)
    z = plain_matmul(u, p["w_in"].astype(BF16), rows_out, COL_TILE, bias=p["b_in"], name="hyena_in")
    y = hyena_segment(z, p, L, 0, B, D)
    if need_ctx:
        y = (y, hyena_segment(z, p, C, B * L // C, B, D))
    return gated_matmul(y, p["w_out"].astype(BF16), h, mod, layer, 2, _out_rows(rows_out), bn=D, bias=p["b_out"],
                        norm=norm, name="hyena_out")


def kernel(x, c, ctx, c_ctx, ada_w, ada_b, norm_g, mlp_w1, mlp_w2, final_norm_g, mla_w_dq, mla_q_norm_g, mla_w_uq, mla_w_dkv, mla_kv_norm_g, mla_w_ukv, mla_w_o, hy_w_in, hy_b_in, hy_conv_w, hy_conv_b, hy_filt_w1, hy_filt_b1, hy_filt_w2, hy_filt_b2, hy_filt_w3, hy_filt_b3, hy_filt_freq, hy_filt_wout, hy_filt_bias, hy_w_out, hy_b_out, df_w_qkv, df_lambda, df_subln_g, df_w_o):
    B, L, D = x.shape
    C = ctx.shape[1]
    depth = ada_w.shape[0]
    bm = min(ROW_TILE, B * C)
    rows_all = Rows(B, C, L, bm)
    rows_lat = Rows(B, C, L, bm, lat_only=True)

    mod = adaln_table(jnp.concatenate([c_ctx[None, :], c], axis=0), ada_w, ada_b)
    norm_g4 = norm_g.reshape(depth, 2, 1, D)
    w1s, w2s = mlp_w1.astype(BF16), mlp_w2.astype(BF16)
    h = (x.reshape(B * L, D), ctx.reshape(B * C, D))
    u = norm_modulate(h, norm_g4, mod, 0, 0, rows_all)

    for i in range(depth):
        need_ctx = i < depth - 1
        rows_out = rows_all if need_ctx else rows_lat
        geo = (B, C, L, rows_all, rows_out)
        kind, j = i % N_MIXERS, i // N_MIXERS
        mlp_norm = ("modulate", norm_g4, i, 1)
        if kind == 0:
            p = dict(w_dq=mla_w_dq[j], q_g=mla_q_norm_g[j], w_uq=mla_w_uq[j], w_dkv=mla_w_dkv[j],
                     kv_g=mla_kv_norm_g[j], w_ukv=mla_w_ukv[j], w_o=mla_w_o[j])
            h, u = mla_mixer(h, u, mod, i, p, geo, need_ctx, mlp_norm)
        elif kind == 1:
            p = dict(w_in=hy_w_in[j], b_in=hy_b_in[j], conv_w=hy_conv_w[j], conv_b=hy_conv_b[j],
                     f_w1=hy_filt_w1[j], f_b1=hy_filt_b1[j], f_w2=hy_filt_w2[j], f_b2=hy_filt_b2[j],
                     f_w3=hy_filt_w3[j], f_b3=hy_filt_b3[j], f_freq=hy_filt_freq[j], f_wout=hy_filt_wout[j],
                     f_bias=hy_filt_bias[j], w_out=hy_w_out[j], b_out=hy_b_out[j])
            h, u = hyena_mixer(h, u, mod, i, p, geo, need_ctx, mlp_norm)
        else:
            lambda_init = 0.8 - 0.6 * math.exp(-0.3 * i)
            p = dict(w_qkv=df_w_qkv[j], lambdas=df_lambda[j], subln_g=df_subln_g[j], w_o=df_w_o[j])
            h, u = diff_mixer(h, u, mod, i, p, geo, need_ctx, lambda_init, mlp_norm)
        next_norm = ("modulate", norm_g4, i + 1, 0) if need_ctx else ("final", final_norm_g)
        res = mlp(h, u, mod, i, (w1s, i), (w2s, i), rows_out, next_norm)
        if need_ctx:
            h, u = res

    return res.reshape(B, L, D)
```

```python
import functools
import math

import jax
import jax.numpy as jnp
import numpy as np
from jax import lax
from jax.experimental import pallas as pl
from jax.experimental.pallas import tpu as pltpu

F32 = jnp.float32
BF16 = jnp.bfloat16
HIGHEST = lax.Precision.HIGHEST
LOG2E = math.log2(math.e)

GRID_W = 64
ROPE_THETA = 10000.0
NORM_EPS = 1e-6
N_MIXERS = 3

MLA_NOPE = 128
MLA_ROPE = 64
MLA_V = 128
MLA_KV_RANK = 512
MLA_HEAD_PAD = 256
MLA_Q_TILE = 1024

HY_SHORT = 3
HY_EMB_DIM = 33
HY_EMB_PAD = 64
HY_TARGET = 1e-2
HY_FAST_PCT = 0.3
HY_SLOW_PCT = 1.5

DF_HEAD_DIM = 128
DF_SUBLN_EPS = 1e-5
DF_Q_TILE = 512

LANES = 128
MXU_COLS = 256
VMEM_LIMIT_MB = 56

ROW_TILE = 1024
COL_TILE = 2048
NORM_ROW_TILE = 512
K_TILE = 1024


def _params(semantics, vmem_mb=VMEM_LIMIT_MB):
    return pltpu.CompilerParams(dimension_semantics=semantics, vmem_limit_bytes=vmem_mb << 20)


class Rows:
    def __init__(self, B, C, L, bm, lat_only=False):
        assert L % bm == 0 and (B * C) % bm == 0
        self.geometry = (B, C, L, lat_only)
        self.bm = bm
        self.tiles_per_batch = L // bm
        self.n_lat = B * L // bm
        self.n = self.n_lat + (0 if lat_only else B * C // bm)

    def with_bm(self, bm):
        B, C, L, lat_only = self.geometry
        return Rows(B, C, L, bm, lat_only)

    def mod_row(self, i):
        return jnp.where(i < self.n_lat, 1 + i // self.tiles_per_batch, 0)

    def pos_block(self, i):
        return jnp.where(i < self.n_lat, i % self.tiles_per_batch, self.tiles_per_batch)


def _mod_spec(layer, chunk, rows, bn, col_of):
    return pl.BlockSpec((None, None, None, 1, bn),
                        lambda *g: (layer, chunk, rows.mod_row(g[0]), 0, col_of(*g)))


def _w_shape(w):
    return w[0].shape[1:] if isinstance(w, tuple) else w.shape


def _row_operand(arr, rows, width, col_of):
    bm = rows.bm
    n_lat = rows.n_lat
    if isinstance(arr, tuple) and rows.n == n_lat:
        arr = arr[0]
    if not isinstance(arr, tuple):
        return [(arr, (bm, width), lambda *g: (g[0], col_of(*g)))], lambda refs: refs[0][...]
    specs = [(arr[0], (bm, width), lambda *g: (jnp.minimum(g[0], n_lat - 1), col_of(*g))),
             (arr[1], (bm, width), lambda *g: (jnp.maximum(g[0] - n_lat, 0), col_of(*g)))]
    return specs, lambda refs: jnp.where(pl.program_id(0) < n_lat, refs[0][...], refs[1][...])


def fused_matmul(a, w, extras, outs, epilogue, *, rows, bn, bk=None, name):
    K, N = _w_shape(w)
    bk = bk or K
    nk = K // bk
    assert K % bk == 0 and N % bn == 0
    a_specs, read_a = _row_operand(a, rows, bk, lambda i, j, k: k)
    n_a = len(a_specs)
    in_specs = [pl.BlockSpec(bs, im) for _, bs, im in a_specs]
    if isinstance(w, tuple):
        w, w_layer = w
        in_specs += [pl.BlockSpec((None, bk, bn), lambda i, j, k: (w_layer, k, j))]
    else:
        in_specs += [pl.BlockSpec((bk, bn), lambda i, j, k: (k, j))]
    in_specs += [pl.BlockSpec(bs, im) for _, bs, im in extras]
    n_ex = len(extras)
    n_out = len(outs)

    def kern(*refs):
        w_ref = refs[n_a]
        ex = refs[n_a + 1:n_a + 1 + n_ex]
        out = refs[n_a + 1 + n_ex:n_a + 1 + n_ex + n_out]

        def dot():
            return jnp.dot(read_a(refs[:n_a]), w_ref[...], preferred_element_type=F32)

        if nk == 1:
            epilogue(dot(), ex, out)
        else:
            acc_ref = out[0]
            k = pl.program_id(2)

            @pl.when(k == 0)
            def _():
                acc_ref[...] = dot()

            @pl.when((k > 0) & (k < nk - 1))
            def _():
                acc_ref[...] += dot()

            @pl.when(k == nk - 1)
            def _():
                epilogue(acc_ref[...] + dot(), ex, out)

    res = pl.pallas_call(
        kern,
        grid=(rows.n, N // bn, nk),
        in_specs=in_specs,
        out_specs=[pl.BlockSpec(bs, im) for _, _, bs, im in outs],
        out_shape=[jax.ShapeDtypeStruct(s, d) for s, d, _, _ in outs],
        compiler_params=_params(("parallel", "arbitrary", "arbitrary")),
        name=name,
    )(*[e[0] for e in a_specs], w, *[e[0] for e in extras])
    return res


def _rot_half(x, quarter):
    n = x.shape[-1]
    lane = lax.broadcasted_iota(jnp.int32, x.shape, x.ndim - 1)
    first = (lane % (2 * quarter)) < quarter
    return jnp.where(first, -pltpu.roll(x, n - quarter, x.ndim - 1), pltpu.roll(x, quarter, x.ndim - 1))


def _rms(x, g, eps):
    return x * lax.rsqrt(jnp.mean(x * x, axis=-1, keepdims=True) + eps) * g


def gated_matmul(a, w, resid, mod, layer, chunk, rows, *, name, bn=COL_TILE, bk=None, bias=None, norm=None):
    T = rows.n * rows.bm
    bm = rows.bm
    N = _w_shape(w)[1]
    res_specs, read_res = _row_operand(resid, rows, bn, lambda i, j, k: j)
    nk = _w_shape(w)[0] // bk if bk else 1
    if nk > 1:
        res_specs = [(arr, bs, (lambda im: lambda i, j, k: im(jnp.where(k >= nk // 2, i, jnp.maximum(i - 1, 0)),
                                                               j, k))(im)) for arr, bs, im in res_specs]
    n_res = len(res_specs)
    extras = res_specs + [(mod,) + _spec_parts(_mod_spec(layer, chunk, rows, bn, lambda i, j, k: j))]
    if bias is not None:
        extras.append((bias.reshape(1, N), (1, bn), lambda i, j, k: (0, j)))
    n_fix = len(extras)
    tile = lambda i, j, k: (i, j)
    outs = [((T, N), F32, (bm, bn), tile)]
    if norm is not None:
        assert bn == N
        if norm[0] == "modulate":
            _, g4, nl, which = norm
            extras += [(g4, (None, None, 1, N), lambda i, j, k: (nl, which, 0, 0)),
                       (mod,) + _spec_parts(_mod_spec(nl, 3 * which, rows, N, lambda i, j, k: 0)),
                       (mod,) + _spec_parts(_mod_spec(nl, 3 * which + 1, rows, N, lambda i, j, k: 0))]
            outs.append(((T, N), BF16, (bm, bn), tile))
        else:
            extras.append((norm[1].reshape(1, N), (1, N), lambda i, j, k: (0, 0)))

    def epi(acc, ex, out):
        y = acc if bias is None else acc + ex[n_res + 1][...]
        hn = read_res(ex[:n_res]) + ex[n_res][...] * y
        if norm is None:
            out[0][...] = hn
        elif norm[0] == "modulate":
            g_ref, sh_ref, sc_ref = ex[n_fix:]
            out[0][...] = hn
            out[1][...] = (_rms(hn, g_ref[...], NORM_EPS) * (1.0 + sc_ref[...]) + sh_ref[...]).astype(BF16)
        else:
            out[0][...] = _rms(hn, ex[n_fix][...], NORM_EPS)

    res = fused_matmul(a, w, extras, outs, epi, rows=rows, bn=bn, bk=bk, name=name)
    return res if len(res) > 1 else res[0]


def _spec_parts(spec):
    return spec.block_shape, spec.index_map


def h_width(h):
    return (h[0] if isinstance(h, tuple) else h).shape[1]


def _out_rows(rows):
    return rows.with_bm(min(NORM_ROW_TILE, rows.bm))


def adaln_table(cvec, ada_w, ada_b):
    depth, D, _ = ada_w.shape
    R = cvec.shape[0]

    def kern(c_ref, w_ref, b_ref, o_ref):
        c = c_ref[...]
        s = c / (1.0 + jnp.exp(-c))
        o_ref[...] = jnp.dot(s.astype(BF16), w_ref[...].astype(BF16), preferred_element_type=F32) + b_ref[...]

    out = pl.pallas_call(
        kern,
        grid=(depth, 6),
        in_specs=[pl.BlockSpec((R, D), lambda l, j: (0, 0)),
                  pl.BlockSpec((None, D, D), lambda l, j: (l, 0, j)),
                  pl.BlockSpec((None, 1, D), lambda l, j: (l, 0, j))],
        out_specs=pl.BlockSpec((None, None, R, D), lambda l, j: (l, j, 0, 0)),
        out_shape=jax.ShapeDtypeStruct((depth, 6, R, D), F32),
        compiler_params=_params(("parallel", "arbitrary")),
        name="adaln_table",
    )(cvec, ada_w, ada_b.reshape(depth, 1, 6 * D))
    return out.reshape(depth, 6, R, 1, D)


def norm_modulate(h, norm_g4, mod, layer, which, rows):
    D = norm_g4.shape[-1]
    bm = rows.bm
    h_specs, read_h = _row_operand(h, rows, D, lambda i: 0)
    n_h = len(h_specs)

    def kern(*refs):
        g_ref, sh_ref, sc_ref, o_ref = refs[n_h:]
        y = _rms(read_h(refs[:n_h]), g_ref[...], NORM_EPS)
        o_ref[...] = (y * (1.0 + sc_ref[...]) + sh_ref[...]).astype(BF16)

    return pl.pallas_call(
        kern,
        grid=(rows.n,),
        in_specs=[pl.BlockSpec(bs, im) for _, bs, im in h_specs]
        + [pl.BlockSpec((None, None, 1, D), lambda i: (layer, which, 0, 0)),
           _mod_spec(layer, 3 * which, rows, D, lambda i: 0),
           _mod_spec(layer, 3 * which + 1, rows, D, lambda i: 0)],
        out_specs=pl.BlockSpec((bm, D), lambda i: (i, 0)),
        out_shape=jax.ShapeDtypeStruct((rows.n * bm, D), BF16),
        compiler_params=_params(("parallel",)),
        name="norm_modulate",
    )(*[e[0] for e in h_specs], norm_g4, mod, mod)


def mlp(h, u, mod, layer, w1, w2, rows, norm):
    T = rows.n * rows.bm
    F, D = _w_shape(w2)

    def relu2(acc, ex, out):
        r = jnp.maximum(acc, 0.0)
        out[0][...] = (r * r).astype(BF16)

    hid = fused_matmul(u, w1, [], [((T, F), BF16, (rows.bm, COL_TILE), lambda i, j, k: (i, j))], relu2,
                       rows=rows, bn=COL_TILE, name="mlp_up")[0]
    return gated_matmul(hid, w2, h, mod, layer, 5, rows, bn=D, bk=K_TILE, norm=norm, name="mlp_down")


def axial_rope_tables(L, rot_dim):
    rows = L // GRID_W
    row = jnp.repeat(jnp.arange(rows, dtype=F32), GRID_W)
    col = jnp.tile(jnp.arange(GRID_W, dtype=F32), rows)
    pos = jnp.stack([row, col], axis=-1)
    n_freq = rot_dim // 4
    inv_freq = ROPE_THETA ** (-jnp.arange(n_freq, dtype=F32) / n_freq)
    ang = pos[:, :, None, None] * inv_freq
    ang = jnp.broadcast_to(ang, (L, 2, 2, n_freq)).reshape(L, rot_dim)
    return jnp.cos(ang), jnp.sin(ang)


def rope_tables_padded(L, rot_dim, bm):
    cos, sin = axial_rope_tables(L, rot_dim)
    cos = jnp.pad(cos, ((0, bm), (0, LANES - rot_dim)), constant_values=1.0)
    sin = jnp.pad(sin, ((0, bm), (0, LANES - rot_dim)))
    return cos, sin


def mla_down(u, w_dq, q_g, w_dkv, kv_g, cos, sin, rows):
    T = rows.n * rows.bm
    bm = rows.bm
    qr = w_dq.shape[1]
    wd = jnp.concatenate([w_dq, w_dkv, jnp.zeros((w_dq.shape[0], LANES - MLA_ROPE), w_dq.dtype)],
                         axis=1).astype(BF16)
    n_all = wd.shape[1]
    c0 = qr + MLA_KV_RANK

    def epi(acc, ex, out):
        qg_ref, kvg_ref, cos_ref, sin_ref = ex
        out[0][...] = _rms(acc[:, :qr], qg_ref[...], NORM_EPS).astype(BF16)
        out[1][...] = _rms(acc[:, qr:c0], kvg_ref[...], NORM_EPS).astype(BF16)
        kr = acc[:, c0:]
        out[2][...] = (kr * cos_ref[...] + _rot_half(kr, MLA_ROPE // 4) * sin_ref[...]).astype(BF16)

    extras = [(q_g.reshape(1, qr), (1, qr), lambda i, j, k: (0, 0)),
              (kv_g.reshape(1, MLA_KV_RANK), (1, MLA_KV_RANK), lambda i, j, k: (0, 0)),
              (cos, (bm, LANES), lambda i, j, k: (rows.pos_block(i), 0)),
              (sin, (bm, LANES), lambda i, j, k: (rows.pos_block(i), 0))]
    outs = [((T, qr), BF16, (bm, qr), lambda i, j, k: (i, 0)),
            ((T, MLA_KV_RANK), BF16, (bm, MLA_KV_RANK), lambda i, j, k: (i, 0)),
            ((T, LANES), BF16, (bm, LANES), lambda i, j, k: (i, 0))]
    return fused_matmul(u, wd, extras, outs, epi, rows=rows, bn=n_all, name="mla_down")


def mla_queries(cq, w_uq, cos, sin, rows, heads):
    T = rows.n * rows.bm
    bm = rows.bm
    qr = w_uq.shape[0]
    hd = MLA_NOPE + MLA_ROPE
    w = w_uq.reshape(qr, heads, hd)
    w = jnp.pad(w, ((0, 0), (0, 0), (0, MLA_HEAD_PAD - hd))).reshape(qr, heads * MLA_HEAD_PAD).astype(BF16)
    scale = hd ** -0.5 * LOG2E
    bn = COL_TILE

    def epi(acc, ex, out):
        cos_ref, sin_ref = ex
        for hh in range(bn // MLA_HEAD_PAD):
            c = hh * MLA_HEAD_PAD
            out[0][:, c:c + LANES] = (acc[:, c:c + LANES] * scale).astype(BF16)
            r = acc[:, c + LANES:c + 2 * LANES]
            r = r * cos_ref[...] + _rot_half(r, MLA_ROPE // 4) * sin_ref[...]
            out[0][:, c + LANES:c + 2 * LANES] = (r * scale).astype(BF16)

    extras = [(cos, (bm, LANES), lambda i, j, k: (rows.pos_block(i), 0)),
              (sin, (bm, LANES), lambda i, j, k: (rows.pos_block(i), 0))]
    outs = [((T, heads * MLA_HEAD_PAD), BF16, (bm, bn), lambda i, j, k: (i, j))]
    return fused_matmul(cq, w, extras, outs, epi, rows=rows, bn=bn, name="mla_queries")[0]


def plain_matmul(a, w, rows, bn, *, name, out_dtype=BF16, bias=None):
    T = rows.n * rows.bm
    N = w.shape[1]
    extras = [] if bias is None else [(bias.reshape(1, N), (1, bn), lambda i, j, k: (0, j))]

    def epi(acc, ex, out):
        y = acc if bias is None else acc + ex[0][...]
        out[0][...] = y.astype(out_dtype)

    return fused_matmul(a, w, extras, [((T, N), out_dtype, (rows.bm, bn), lambda i, j, k: (i, j))], epi,
                        rows=rows, bn=bn, name=name)[0]


def mla_attention(q, kv, kr, B, C, L, heads, ctx_queries):
    hp = 2
    tq = C if ctx_queries else MLA_Q_TILE
    n_t = 1 if ctx_queries else L // tq
    ctx_blk = B * L // C
    S = C if ctx_queries else C + L
    kvw = 2 * LANES
    q_row = (lambda b, g, t: ctx_blk + b) if ctx_queries else (lambda b, g, t: b * n_t + t)

    def kern(*refs):
        if ctx_queries:
            q_ref, kvc_ref, krc_ref, o_ref, k_s, v_s = refs
            parts = [(kvc_ref, krc_ref, 0, C)]
        else:
            q_ref, kvl_ref, kvc_ref, krl_ref, krc_ref, o_ref, k_s, v_s = refs
            parts = [(kvc_ref, krc_ref, 0, C), (kvl_ref, krl_ref, C, S)]

        @pl.when(pl.program_id(2) == 0)
        def _():
            for hh in range(hp):
                c0 = hh * kvw
                for kv_ref, kr_ref, lo, hi in parts:
                    k_s[hh, lo:hi, :LANES] = kv_ref[:, c0:c0 + LANES]
                    k_s[hh, lo:hi, LANES:] = kr_ref[...]
                    v_s[hh, lo:hi, :MLA_V] = kv_ref[:, c0 + LANES:c0 + kvw]
                v_s[hh, :, MLA_V:] = jnp.ones((S, MXU_COLS - MLA_V), BF16)

        ss = [lax.dot_general(q_ref[:, hh * MLA_HEAD_PAD:(hh + 1) * MLA_HEAD_PAD], k_s[hh],
                              (((1,), (1,)), ((), ())), preferred_element_type=F32) for hh in range(hp)]
        ps = [jnp.exp2(s - jnp.max(s, axis=-1, keepdims=True)).astype(BF16) for s in ss]
        for hh in range(hp):
            ov = jnp.dot(ps[hh], v_s[hh], preferred_element_type=F32)
            o_ref[:, hh * MLA_V:(hh + 1) * MLA_V] = (ov[:, :MLA_V] / ov[:, MLA_V:2 * MLA_V]).astype(BF16)

    lat_specs = [pl.BlockSpec((L, hp * kvw), lambda b, g, t: (b, g)),
                 pl.BlockSpec((L, LANES), lambda b, g, t: (b, 0))]
    ctx_specs = [pl.BlockSpec((C, hp * kvw), lambda b, g, t: (ctx_blk + b, g)),
                 pl.BlockSpec((C, LANES), lambda b, g, t: (ctx_blk + b, 0))]
    q_spec = pl.BlockSpec((tq, hp * MLA_HEAD_PAD), lambda b, g, t: (q_row(b, g, t), g))
    if ctx_queries:
        in_specs, args = [q_spec] + ctx_specs, (q, kv, kr)
    else:
        in_specs, args = [q_spec, lat_specs[0], ctx_specs[0], lat_specs[1], ctx_specs[1]], (q, kv, kv, kr, kr)
    return pl.pallas_call(
        kern,
        grid=(B, heads // hp, n_t),
        in_specs=in_specs,
        out_specs=pl.BlockSpec((tq, hp * MLA_V), lambda b, g, t: (b * n_t + t, g)),
        out_shape=jax.ShapeDtypeStruct((B * n_t * tq, heads * MLA_V), BF16),
        scratch_shapes=[pltpu.VMEM((hp, S, MLA_HEAD_PAD), BF16), pltpu.VMEM((hp, S, MXU_COLS), BF16)],
        compiler_params=_params(("parallel", "parallel", "arbitrary")),
        name="mla_attention_ctx" if ctx_queries else "mla_attention",
    )(*args)


def mla_mixer(h, u, mod, layer, p, geo, need_ctx, norm):
    B, C, L, rows_all, rows_out = geo
    heads = p["w_uq"].shape[1] // (MLA_NOPE + MLA_ROPE)
    cos, sin = rope_tables_padded(L, MLA_ROPE, rows_all.bm)
    cq, ckv, kr = mla_down(u, p["w_dq"], p["q_g"], p["w_dkv"], p["kv_g"], cos, sin, rows_all)
    kv = plain_matmul(ckv, p["w_ukv"].astype(BF16), rows_all, p["w_ukv"].shape[1], name="mla_kv")
    q = mla_queries(cq, p["w_uq"], cos, sin, rows_out, heads)
    o = mla_attention(q, kv, kr, B, C, L, heads, False)
    if need_ctx:
        o = (o, mla_attention(q, kv, kr, B, C, L, heads, True))
    return gated_matmul(o, p["w_o"].astype(BF16), h, mod, layer, 2, _out_rows(rows_out), bn=h_width(h), norm=norm,
                        name="mla_out")


def _diff_pair_layout():
    quarter = DF_HEAD_DIM // 4
    n = np.arange(2 * DF_HEAD_DIM)
    hf, c, a, r = n // DF_HEAD_DIM, (n % DF_HEAD_DIM) // (2 * quarter), (n % (2 * quarter)) // quarter, n % quarter
    head_perm = c * DF_HEAD_DIM + a * 2 * quarter + hf * quarter + r
    table_cols = (a * 2 * quarter + r)[:DF_HEAD_DIM]
    return head_perm, table_cols


def diff_rope_tables(L, bm):
    cos, sin = axial_rope_tables(L, DF_HEAD_DIM)
    _, cols = _diff_pair_layout()
    cos = jnp.pad(cos[:, cols], ((0, bm), (0, 0)), constant_values=1.0)
    sin = jnp.pad(sin[:, cols], ((0, bm), (0, 0)))
    return cos, sin


def diff_qkv(u, w_qkv, cos, sin, rows):
    T = rows.n * rows.bm
    bm = rows.bm
    D = w_qkv.shape[0]
    bn = COL_TILE
    hw = 2 * DF_HEAD_DIM
    scale = DF_HEAD_DIM ** -0.5 * LOG2E
    n_q = D // bn
    head_perm, _ = _diff_pair_layout()
    qk_cols = (np.arange(2 * D) // hw * hw)[:, None].reshape(-1, hw) + head_perm[None, :]
    cols = np.concatenate([qk_cols.reshape(-1), np.arange(2 * D, 3 * D)])
    w = w_qkv[:, cols].astype(BF16)

    def epi(acc, ex, out):
        cos_ref, sin_ref = ex
        mul = jnp.where(pl.program_id(1) < n_q, scale, 1.0).astype(F32)
        cs, sn = cos_ref[...] * mul, sin_ref[...] * mul
        for s in range(bn // hw):
            x1 = acc[:, s * hw:s * hw + LANES]
            x2 = acc[:, s * hw + LANES:(s + 1) * hw]
            out[0][:, s * hw:s * hw + LANES] = (x1 * cs - x2 * sn).astype(BF16)
            out[0][:, s * hw + LANES:(s + 1) * hw] = (x2 * cs + x1 * sn).astype(BF16)

    def table_block(i, j, k):
        return jnp.where(j < 2 * n_q, rows.pos_block(i), rows.tiles_per_batch), 0

    extras = [(cos, (bm, LANES), table_block), (sin, (bm, LANES), table_block)]
    outs = [((T, 3 * D), BF16, (bm, bn), lambda i, j, k: (i, j))]
    return fused_matmul(u, w, extras, outs, epi, rows=rows, bn=bn, name="diff_qkv")[0]


def diff_attention(qkv, lambdas, subln_g, lambda_init, B, C, L, D, ctx_queries):
    hd = DF_HEAD_DIM
    hw = 2 * hd
    heads = D // hw
    hp = 2
    tq = C if ctx_queries else DF_Q_TILE
    n_t = 1 if ctx_queries else L // tq
    ctx_blk = B * L // C
    S = C if ctx_queries else C + L
    kcol = D // (hp * hw)
    vcol = 2 * D // (hp * hw)
    q_row = (lambda b, g, t: ctx_blk + b) if ctx_queries else (lambda b, g, t: b * n_t + t)

    def kern(*refs):
        if ctx_queries:
            q_ref, kc_ref, vc_ref, lam_ref, g_ref, o_ref, k_s, v_s = refs
            parts = [(kc_ref, vc_ref, 0, C)]
        else:
            q_ref, kl_ref, kc_ref, vl_ref, vc_ref, lam_ref, g_ref, o_ref, k_s, v_s = refs
            parts = [(kc_ref, vc_ref, 0, C), (kl_ref, vl_ref, C, S)]

        @pl.when(pl.program_id(2) == 0)
        def _():
            map0 = (lax.broadcasted_iota(jnp.int32, (1, hw), 1) % hd) < hd // 2
            for k_ref, v_ref, lo, hi in parts:
                for hh in range(hp):
                    k = k_ref[:, hh * hw:(hh + 1) * hw]
                    k_s[2 * hh, lo:hi, :] = jnp.where(map0, k, jnp.zeros_like(k))
                    k_s[2 * hh + 1, lo:hi, :] = jnp.where(map0, jnp.zeros_like(k), k)
                    v_s[hh, lo:hi, :] = v_ref[:, hh * hw:(hh + 1) * hw]

        lf = lam_ref[...]
        lam = (jnp.exp(jnp.sum(lf[0:1] * lf[1:2], axis=-1, keepdims=True))
               - jnp.exp(jnp.sum(lf[2:3] * lf[3:4], axis=-1, keepdims=True)) + lambda_init)

        nc = 2 * hp
        ss = [lax.dot_general(q_ref[:, (c // 2) * hw:(c // 2 + 1) * hw], k_s[c],
                              (((1,), (1,)), ((), ())), preferred_element_type=F32) for c in range(nc)]
        ms = [jnp.max(s, axis=-1, keepdims=True) for s in ss]
        os_, ls = [0.0] * nc, [0.0] * nc
        for lo in range(0, S, MXU_COLS):
            for c in range(nc):
                p = jnp.exp2(ss[c][:, lo:lo + MXU_COLS] - ms[c])
                ls[c] = ls[c] + jnp.sum(p[:, :LANES] + p[:, LANES:], axis=-1, keepdims=True)
                os_[c] = os_[c] + jnp.dot(p.astype(BF16), v_s[c // 2, lo:lo + MXU_COLS, :],
                                          preferred_element_type=F32)
        for hh in range(hp):
            o = os_[2 * hh] / ls[2 * hh] - lam * (os_[2 * hh + 1] / ls[2 * hh + 1])
            o_ref[:, hh * hw:(hh + 1) * hw] = (_rms(o, g_ref[...], DF_SUBLN_EPS)
                                               * (1.0 - lambda_init)).astype(BF16)

    bw = hp * hw
    q_spec = pl.BlockSpec((tq, bw), lambda b, g, t: (q_row(b, g, t), g))
    lat_specs = [pl.BlockSpec((L, bw), lambda b, g, t: (b, kcol + g)),
                 pl.BlockSpec((L, bw), lambda b, g, t: (b, vcol + g))]
    ctx_specs = [pl.BlockSpec((C, bw), lambda b, g, t: (ctx_blk + b, kcol + g)),
                 pl.BlockSpec((C, bw), lambda b, g, t: (ctx_blk + b, vcol + g))]
    par_specs = [pl.BlockSpec((4, hd), lambda b, g, t: (0, 0)), pl.BlockSpec((1, hw), lambda b, g, t: (0, 0))]
    if ctx_queries:
        in_specs, args = [q_spec] + ctx_specs, (qkv, qkv, qkv)
    else:
        in_specs = [q_spec, lat_specs[0], ctx_specs[0], lat_specs[1], ctx_specs[1]]
        args = (qkv, qkv, qkv, qkv, qkv)
    return pl.pallas_call(
        kern,
        grid=(B, heads // hp, n_t),
        in_specs=in_specs + par_specs,
        out_specs=pl.BlockSpec((tq, bw), lambda b, g, t: (b * n_t + t, g)),
        out_shape=jax.ShapeDtypeStruct((B * n_t * tq, D), BF16),
        scratch_shapes=[pltpu.VMEM((2 * hp, S, hw), BF16), pltpu.VMEM((hp, S, hw), BF16)],
        compiler_params=_params(("parallel", "parallel", "arbitrary")),
        name="diff_attention_ctx" if ctx_queries else "diff_attention",
    )(*args, lambdas, subln_g.reshape(1, hw))


def diff_mixer(h, u, mod, layer, p, geo, need_ctx, lambda_init, norm):
    B, C, L, rows_all, rows_out = geo
    D = h_width(h)
    cos, sin = diff_rope_tables(L, rows_all.bm)
    qkv = diff_qkv(u, p["w_qkv"], cos, sin, rows_all)
    o = diff_attention(qkv, p["lambdas"], p["subln_g"], lambda_init, B, C, L, D, False)
    if need_ctx:
        o = (o, diff_attention(qkv, p["lambdas"], p["subln_g"], lambda_init, B, C, L, D, True))
    return gated_matmul(o, p["w_o"].astype(BF16), h, mod, layer, 2, _out_rows(rows_out), bn=D, norm=norm,
                        name="diff_out")


def dft_matrices(Ls):
    n = 2 * Ls
    t0n = min(64, Ls)
    t1n = Ls // t0n
    f = jnp.arange(Ls, dtype=jnp.int32)[:, None]
    a1 = ((f * (jnp.arange(t1n, dtype=jnp.int32) * t0n)[None, :]) % n).astype(F32) * (2.0 * math.pi / n)
    a0 = ((f * jnp.arange(t0n, dtype=jnp.int32)[None, :]) % n).astype(F32) * (2.0 * math.pi / n)
    c1, s1 = jnp.cos(a1)[:, :, None], jnp.sin(a1)[:, :, None]
    c0, s0 = jnp.cos(a0)[:, None, :], jnp.sin(a0)[:, None, :]
    cosm = (c1 * c0 - s1 * s0).reshape(Ls, Ls)
    sinm = (s1 * c0 + c1 * s0).reshape(Ls, Ls)
    nyq = jnp.where(jnp.arange(Ls) % 2 == 0, 1.0, -1.0).astype(F32)[None, :]
    imag = jnp.where(f == 0, nyq, -sinm)
    fwd = jnp.concatenate([cosm, imag], axis=0)
    col = jnp.arange(n)
    cscale = jnp.where((col == 0) | (col == Ls), 1.0 / n, 2.0 / n).astype(F32)
    inv = fwd.T * cscale[None, :]
    return fwd.astype(BF16), inv.astype(BF16)


def hyena_filter_time(Ls, p):
    D = p["f_bias"].shape[0]
    order = p["f_w2"].shape[0]
    bands = (HY_EMB_DIM - 1) // 2
    t = jnp.linspace(0.0, 1.0, Ls, dtype=F32)[:, None]
    w = 2.0 * math.pi * jnp.arange(Ls, dtype=F32)[:, None] / Ls
    f = jnp.linspace(1e-4, bands - 1, bands, dtype=F32)
    feats = jnp.concatenate([t, jnp.cos(f * w), -jnp.sin(f * w)], axis=-1)
    feats = jnp.pad(feats, ((0, 0), (0, HY_EMB_PAD - HY_EMB_DIM)))
    w1 = jnp.pad(p["f_w1"], ((0, HY_EMB_PAD - HY_EMB_DIM), (0, 0)))
    deltas = jnp.abs(jnp.linspace(math.log(HY_TARGET) / HY_SLOW_PCT, math.log(HY_TARGET) / HY_FAST_PCT,
                                  D, dtype=F32))
    deltas2 = jnp.concatenate([deltas, deltas]).reshape(1, 2 * D)

    def ffn_kern(x_ref, w1_ref, b1_ref, w2_ref, b2_ref, w3_ref, b3_ref, fr_ref, o_ref):
        dot = functools.partial(jnp.dot, precision=HIGHEST, preferred_element_type=F32)
        fr = fr_ref[...]
        hcur = jnp.sin(fr[0:1] * (dot(x_ref[...], w1_ref[...]) + b1_ref[...]))
        hcur = jnp.sin(fr[1:2] * (dot(hcur, w2_ref[...]) + b2_ref[...]))
        o_ref[...] = jnp.sin(fr[2:3] * (dot(hcur, w3_ref[...]) + b3_ref[...]))

    full = lambda a: pl.BlockSpec(a.shape, lambda: (0,) * a.ndim)
    ffn_in = [feats, w1, p["f_b1"].reshape(1, order), p["f_w2"], p["f_b2"].reshape(1, order),
              p["f_w3"], p["f_b3"].reshape(1, order), p["f_freq"]]
    hff = pl.pallas_call(
        ffn_kern,
        in_specs=[full(a) for a in ffn_in],
        out_specs=pl.BlockSpec((Ls, order), lambda: (0, 0)),
        out_shape=jax.ShapeDtypeStruct((Ls, order), F32),
        name="hyena_filter_ffn",
    )(*ffn_in)

    bn = COL_TILE // 4

    def out_kern(h_ref, w_ref, d_ref, o_ref):
        tt = lax.broadcasted_iota(jnp.int32, (Ls, 1), 0).astype(F32) * (1.0 / (Ls - 1))
        hw = jnp.dot(h_ref[...], w_ref[...], precision=HIGHEST, preferred_element_type=F32)
        o_ref[...] = hw * jnp.exp(-tt * d_ref[...])

    return pl.pallas_call(
        out_kern,
        grid=(2 * D // bn,),
        in_specs=[pl.BlockSpec((Ls, order), lambda j: (0, 0)),
                  pl.BlockSpec((order, bn), lambda j: (0, j)),
                  pl.BlockSpec((1, bn), lambda j: (0, j))],
        out_specs=pl.BlockSpec((Ls, bn), lambda j: (0, j)),
        out_shape=jax.ShapeDtypeStruct((Ls, 2 * D), F32),
        compiler_params=_params(("parallel",)),
        name="hyena_filter_out",
    )(hff, p["f_wout"], deltas2)


def hyena_filter_spectrum(hfb, f_bias, fwd, Ls, D):
    n = 2 * Ls
    cb = LANES
    ncb = D // cb

    def kern(w_ref, hf_ref, hb_ref, fb_ref, o_ref):
        row = lax.broadcasted_iota(jnp.int32, (Ls, 1), 0)
        hb = jnp.where(row == 0, 0.0, hb_ref[...])
        ab = jnp.dot(w_ref[...], jnp.concatenate([hf_ref[...], hb], axis=1).astype(BF16),
                     preferred_element_type=F32)
        a, b = ab[:, :cb], ab[:, cb:]
        frow = lax.broadcasted_iota(jnp.int32, (n, 1), 0)
        real = frow <= Ls
        o_ref[...] = a + jnp.where(real, b + fb_ref[...], -b)

    return pl.pallas_call(
        kern,
        grid=(ncb,),
        in_specs=[pl.BlockSpec((n, Ls), lambda j: (0, 0), pipeline_mode=pl.Buffered(1)),
                  pl.BlockSpec((Ls, cb), lambda j: (0, j)),
                  pl.BlockSpec((Ls, cb), lambda j: (0, ncb + j)),
                  pl.BlockSpec((1, cb), lambda j: (0, j))],
        out_specs=pl.BlockSpec((n, cb), lambda j: (0, j)),
        out_shape=jax.ShapeDtypeStruct((n, D), F32),
        compiler_params=_params(("parallel",)),
        name="hyena_filter_spectrum",
    )(fwd, hfb, hfb, f_bias.reshape(1, D))


def _short_conv(x, w, b):
    n = x.shape[0]
    row = lax.broadcasted_iota(jnp.int32, (n, 1), 0)
    prev = jnp.where(row == 0, 0.0, pltpu.roll(x, 1, 0))
    nxt = jnp.where(row == n - 1, 0.0, pltpu.roll(x, n - 1, 0))
    return prev * w[0:1] + x * w[1:2] + nxt * w[2:3] + b


def hyena_segment(z, p, Ls, row_blk0, B, D):
    n = 2 * Ls
    cb = MXU_COLS
    ncb = D // cb
    fwd, inv = dft_matrices(Ls)
    hfb = hyena_filter_time(Ls, p)
    kf = hyena_filter_spectrum(hfb, p["f_bias"], fwd, Ls, D)
    conv_w, conv_b = p["conv_w"], p["conv_b"].reshape(1, 3 * D)

    def fwd_kern(w_ref, x1_ref, v_ref, cw1_ref, cb1_ref, cwv_ref, cbv_ref, kf_ref, y_ref):
        g = (_short_conv(v_ref[...].astype(F32), cwv_ref[...], cbv_ref[...])
             * _short_conv(x1_ref[...].astype(F32), cw1_ref[...], cb1_ref[...])).astype(BF16)
        u = jnp.dot(w_ref[...], g, preferred_element_type=F32)
        ure, uim = u[:Ls], u[Ls:]
        kre, kim = kf_ref[:Ls, :], kf_ref[Ls:, :]
        first = lax.broadcasted_iota(jnp.int32, (Ls, 1), 0) == 0
        y_ref[:Ls, :] = (ure * kre - jnp.where(first, 0.0, uim * kim)).astype(BF16)
        y_ref[Ls:, :] = jnp.where(first, uim * kim, ure * kim + uim * kre).astype(BF16)

    y = pl.pallas_call(
        fwd_kern,
        grid=(ncb, B),
        in_specs=[pl.BlockSpec((n, Ls), lambda j, b: (0, 0), pipeline_mode=pl.Buffered(1)),
                  pl.BlockSpec((Ls, cb), lambda j, b: (row_blk0 + b, ncb + j)),
                  pl.BlockSpec((Ls, cb), lambda j, b: (row_blk0 + b, 2 * ncb + j)),
                  pl.BlockSpec((HY_SHORT, cb), lambda j, b: (0, ncb + j)),
                  pl.BlockSpec((1, cb), lambda j, b: (0, ncb + j)),
                  pl.BlockSpec((HY_SHORT, cb), lambda j, b: (0, 2 * ncb + j)),
                  pl.BlockSpec((1, cb), lambda j, b: (0, 2 * ncb + j)),
                  pl.BlockSpec((n, cb), lambda j, b: (0, j))],
        out_specs=pl.BlockSpec((None, n, cb), lambda j, b: (b, 0, j)),
        out_shape=jax.ShapeDtypeStruct((B, n, D), BF16),
        compiler_params=_params(("parallel", "arbitrary")),
        name="hyena_dft",
    )(fwd, z, z, conv_w, conv_b, conv_w, conv_b, kf)

    def inv_kern(w_ref, y_ref, x0_ref, cw0_ref, cb0_ref, o_ref):
        conv = jnp.dot(w_ref[...], y_ref[...], preferred_element_type=F32)
        o_ref[...] = (conv * _short_conv(x0_ref[...].astype(F32), cw0_ref[...], cb0_ref[...])).astype(BF16)

    return pl.pallas_call(
        inv_kern,
        grid=(ncb, B),
        in_specs=[pl.BlockSpec((Ls, n), lambda j, b: (0, 0), pipeline_mode=pl.Buffered(1)),
                  pl.BlockSpec((None, n, cb), lambda j, b: (b, 0, j)),
                  pl.BlockSpec((Ls, cb), lambda j, b: (row_blk0 + b, j)),
                  pl.BlockSpec((HY_SHORT, cb), lambda j, b: (0, j)),
                  pl.BlockSpec((1, cb), lambda j, b: (0, j))],
        out_specs=pl.BlockSpec((Ls, cb), lambda j, b: (b, j)),
        out_shape=jax.ShapeDtypeStruct((B * Ls, D), BF16),
        compiler_params=_params(("parallel", "arbitrary")),
        name="hyena_idft",
    )(inv, y, z, conv_w, conv_b)


def hyena_mixer(h, u, mod, layer, p, geo, need_ctx, norm):
    B, C, L, rows_all, rows_out = geo
    D = h_width(h)
    z = plain_matmul(u, p["w_in"].astype(BF16), rows_out, COL_TILE, bias=p["b_in"], name="hyena_in")
    y = hyena_segment(z, p, L, 0, B, D)
    if need_ctx:
        y = (y, hyena_segment(z, p, C, B * L // C, B, D))
    return gated_matmul(y, p["w_out"].astype(BF16), h, mod, layer, 2, _out_rows(rows_out), bn=D, bias=p["b_out"],
                        norm=norm, name="hyena_out")


def kernel(x, c, ctx, c_ctx, ada_w, ada_b, norm_g, mlp_w1, mlp_w2, final_norm_g, mla_w_dq, mla_q_norm_g, mla_w_uq, mla_w_dkv, mla_kv_norm_g, mla_w_ukv, mla_w_o, hy_w_in, hy_b_in, hy_conv_w, hy_conv_b, hy_filt_w1, hy_filt_b1, hy_filt_w2, hy_filt_b2, hy_filt_w3, hy_filt_b3, hy_filt_freq, hy_filt_wout, hy_filt_bias, hy_w_out, hy_b_out, df_w_qkv, df_lambda, df_subln_g, df_w_o):
    B, L, D = x.shape
    C = ctx.shape[1]
    depth = ada_w.shape[0]
    bm = min(ROW_TILE, B * C)
    rows_all = Rows(B, C, L, bm)
    rows_lat = Rows(B, C, L, bm, lat_only=True)

    mod = adaln_table(jnp.concatenate([c_ctx[None, :], c], axis=0), ada_w, ada_b)
    norm_g4 = norm_g.reshape(depth, 2, 1, D)
    w1s, w2s = mlp_w1.astype(BF16), mlp_w2.astype(BF16)
    h = (x.reshape(B * L, D), ctx.reshape(B * C, D))
    u = norm_modulate(h, norm_g4, mod, 0, 0, rows_all)

    for i in range(depth):
        need_ctx = i < depth - 1
        rows_out = rows_all if need_ctx else rows_lat
        geo = (B, C, L, rows_all, rows_out)
        kind, j = i % N_MIXERS, i // N_MIXERS
        mlp_norm = ("modulate", norm_g4, i, 1)
        if kind == 0:
            p = dict(w_dq=mla_w_dq[j], q_g=mla_q_norm_g[j], w_uq=mla_w_uq[j], w_dkv=mla_w_dkv[j],
                     kv_g=mla_kv_norm_g[j], w_ukv=mla_w_ukv[j], w_o=mla_w_o[j])
            h, u = mla_mixer(h, u, mod, i, p, geo, need_ctx, mlp_norm)
        elif kind == 1:
            p = dict(w_in=hy_w_in[j], b_in=hy_b_in[j], conv_w=hy_conv_w[j], conv_b=hy_conv_b[j],
                     f_w1=hy_filt_w1[j], f_b1=hy_filt_b1[j], f_w2=hy_filt_w2[j], f_b2=hy_filt_b2[j],
                     f_w3=hy_filt_w3[j], f_b3=hy_filt_b3[j], f_freq=hy_filt_freq[j], f_wout=hy_filt_wout[j],
                     f_bias=hy_filt_bias[j], w_out=hy_w_out[j], b_out=hy_b_out[j])
            h, u = hyena_mixer(h, u, mod, i, p, geo, need_ctx, mlp_norm)
        else:
            lambda_init = 0.8 - 0.6 * math.exp(-0.3 * i)
            p = dict(w_qkv=df_w_qkv[j], lambdas=df_lambda[j], subln_g=df_subln_g[j], w_o=df_w_o[j])
            h, u = diff_mixer(h, u, mod, i, p, geo, need_ctx, lambda_init, mlp_norm)
        next_norm = ("modulate", norm_g4, i + 1, 0) if need_ctx else ("final", final_norm_g)
        res = mlp(h, u, mod, i, (w1s, i), (w2s, i), rows_out, next_norm)
        if need_ctx:
            h, u = res

    return res.reshape(B, L, D)
```

```python
import functools
import math

import jax
import jax.numpy as jnp
import numpy as np
from jax import lax
from jax.experimental import pallas as pl
from jax.experimental.pallas import tpu as pltpu

F32 = jnp.float32
BF16 = jnp.bfloat16
HIGHEST = lax.Precision.HIGHEST
LOG2E = math.log2(math.e)

GRID_W = 64
ROPE_THETA = 10000.0
NORM_EPS = 1e-6
N_MIXERS = 3

MLA_NOPE = 128
MLA_ROPE = 64
MLA_V = 128
MLA_KV_RANK = 512
MLA_HEAD_PAD = 256
MLA_Q_TILE = 1024

HY_SHORT = 3
HY_EMB_DIM = 33
HY_EMB_PAD = 64
HY_TARGET = 1e-2
HY_FAST_PCT = 0.3
HY_SLOW_PCT = 1.5

DF_HEAD_DIM = 128
DF_SUBLN_EPS = 1e-5
DF_Q_TILE = 512

LANES = 128
MXU_COLS = 256
VMEM_LIMIT_MB = 56

ROW_TILE = 1024
COL_TILE = 2048
NORM_ROW_TILE = 512
K_TILE = 1024


def _params(semantics, vmem_mb=VMEM_LIMIT_MB, fuse_inputs=None):
    return pltpu.CompilerParams(dimension_semantics=semantics, vmem_limit_bytes=vmem_mb << 20,
                                allow_input_fusion=fuse_inputs)


class Rows:
    def __init__(self, B, C, L, bm, lat_only=False):
        assert L % bm == 0 and (B * C) % bm == 0
        self.geometry = (B, C, L, lat_only)
        self.bm = bm
        self.tiles_per_batch = L // bm
        self.n_lat = B * L // bm
        self.n = self.n_lat + (0 if lat_only else B * C // bm)

    def with_bm(self, bm):
        B, C, L, lat_only = self.geometry
        return Rows(B, C, L, bm, lat_only)

    def mod_row(self, i):
        return jnp.where(i < self.n_lat, 1 + i // self.tiles_per_batch, 0)

    def pos_block(self, i):
        return jnp.where(i < self.n_lat, i % self.tiles_per_batch, self.tiles_per_batch)


def _mod_spec(layer, chunk, rows, bn, col_of):
    return pl.BlockSpec((None, None, None, 1, bn),
                        lambda *g: (layer, chunk, rows.mod_row(g[0]), 0, col_of(*g)))


def _w_shape(w):
    return w[0].shape[1:] if isinstance(w, tuple) else w.shape


def _row_operand(arr, rows, width, col_of):
    bm = rows.bm
    n_lat = rows.n_lat
    if isinstance(arr, tuple) and rows.n == n_lat:
        arr = arr[0]
    if not isinstance(arr, tuple):
        return [(arr, (bm, width), lambda *g: (g[0], col_of(*g)))], lambda refs: refs[0][...]
    specs = [(arr[0], (bm, width), lambda *g: (jnp.minimum(g[0], n_lat - 1), col_of(*g))),
             (arr[1], (bm, width), lambda *g: (jnp.maximum(g[0] - n_lat, 0), col_of(*g)))]
    return specs, lambda refs: jnp.where(pl.program_id(0) < n_lat, refs[0][...], refs[1][...])


def fused_matmul(a, w, extras, outs, epilogue, *, rows, bn, bk=None, name):
    K, N = _w_shape(w)
    bk = bk or K
    nk = K // bk
    assert K % bk == 0 and N % bn == 0
    a_specs, read_a = _row_operand(a, rows, bk, lambda i, j, k: k)
    n_a = len(a_specs)
    in_specs = [pl.BlockSpec(bs, im) for _, bs, im in a_specs]
    if isinstance(w, tuple):
        w, w_layer = w
        in_specs += [pl.BlockSpec((None, bk, bn), lambda i, j, k: (w_layer, k, j))]
    else:
        in_specs += [pl.BlockSpec((bk, bn), lambda i, j, k: (k, j))]
    in_specs += [pl.BlockSpec(bs, im) for _, bs, im in extras]
    n_ex = len(extras)
    n_out = len(outs)

    def kern(*refs):
        w_ref = refs[n_a]
        ex = refs[n_a + 1:n_a + 1 + n_ex]
        out = refs[n_a + 1 + n_ex:n_a + 1 + n_ex + n_out]

        def dot():
            return jnp.dot(read_a(refs[:n_a]), w_ref[...], preferred_element_type=F32)

        if nk == 1:
            epilogue(dot(), ex, out)
        else:
            acc_ref = out[0]
            k = pl.program_id(2)

            @pl.when(k == 0)
            def _():
                acc_ref[...] = dot()

            @pl.when((k > 0) & (k < nk - 1))
            def _():
                acc_ref[...] += dot()

            @pl.when(k == nk - 1)
            def _():
                epilogue(acc_ref[...] + dot(), ex, out)

    res = pl.pallas_call(
        kern,
        grid=(rows.n, N // bn, nk),
        in_specs=in_specs,
        out_specs=[pl.BlockSpec(bs, im) for _, _, bs, im in outs],
        out_shape=[jax.ShapeDtypeStruct(s, d) for s, d, _, _ in outs],
        compiler_params=_params(("parallel", "arbitrary", "arbitrary"),
                                fuse_inputs=[False] * n_a + [True] + [False] * n_ex),
        name=name,
    )(*[e[0] for e in a_specs], w, *[e[0] for e in extras])
    return res


def _rot_half(x, quarter):
    n = x.shape[-1]
    lane = lax.broadcasted_iota(jnp.int32, x.shape, x.ndim - 1)
    first = (lane % (2 * quarter)) < quarter
    return jnp.where(first, -pltpu.roll(x, n - quarter, x.ndim - 1), pltpu.roll(x, quarter, x.ndim - 1))


def _rms(x, g, eps):
    return x * lax.rsqrt(jnp.mean(x * x, axis=-1, keepdims=True) + eps) * g


def gated_matmul(a, w, resid, mod, layer, chunk, rows, *, name, bn=COL_TILE, bk=None, bias=None, norm=None):
    T = rows.n * rows.bm
    bm = rows.bm
    N = _w_shape(w)[1]
    res_specs, read_res = _row_operand(resid, rows, bn, lambda i, j, k: j)
    n_res = len(res_specs)
    extras = res_specs + [(mod,) + _spec_parts(_mod_spec(layer, chunk, rows, bn, lambda i, j, k: j))]
    if bias is not None:
        extras.append((bias.reshape(1, N), (1, bn), lambda i, j, k: (0, j)))
    n_fix = len(extras)
    tile = lambda i, j, k: (i, j)
    outs = [((T, N), F32, (bm, bn), tile)]
    if norm is not None:
        assert bn == N
        if norm[0] == "modulate":
            _, g4, nl, which = norm
            extras += [(g4, (None, None, 1, N), lambda i, j, k: (nl, which, 0, 0)),
                       (mod,) + _spec_parts(_mod_spec(nl, 3 * which, rows, N, lambda i, j, k: 0)),
                       (mod,) + _spec_parts(_mod_spec(nl, 3 * which + 1, rows, N, lambda i, j, k: 0))]
            outs.append(((T, N), BF16, (bm, bn), tile))
        else:
            extras.append((norm[1].reshape(1, N), (1, N), lambda i, j, k: (0, 0)))

    def epi(acc, ex, out):
        y = acc if bias is None else acc + ex[n_res + 1][...]
        hn = read_res(ex[:n_res]) + ex[n_res][...] * y
        if norm is None:
            out[0][...] = hn
        elif norm[0] == "modulate":
            g_ref, sh_ref, sc_ref = ex[n_fix:]
            out[0][...] = hn
            out[1][...] = (_rms(hn, g_ref[...], NORM_EPS) * (1.0 + sc_ref[...]) + sh_ref[...]).astype(BF16)
        else:
            out[0][...] = _rms(hn, ex[n_fix][...], NORM_EPS)

    res = fused_matmul(a, w, extras, outs, epi, rows=rows, bn=bn, bk=bk, name=name)
    return res if len(res) > 1 else res[0]


def _spec_parts(spec):
    return spec.block_shape, spec.index_map


def h_width(h):
    return (h[0] if isinstance(h, tuple) else h).shape[1]


def _out_rows(rows):
    return rows.with_bm(min(NORM_ROW_TILE, rows.bm))


def adaln_table(cvec, ada_w, ada_b):
    depth, D, _ = ada_w.shape
    R = cvec.shape[0]

    def kern(c_ref, w_ref, b_ref, o_ref):
        c = c_ref[...]
        s = c / (1.0 + jnp.exp(-c))
        o_ref[...] = jnp.dot(s.astype(BF16), w_ref[...].astype(BF16), preferred_element_type=F32) + b_ref[...]

    out = pl.pallas_call(
        kern,
        grid=(depth, 6),
        in_specs=[pl.BlockSpec((R, D), lambda l, j: (0, 0)),
                  pl.BlockSpec((None, D, D), lambda l, j: (l, 0, j)),
                  pl.BlockSpec((None, 1, D), lambda l, j: (l, 0, j))],
        out_specs=pl.BlockSpec((None, None, R, D), lambda l, j: (l, j, 0, 0)),
        out_shape=jax.ShapeDtypeStruct((depth, 6, R, D), F32),
        compiler_params=_params(("parallel", "arbitrary")),
        name="adaln_table",
    )(cvec, ada_w, ada_b.reshape(depth, 1, 6 * D))
    return out.reshape(depth, 6, R, 1, D)


def norm_modulate(h, norm_g4, mod, layer, which, rows):
    D = norm_g4.shape[-1]
    bm = rows.bm
    h_specs, read_h = _row_operand(h, rows, D, lambda i: 0)
    n_h = len(h_specs)

    def kern(*refs):
        g_ref, sh_ref, sc_ref, o_ref = refs[n_h:]
        y = _rms(read_h(refs[:n_h]), g_ref[...], NORM_EPS)
        o_ref[...] = (y * (1.0 + sc_ref[...]) + sh_ref[...]).astype(BF16)

    return pl.pallas_call(
        kern,
        grid=(rows.n,),
        in_specs=[pl.BlockSpec(bs, im) for _, bs, im in h_specs]
        + [pl.BlockSpec((None, None, 1, D), lambda i: (layer, which, 0, 0)),
           _mod_spec(layer, 3 * which, rows, D, lambda i: 0),
           _mod_spec(layer, 3 * which + 1, rows, D, lambda i: 0)],
        out_specs=pl.BlockSpec((bm, D), lambda i: (i, 0)),
        out_shape=jax.ShapeDtypeStruct((rows.n * bm, D), BF16),
        compiler_params=_params(("parallel",)),
        name="norm_modulate",
    )(*[e[0] for e in h_specs], norm_g4, mod, mod)


def mlp(h, u, mod, layer, w1, w2, rows, norm):
    T = rows.n * rows.bm
    F, D = _w_shape(w2)

    def relu2(acc, ex, out):
        r = jnp.maximum(acc, 0.0)
        out[0][...] = (r * r).astype(BF16)

    hid = fused_matmul(u, w1, [], [((T, F), BF16, (rows.bm, COL_TILE), lambda i, j, k: (i, j))], relu2,
                       rows=rows, bn=COL_TILE, name="mlp_up")[0]
    return gated_matmul(hid, w2, h, mod, layer, 5, rows, bn=D, bk=K_TILE, norm=norm, name="mlp_down")


def axial_rope_tables(L, rot_dim):
    rows = L // GRID_W
    row = jnp.repeat(jnp.arange(rows, dtype=F32), GRID_W)
    col = jnp.tile(jnp.arange(GRID_W, dtype=F32), rows)
    pos = jnp.stack([row, col], axis=-1)
    n_freq = rot_dim // 4
    inv_freq = ROPE_THETA ** (-jnp.arange(n_freq, dtype=F32) / n_freq)
    ang = pos[:, :, None, None] * inv_freq
    ang = jnp.broadcast_to(ang, (L, 2, 2, n_freq)).reshape(L, rot_dim)
    return jnp.cos(ang), jnp.sin(ang)


def rope_tables_padded(L, rot_dim, bm):
    cos, sin = axial_rope_tables(L, rot_dim)
    cos = jnp.pad(cos, ((0, bm), (0, LANES - rot_dim)), constant_values=1.0)
    sin = jnp.pad(sin, ((0, bm), (0, LANES - rot_dim)))
    return cos, sin


def mla_down(u, w_dq, q_g, w_dkv, kv_g, cos, sin, rows):
    T = rows.n * rows.bm
    bm = rows.bm
    qr = w_dq.shape[1]
    wd = jnp.concatenate([w_dq, w_dkv, jnp.zeros((w_dq.shape[0], LANES - MLA_ROPE), w_dq.dtype)],
                         axis=1).astype(BF16)
    n_all = wd.shape[1]
    c0 = qr + MLA_KV_RANK

    def epi(acc, ex, out):
        qg_ref, kvg_ref, cos_ref, sin_ref = ex
        out[0][...] = _rms(acc[:, :qr], qg_ref[...], NORM_EPS).astype(BF16)
        out[1][...] = _rms(acc[:, qr:c0], kvg_ref[...], NORM_EPS).astype(BF16)
        kr = acc[:, c0:]
        out[2][...] = (kr * cos_ref[...] + _rot_half(kr, MLA_ROPE // 4) * sin_ref[...]).astype(BF16)

    extras = [(q_g.reshape(1, qr), (1, qr), lambda i, j, k: (0, 0)),
              (kv_g.reshape(1, MLA_KV_RANK), (1, MLA_KV_RANK), lambda i, j, k: (0, 0)),
              (cos, (bm, LANES), lambda i, j, k: (rows.pos_block(i), 0)),
              (sin, (bm, LANES), lambda i, j, k: (rows.pos_block(i), 0))]
    outs = [((T, qr), BF16, (bm, qr), lambda i, j, k: (i, 0)),
            ((T, MLA_KV_RANK), BF16, (bm, MLA_KV_RANK), lambda i, j, k: (i, 0)),
            ((T, LANES), BF16, (bm, LANES), lambda i, j, k: (i, 0))]
    return fused_matmul(u, wd, extras, outs, epi, rows=rows, bn=n_all, name="mla_down")


def mla_queries(cq, w_uq, cos, sin, rows, heads):
    T = rows.n * rows.bm
    bm = rows.bm
    qr = w_uq.shape[0]
    hd = MLA_NOPE + MLA_ROPE
    w = w_uq.reshape(qr, heads, hd)
    w = jnp.pad(w, ((0, 0), (0, 0), (0, MLA_HEAD_PAD - hd))).reshape(qr, heads * MLA_HEAD_PAD).astype(BF16)
    scale = hd ** -0.5 * LOG2E
    bn = COL_TILE

    def epi(acc, ex, out):
        cos_ref, sin_ref = ex
        for hh in range(bn // MLA_HEAD_PAD):
            c = hh * MLA_HEAD_PAD
            out[0][:, c:c + LANES] = (acc[:, c:c + LANES] * scale).astype(BF16)
            r = acc[:, c + LANES:c + 2 * LANES]
            r = r * cos_ref[...] + _rot_half(r, MLA_ROPE // 4) * sin_ref[...]
            out[0][:, c + LANES:c + 2 * LANES] = (r * scale).astype(BF16)

    extras = [(cos, (bm, LANES), lambda i, j, k: (rows.pos_block(i), 0)),
              (sin, (bm, LANES), lambda i, j, k: (rows.pos_block(i), 0))]
    outs = [((T, heads * MLA_HEAD_PAD), BF16, (bm, bn), lambda i, j, k: (i, j))]
    return fused_matmul(cq, w, extras, outs, epi, rows=rows, bn=bn, name="mla_queries")[0]


def plain_matmul(a, w, rows, bn, *, name, out_dtype=BF16, bias=None):
    T = rows.n * rows.bm
    N = w.shape[1]
    extras = [] if bias is None else [(bias.reshape(1, N), (1, bn), lambda i, j, k: (0, j))]

    def epi(acc, ex, out):
        y = acc if bias is None else acc + ex[0][...]
        out[0][...] = y.astype(out_dtype)

    return fused_matmul(a, w, extras, [((T, N), out_dtype, (rows.bm, bn), lambda i, j, k: (i, j))], epi,
                        rows=rows, bn=bn, name=name)[0]


def mla_attention(q, kv, kr, B, C, L, heads, ctx_queries):
    hp = 2
    tq = C if ctx_queries else MLA_Q_TILE
    n_t = 1 if ctx_queries else L // tq
    ctx_blk = B * L // C
    S = C if ctx_queries else C + L
    kvw = 2 * LANES
    q_row = (lambda b, g, t: ctx_blk + b) if ctx_queries else (lambda b, g, t: b * n_t + t)

    def kern(*refs):
        if ctx_queries:
            q_ref, kvc_ref, krc_ref, o_ref, k_s, v_s = refs
            parts = [(kvc_ref, krc_ref, 0, C)]
        else:
            q_ref, kvl_ref, kvc_ref, krl_ref, krc_ref, o_ref, k_s, v_s = refs
            parts = [(kvc_ref, krc_ref, 0, C), (kvl_ref, krl_ref, C, S)]

        @pl.when(pl.program_id(2) == 0)
        def _():
            for hh in range(hp):
                c0 = hh * kvw
                for kv_ref, kr_ref, lo, hi in parts:
                    k_s[hh, lo:hi, :LANES] = kv_ref[:, c0:c0 + LANES]
                    k_s[hh, lo:hi, LANES:] = kr_ref[...]
                    v_s[hh, lo:hi, :MLA_V] = kv_ref[:, c0 + LANES:c0 + kvw]
                v_s[hh, :, MLA_V:] = jnp.ones((S, MXU_COLS - MLA_V), BF16)

        ss = [lax.dot_general(q_ref[:, hh * MLA_HEAD_PAD:(hh + 1) * MLA_HEAD_PAD], k_s[hh],
                              (((1,), (1,)), ((), ())), preferred_element_type=F32) for hh in range(hp)]
        ps = [jnp.exp2(s - jnp.max(s, axis=-1, keepdims=True)).astype(BF16) for s in ss]
        for hh in range(hp):
            ov = jnp.dot(ps[hh], v_s[hh], preferred_element_type=F32)
            o_ref[:, hh * MLA_V:(hh + 1) * MLA_V] = (ov[:, :MLA_V] / ov[:, MLA_V:2 * MLA_V]).astype(BF16)

    lat_specs = [pl.BlockSpec((L, hp * kvw), lambda b, g, t: (b, g)),
                 pl.BlockSpec((L, LANES), lambda b, g, t: (b, 0))]
    ctx_specs = [pl.BlockSpec((C, hp * kvw), lambda b, g, t: (ctx_blk + b, g)),
                 pl.BlockSpec((C, LANES), lambda b, g, t: (ctx_blk + b, 0))]
    q_spec = pl.BlockSpec((tq, hp * MLA_HEAD_PAD), lambda b, g, t: (q_row(b, g, t), g))
    if ctx_queries:
        in_specs, args = [q_spec] + ctx_specs, (q, kv, kr)
    else:
        in_specs, args = [q_spec, lat_specs[0], ctx_specs[0], lat_specs[1], ctx_specs[1]], (q, kv, kv, kr, kr)
    return pl.pallas_call(
        kern,
        grid=(B, heads // hp, n_t),
        in_specs=in_specs,
        out_specs=pl.BlockSpec((tq, hp * MLA_V), lambda b, g, t: (b * n_t + t, g)),
        out_shape=jax.ShapeDtypeStruct((B * n_t * tq, heads * MLA_V), BF16),
        scratch_shapes=[pltpu.VMEM((hp, S, MLA_HEAD_PAD), BF16), pltpu.VMEM((hp, S, MXU_COLS), BF16)],
        compiler_params=_params(("parallel", "parallel", "arbitrary")),
        name="mla_attention_ctx" if ctx_queries else "mla_attention",
    )(*args)


def mla_mixer(h, u, mod, layer, p, geo, need_ctx, norm):
    B, C, L, rows_all, rows_out = geo
    heads = p["w_uq"].shape[1] // (MLA_NOPE + MLA_ROPE)
    cos, sin = rope_tables_padded(L, MLA_ROPE, rows_all.bm)
    cq, ckv, kr = mla_down(u, p["w_dq"], p["q_g"], p["w_dkv"], p["kv_g"], cos, sin, rows_all)
    kv = plain_matmul(ckv, p["w_ukv"].astype(BF16), rows_all, p["w_ukv"].shape[1], name="mla_kv")
    q = mla_queries(cq, p["w_uq"], cos, sin, rows_out, heads)
    o = mla_attention(q, kv, kr, B, C, L, heads, False)
    if need_ctx:
        o = (o, mla_attention(q, kv, kr, B, C, L, heads, True))
    return gated_matmul(o, p["w_o"].astype(BF16), h, mod, layer, 2, _out_rows(rows_out), bn=h_width(h), norm=norm,
                        name="mla_out")


def _diff_pair_layout():
    quarter = DF_HEAD_DIM // 4
    n = np.arange(2 * DF_HEAD_DIM)
    hf, c, a, r = n // DF_HEAD_DIM, (n % DF_HEAD_DIM) // (2 * quarter), (n % (2 * quarter)) // quarter, n % quarter
    head_perm = c * DF_HEAD_DIM + a * 2 * quarter + hf * quarter + r
    table_cols = (a * 2 * quarter + r)[:DF_HEAD_DIM]
    return head_perm, table_cols


def diff_rope_tables(L, bm):
    cos, sin = axial_rope_tables(L, DF_HEAD_DIM)
    _, cols = _diff_pair_layout()
    cos = jnp.pad(cos[:, cols], ((0, bm), (0, 0)), constant_values=1.0)
    sin = jnp.pad(sin[:, cols], ((0, bm), (0, 0)))
    return cos, sin


def diff_qkv(u, w_qkv, cos, sin, rows):
    T = rows.n * rows.bm
    bm = rows.bm
    D = w_qkv.shape[0]
    bn = COL_TILE
    hw = 2 * DF_HEAD_DIM
    scale = DF_HEAD_DIM ** -0.5 * LOG2E
    n_q = D // bn
    head_perm, _ = _diff_pair_layout()
    qk_cols = (np.arange(2 * D) // hw * hw)[:, None].reshape(-1, hw) + head_perm[None, :]
    cols = np.concatenate([qk_cols.reshape(-1), np.arange(2 * D, 3 * D)])
    w = w_qkv[:, cols].astype(BF16)

    def epi(acc, ex, out):
        cos_ref, sin_ref = ex
        mul = jnp.where(pl.program_id(1) < n_q, scale, 1.0).astype(F32)
        cs, sn = cos_ref[...] * mul, sin_ref[...] * mul
        for s in range(bn // hw):
            x1 = acc[:, s * hw:s * hw + LANES]
            x2 = acc[:, s * hw + LANES:(s + 1) * hw]
            out[0][:, s * hw:s * hw + LANES] = (x1 * cs - x2 * sn).astype(BF16)
            out[0][:, s * hw + LANES:(s + 1) * hw] = (x2 * cs + x1 * sn).astype(BF16)

    def table_block(i, j, k):
        return jnp.where(j < 2 * n_q, rows.pos_block(i), rows.tiles_per_batch), 0

    extras = [(cos, (bm, LANES), table_block), (sin, (bm, LANES), table_block)]
    outs = [((T, 3 * D), BF16, (bm, bn), lambda i, j, k: (i, j))]
    return fused_matmul(u, w, extras, outs, epi, rows=rows, bn=bn, name="diff_qkv")[0]


def diff_attention(qkv, lambdas, subln_g, lambda_init, B, C, L, D, ctx_queries):
    hd = DF_HEAD_DIM
    hw = 2 * hd
    heads = D // hw
    hp = 2
    tq = C if ctx_queries else DF_Q_TILE
    n_t = 1 if ctx_queries else L // tq
    ctx_blk = B * L // C
    S = C if ctx_queries else C + L
    kcol = D // (hp * hw)
    vcol = 2 * D // (hp * hw)
    q_row = (lambda b, g, t: ctx_blk + b) if ctx_queries else (lambda b, g, t: b * n_t + t)

    def kern(*refs):
        if ctx_queries:
            q_ref, kc_ref, vc_ref, lam_ref, g_ref, o_ref, k_s, v_s = refs
            parts = [(kc_ref, vc_ref, 0, C)]
        else:
            q_ref, kl_ref, kc_ref, vl_ref, vc_ref, lam_ref, g_ref, o_ref, k_s, v_s = refs
            parts = [(kc_ref, vc_ref, 0, C), (kl_ref, vl_ref, C, S)]

        @pl.when(pl.program_id(2) == 0)
        def _():
            map0 = (lax.broadcasted_iota(jnp.int32, (1, hw), 1) % hd) < hd // 2
            for k_ref, v_ref, lo, hi in parts:
                for hh in range(hp):
                    k = k_ref[:, hh * hw:(hh + 1) * hw]
                    k_s[2 * hh, lo:hi, :] = jnp.where(map0, k, jnp.zeros_like(k))
                    k_s[2 * hh + 1, lo:hi, :] = jnp.where(map0, jnp.zeros_like(k), k)
                    v_s[hh, lo:hi, :] = v_ref[:, hh * hw:(hh + 1) * hw]

        lf = lam_ref[...]
        lam = (jnp.exp(jnp.sum(lf[0:1] * lf[1:2], axis=-1, keepdims=True))
               - jnp.exp(jnp.sum(lf[2:3] * lf[3:4], axis=-1, keepdims=True)) + lambda_init)

        nc = 2 * hp
        ss = [lax.dot_general(q_ref[:, (c // 2) * hw:(c // 2 + 1) * hw], k_s[c],
                              (((1,), (1,)), ((), ())), preferred_element_type=F32) for c in range(nc)]
        ms = [jnp.max(s, axis=-1, keepdims=True) for s in ss]
        os_, ls = [0.0] * nc, [0.0] * nc
        for lo in range(0, S, MXU_COLS):
            for c in range(nc):
                p = jnp.exp2(ss[c][:, lo:lo + MXU_COLS] - ms[c])
                ls[c] = ls[c] + jnp.sum(p[:, :LANES] + p[:, LANES:], axis=-1, keepdims=True)
                os_[c] = os_[c] + jnp.dot(p.astype(BF16), v_s[c // 2, lo:lo + MXU_COLS, :],
                                          preferred_element_type=F32)
        for hh in range(hp):
            o = os_[2 * hh] / ls[2 * hh] - lam * (os_[2 * hh + 1] / ls[2 * hh + 1])
            o_ref[:, hh * hw:(hh + 1) * hw] = (_rms(o, g_ref[...], DF_SUBLN_EPS)
                                               * (1.0 - lambda_init)).astype(BF16)

    bw = hp * hw
    q_spec = pl.BlockSpec((tq, bw), lambda b, g, t: (q_row(b, g, t), g))
    lat_specs = [pl.BlockSpec((L, bw), lambda b, g, t: (b, kcol + g)),
                 pl.BlockSpec((L, bw), lambda b, g, t: (b, vcol + g))]
    ctx_specs = [pl.BlockSpec((C, bw), lambda b, g, t: (ctx_blk + b, kcol + g)),
                 pl.BlockSpec((C, bw), lambda b, g, t: (ctx_blk + b, vcol + g))]
    par_specs = [pl.BlockSpec((4, hd), lambda b, g, t: (0, 0)), pl.BlockSpec((1, hw), lambda b, g, t: (0, 0))]
    if ctx_queries:
        in_specs, args = [q_spec] + ctx_specs, (qkv, qkv, qkv)
    else:
        in_specs = [q_spec, lat_specs[0], ctx_specs[0], lat_specs[1], ctx_specs[1]]
        args = (qkv, qkv, qkv, qkv, qkv)
    return pl.pallas_call(
        kern,
        grid=(B, heads // hp, n_t),
        in_specs=in_specs + par_specs,
        out_specs=pl.BlockSpec((tq, bw), lambda b, g, t: (b * n_t + t, g)),
        out_shape=jax.ShapeDtypeStruct((B * n_t * tq, D), BF16),
        scratch_shapes=[pltpu.VMEM((2 * hp, S, hw), BF16), pltpu.VMEM((hp, S, hw), BF16)],
        compiler_params=_params(("parallel", "parallel", "arbitrary")),
        name="diff_attention_ctx" if ctx_queries else "diff_attention",
    )(*args, lambdas, subln_g.reshape(1, hw))


def diff_mixer(h, u, mod, layer, p, geo, need_ctx, lambda_init, norm):
    B, C, L, rows_all, rows_out = geo
    D = h_width(h)
    cos, sin = diff_rope_tables(L, rows_all.bm)
    qkv = diff_qkv(u, p["w_qkv"], cos, sin, rows_all)
    o = diff_attention(qkv, p["lambdas"], p["subln_g"], lambda_init, B, C, L, D, False)
    if need_ctx:
        o = (o, diff_attention(qkv, p["lambdas"], p["subln_g"], lambda_init, B, C, L, D, True))
    return gated_matmul(o, p["w_o"].astype(BF16), h, mod, layer, 2, _out_rows(rows_out), bn=D, norm=norm,
                        name="diff_out")


def dft_matrices(Ls):
    n = 2 * Ls
    t0n = min(64, Ls)
    t1n = Ls // t0n
    f = jnp.arange(Ls, dtype=jnp.int32)[:, None]
    a1 = ((f * (jnp.arange(t1n, dtype=jnp.int32) * t0n)[None, :]) % n).astype(F32) * (2.0 * math.pi / n)
    a0 = ((f * jnp.arange(t0n, dtype=jnp.int32)[None, :]) % n).astype(F32) * (2.0 * math.pi / n)
    c1, s1 = jnp.cos(a1)[:, :, None], jnp.sin(a1)[:, :, None]
    c0, s0 = jnp.cos(a0)[:, None, :], jnp.sin(a0)[:, None, :]
    cosm = (c1 * c0 - s1 * s0).reshape(Ls, Ls)
    sinm = (s1 * c0 + c1 * s0).reshape(Ls, Ls)
    nyq = jnp.where(jnp.arange(Ls) % 2 == 0, 1.0, -1.0).astype(F32)[None, :]
    imag = jnp.where(f == 0, nyq, -sinm)
    fwd = jnp.concatenate([cosm, imag], axis=0)
    col = jnp.arange(n)
    cscale = jnp.where((col == 0) | (col == Ls), 1.0 / n, 2.0 / n).astype(F32)
    inv = fwd.T * cscale[None, :]
    return fwd.astype(BF16), inv.astype(BF16)


def hyena_filter_time(Ls, p):
    D = p["f_bias"].shape[0]
    order = p["f_w2"].shape[0]
    bands = (HY_EMB_DIM - 1) // 2
    t = jnp.linspace(0.0, 1.0, Ls, dtype=F32)[:, None]
    w = 2.0 * math.pi * jnp.arange(Ls, dtype=F32)[:, None] / Ls
    f = jnp.linspace(1e-4, bands - 1, bands, dtype=F32)
    feats = jnp.concatenate([t, jnp.cos(f * w), -jnp.sin(f * w)], axis=-1)
    feats = jnp.pad(feats, ((0, 0), (0, HY_EMB_PAD - HY_EMB_DIM)))
    w1 = jnp.pad(p["f_w1"], ((0, HY_EMB_PAD - HY_EMB_DIM), (0, 0)))
    deltas = jnp.abs(jnp.linspace(math.log(HY_TARGET) / HY_SLOW_PCT, math.log(HY_TARGET) / HY_FAST_PCT,
                                  D, dtype=F32))
    deltas2 = jnp.concatenate([deltas, deltas]).reshape(1, 2 * D)

    def ffn_kern(x_ref, w1_ref, b1_ref, w2_ref, b2_ref, w3_ref, b3_ref, fr_ref, o_ref):
        dot = functools.partial(jnp.dot, precision=HIGHEST, preferred_element_type=F32)
        fr = fr_ref[...]
        hcur = jnp.sin(fr[0:1] * (dot(x_ref[...], w1_ref[...]) + b1_ref[...]))
        hcur = jnp.sin(fr[1:2] * (dot(hcur, w2_ref[...]) + b2_ref[...]))
        o_ref[...] = jnp.sin(fr[2:3] * (dot(hcur, w3_ref[...]) + b3_ref[...]))

    full = lambda a: pl.BlockSpec(a.shape, lambda: (0,) * a.ndim)
    ffn_in = [feats, w1, p["f_b1"].reshape(1, order), p["f_w2"], p["f_b2"].reshape(1, order),
              p["f_w3"], p["f_b3"].reshape(1, order), p["f_freq"]]
    hff = pl.pallas_call(
        ffn_kern,
        in_specs=[full(a) for a in ffn_in],
        out_specs=pl.BlockSpec((Ls, order), lambda: (0, 0)),
        out_shape=jax.ShapeDtypeStruct((Ls, order), F32),
        name="hyena_filter_ffn",
    )(*ffn_in)

    bn = COL_TILE // 4

    def out_kern(h_ref, w_ref, d_ref, o_ref):
        tt = lax.broadcasted_iota(jnp.int32, (Ls, 1), 0).astype(F32) * (1.0 / (Ls - 1))
        hw = jnp.dot(h_ref[...], w_ref[...], precision=HIGHEST, preferred_element_type=F32)
        o_ref[...] = hw * jnp.exp(-tt * d_ref[...])

    return pl.pallas_call(
        out_kern,
        grid=(2 * D // bn,),
        in_specs=[pl.BlockSpec((Ls, order), lambda j: (0, 0)),
                  pl.BlockSpec((order, bn), lambda j: (0, j)),
                  pl.BlockSpec((1, bn), lambda j: (0, j))],
        out_specs=pl.BlockSpec((Ls, bn), lambda j: (0, j)),
        out_shape=jax.ShapeDtypeStruct((Ls, 2 * D), F32),
        compiler_params=_params(("parallel",)),
        name="hyena_filter_out",
    )(hff, p["f_wout"], deltas2)


def hyena_filter_spectrum(hfb, f_bias, fwd, Ls, D):
    n = 2 * Ls
    cb = LANES
    ncb = D // cb

    def kern(w_ref, hf_ref, hb_ref, fb_ref, o_ref):
        row = lax.broadcasted_iota(jnp.int32, (Ls, 1), 0)
        hb = jnp.where(row == 0, 0.0, hb_ref[...])
        ab = jnp.dot(w_ref[...], jnp.concatenate([hf_ref[...], hb], axis=1).astype(BF16),
                     preferred_element_type=F32)
        a, b = ab[:, :cb], ab[:, cb:]
        frow = lax.broadcasted_iota(jnp.int32, (n, 1), 0)
        real = frow <= Ls
        o_ref[...] = a + jnp.where(real, b + fb_ref[...], -b)

    return pl.pallas_call(
        kern,
        grid=(ncb,),
        in_specs=[pl.BlockSpec((n, Ls), lambda j: (0, 0), pipeline_mode=pl.Buffered(1)),
                  pl.BlockSpec((Ls, cb), lambda j: (0, j)),
                  pl.BlockSpec((Ls, cb), lambda j: (0, ncb + j)),
                  pl.BlockSpec((1, cb), lambda j: (0, j))],
        out_specs=pl.BlockSpec((n, cb), lambda j: (0, j)),
        out_shape=jax.ShapeDtypeStruct((n, D), F32),
        compiler_params=_params(("parallel",)),
        name="hyena_filter_spectrum",
    )(fwd, hfb, hfb, f_bias.reshape(1, D))


def _short_conv(x, w, b):
    n = x.shape[0]
    row = lax.broadcasted_iota(jnp.int32, (n, 1), 0)
    prev = jnp.where(row == 0, 0.0, pltpu.roll(x, 1, 0))
    nxt = jnp.where(row == n - 1, 0.0, pltpu.roll(x, n - 1, 0))
    return prev * w[0:1] + x * w[1:2] + nxt * w[2:3] + b


def hyena_segment(z, p, Ls, row_blk0, B, D):
    n = 2 * Ls
    cb = MXU_COLS
    ncb = D // cb
    fwd, inv = dft_matrices(Ls)
    hfb = hyena_filter_time(Ls, p)
    kf = hyena_filter_spectrum(hfb, p["f_bias"], fwd, Ls, D)
    conv_w, conv_b = p["conv_w"], p["conv_b"].reshape(1, 3 * D)

    def fwd_kern(w_ref, x1_ref, v_ref, cw1_ref, cb1_ref, cwv_ref, cbv_ref, kf_ref, y_ref):
        g = (_short_conv(v_ref[...].astype(F32), cwv_ref[...], cbv_ref[...])
             * _short_conv(x1_ref[...].astype(F32), cw1_ref[...], cb1_ref[...])).astype(BF16)
        u = jnp.dot(w_ref[...], g, preferred_element_type=F32)
        ure, uim = u[:Ls], u[Ls:]
        kre, kim = kf_ref[:Ls, :], kf_ref[Ls:, :]
        first = lax.broadcasted_iota(jnp.int32, (Ls, 1), 0) == 0
        y_ref[:Ls, :] = (ure * kre - jnp.where(first, 0.0, uim * kim)).astype(BF16)
        y_ref[Ls:, :] = jnp.where(first, uim * kim, ure * kim + uim * kre).astype(BF16)

    y = pl.pallas_call(
        fwd_kern,
        grid=(ncb, B),
        in_specs=[pl.BlockSpec((n, Ls), lambda j, b: (0, 0), pipeline_mode=pl.Buffered(1)),
                  pl.BlockSpec((Ls, cb), lambda j, b: (row_blk0 + b, ncb + j)),
                  pl.BlockSpec((Ls, cb), lambda j, b: (row_blk0 + b, 2 * ncb + j)),
                  pl.BlockSpec((HY_SHORT, cb), lambda j, b: (0, ncb + j)),
                  pl.BlockSpec((1, cb), lambda j, b: (0, ncb + j)),
                  pl.BlockSpec((HY_SHORT, cb), lambda j, b: (0, 2 * ncb + j)),
                  pl.BlockSpec((1, cb), lambda j, b: (0, 2 * ncb + j)),
                  pl.BlockSpec((n, cb), lambda j, b: (0, j))],
        out_specs=pl.BlockSpec((None, n, cb), lambda j, b: (b, 0, j)),
        out_shape=jax.ShapeDtypeStruct((B, n, D), BF16),
        compiler_params=_params(("parallel", "arbitrary")),
        name="hyena_dft",
    )(fwd, z, z, conv_w, conv_b, conv_w, conv_b, kf)

    def inv_kern(w_ref, y_ref, x0_ref, cw0_ref, cb0_ref, o_ref):
        conv = jnp.dot(w_ref[...], y_ref[...], preferred_element_type=F32)
        o_ref[...] = (conv * _short_conv(x0_ref[...].astype(F32), cw0_ref[...], cb0_ref[...])).astype(BF16)

    return pl.pallas_call(
        inv_kern,
        grid=(ncb, B),
        in_specs=[pl.BlockSpec((Ls, n), lambda j, b: (0, 0), pipeline_mode=pl.Buffered(1)),
                  pl.BlockSpec((None, n, cb), lambda j, b: (b, 0, j)),
                  pl.BlockSpec((Ls, cb), lambda j, b: (row_blk0 + b, j)),
                  pl.BlockSpec((HY_SHORT, cb), lambda j, b: (0, j)),
                  pl.BlockSpec((1, cb), lambda j, b: (0, j))],
        out_specs=pl.BlockSpec((Ls, cb), lambda j, b: (b, j)),
        out_shape=jax.ShapeDtypeStruct((B * Ls, D), BF16),
        compiler_params=_params(("parallel", "arbitrary")),
        name="hyena_idft",
    )(inv, y, z, conv_w, conv_b)


def hyena_mixer(h, u, mod, layer, p, geo, need_ctx, norm):
    B, C, L, rows_all, rows_out = geo
    D = h_width(h)
    z = plain_matmul(u, p["w_in"].astype(BF16), rows_out, COL_TILE, bias=p["b_in"], name="hyena_in")
    y = hyena_segment(z, p, L, 0, B, D)
    if need_ctx:
        y = (y, hyena_segment(z, p, C, B * L // C, B, D))
    return gated_matmul(y, p["w_out"].astype(BF16), h, mod, layer, 2, _out_rows(rows_out), bn=D, bias=p["b_out"],
                        norm=norm, name="hyena_out")


def kernel(x, c, ctx, c_ctx, ada_w, ada_b, norm_g, mlp_w1, mlp_w2, final_norm_g, mla_w_dq, mla_q_norm_g, mla_w_uq, mla_w_dkv, mla_kv_norm_g, mla_w_ukv, mla_w_o, hy_w_in, hy_b_in, hy_conv_w, hy_conv_b, hy_filt_w1, hy_filt_b1, hy_filt_w2, hy_filt_b2, hy_filt_w3, hy_filt_b3, hy_filt_freq, hy_filt_wout, hy_filt_bias, hy_w_out, hy_b_out, df_w_qkv, df_lambda, df_subln_g, df_w_o):
    B, L, D = x.shape
    C = ctx.shape[1]
    depth = ada_w.shape[0]
    bm = min(ROW_TILE, B * C)
    rows_all = Rows(B, C, L, bm)
    rows_lat = Rows(B, C, L, bm, lat_only=True)

    mod = adaln_table(jnp.concatenate([c_ctx[None, :], c], axis=0), ada_w, ada_b)
    norm_g4 = norm_g.reshape(depth, 2, 1, D)
    w1s, w2s = mlp_w1.astype(BF16), mlp_w2.astype(BF16)
    h = (x.reshape(B * L, D), ctx.reshape(B * C, D))
    u = norm_modulate(h, norm_g4, mod, 0, 0, rows_all)

    for i in range(depth):
        need_ctx = i < depth - 1
        rows_out = rows_all if need_ctx else rows_lat
        geo = (B, C, L, rows_all, rows_out)
        kind, j = i % N_MIXERS, i // N_MIXERS
        mlp_norm = ("modulate", norm_g4, i, 1)
        if kind == 0:
            p = dict(w_dq=mla_w_dq[j], q_g=mla_q_norm_g[j], w_uq=mla_w_uq[j], w_dkv=mla_w_dkv[j],
                     kv_g=mla_kv_norm_g[j], w_ukv=mla_w_ukv[j], w_o=mla_w_o[j])
            h, u = mla_mixer(h, u, mod, i, p, geo, need_ctx, mlp_norm)
        elif kind == 1:
            p = dict(w_in=hy_w_in[j], b_in=hy_b_in[j], conv_w=hy_conv_w[j], conv_b=hy_conv_b[j],
                     f_w1=hy_filt_w1[j], f_b1=hy_filt_b1[j], f_w2=hy_filt_w2[j], f_b2=hy_filt_b2[j],
                     f_w3=hy_filt_w3[j], f_b3=hy_filt_b3[j], f_freq=hy_filt_freq[j], f_wout=hy_filt_wout[j],
                     f_bias=hy_filt_bias[j], w_out=hy_w_out[j], b_out=hy_b_out[j])
            h, u = hyena_mixer(h, u, mod, i, p, geo, need_ctx, mlp_norm)
        else:
            lambda_init = 0.8 - 0.6 * math.exp(-0.3 * i)
            p = dict(w_qkv=df_w_qkv[j], lambdas=df_lambda[j], subln_g=df_subln_g[j], w_o=df_w_o[j])
            h, u = diff_mixer(h, u, mod, i, p, geo, need_ctx, lambda_init, mlp_norm)
        next_norm = ("modulate", norm_g4, i + 1, 0) if need_ctx else ("final", final_norm_g)
        res = mlp(h, u, mod, i, (w1s, i), (w2s, i), rows_out, next_norm)
        if need_ctx:
            h, u = res

    return res.reshape(B, L, D)
```
